```python
import math
import jax, jax.numpy as jnp
from jax import lax
import numpy as np

D_MODEL = 2048
BATCH = 1
SEQ = 8192
DEPTH = 2

CHUNK = 64
N_BRANCH = 3
EPS = 1e-6

HG_DK = 128
HG_DV = 128
HG_WIDTH = D_MODEL // 2
HG_HEADS = HG_WIDTH // HG_DK
HG_VWIDTH = HG_HEADS * HG_DV

S5_WIDTH = D_MODEL // 2
S5_GROUP = 16
S5_GROUPS = S5_WIDTH // S5_GROUP
S5_STATE = 64

FOX_DH = 128
FOX_WIDTH = D_MODEL // 2
FOX_HEADS = FOX_WIDTH // FOX_DH
Q_BLOCK = 128

IN_SIZES = (HG_WIDTH, HG_WIDTH, HG_VWIDTH, HG_VWIDTH,
            S5_WIDTH, S5_WIDTH,
            FOX_WIDTH, FOX_WIDTH, FOX_WIDTH, FOX_HEADS, FOX_WIDTH,
            N_BRANCH * D_MODEL)
IN_WIDTH = sum(IN_SIZES)

kernel_name = 'hybrid_hgrn2_s5_fox_gated_merge'


def rms_norm(x, g):
    xf = x.astype(jnp.float32)
    y = xf * lax.rsqrt(jnp.mean(xf * xf, axis=-1, keepdims=True) + EPS)
    return (y * g.astype(jnp.float32)).astype(x.dtype)


def split_columns(z):
    parts = []
    off = 0
    for n in IN_SIZES:
        parts.append(z[..., off:off + n])
        off += n
    return parts


def hgrn_lower_bounds(lb_param):
    p = jax.nn.softmax(lb_param.astype(jnp.float32), axis=0)
    c = jnp.cumsum(p, axis=0)
    return c - c[0:1]


def hgrn2_mixer(q, f_logit, i, lb):
    b_, s_, _ = q.shape
    nc = s_ // CHUNK
    f32 = jnp.float32
    lbf = lb.astype(f32)
    zf = f_logit.astype(f32)
    log_f = jnp.logaddexp(jnp.log(lbf), jnp.log1p(-lbf) + jax.nn.log_sigmoid(zf))
    k = (1.0 - lbf) * jax.nn.sigmoid(-zf)

    def to_chunks(t, d):
        return t.astype(f32).reshape(b_, nc, CHUNK, HG_HEADS, d).transpose(1, 0, 3, 2, 4)

    xs = (to_chunks(q, HG_DK), to_chunks(k, HG_DK), to_chunks(i, HG_DV), to_chunks(log_f, HG_DK))
    causal = jnp.tril(jnp.ones((CHUNK, CHUNK), dtype=bool))[:, :, None]

    def step(state, inp):
        qb, kb, ib, gb = inp
        cum = jnp.cumsum(gb, axis=2)
        last = cum[:, :, -1:, :]
        diff = cum[:, :, :, None, :] - cum[:, :, None, :, :]
        decay = jnp.exp(jnp.where(causal, diff, -jnp.inf))
        scores = jnp.einsum('bhtk,bhtsk,bhsk->bhts', qb, decay, kb)
        out = (jnp.einsum('bhts,bhsv->bhtv', scores, ib)
               + jnp.einsum('bhtk,bhkv->bhtv', qb * jnp.exp(cum), state))
        new_state = (jnp.exp(last[:, :, 0, :, None]) * state
                     + jnp.einsum('bhsk,bhsv->bhkv', kb * jnp.exp(last - cum), ib))
        return new_state, out

    s0 = jnp.zeros((b_, HG_HEADS, HG_DK, HG_DV), f32)
    _, o = lax.scan(step, s0, xs)
    return o.transpose(1, 0, 3, 2, 4).reshape(b_, s_, HG_HEADS, HG_DV)


def s5_mixer(u, a_re, a_im, log_dt, b_re, b_im, c_re, c_im, d_skip, w_glu):
    b_, s_, _ = u.shape
    f32 = jnp.float32
    uf = u.astype(f32).reshape(b_, s_, S5_GROUPS, S5_GROUP)
    ar = a_re.astype(f32)
    ai = a_im.astype(f32)
    dt = jnp.exp(log_dt.astype(f32))[:, None]
    mag = jnp.exp(ar * dt)
    ang = ai * dt
    abar_re = mag * jnp.cos(ang)
    abar_im = mag * jnp.sin(ang)
    nr = abar_re - 1.0
    den = ar * ar + ai * ai
    zr = (nr * ar + abar_im * ai) / den
    zi = (abar_im * ar - nr * ai) / den
    br = b_re.astype(f32)
    bi = b_im.astype(f32)
    bbar_re = zr[:, :, None] * br - zi[:, :, None] * bi
    bbar_im = zr[:, :, None] * bi + zi[:, :, None] * br
    bu_re = jnp.einsum('bsgc,gpc->bsgp', uf, bbar_re)
    bu_im = jnp.einsum('bsgc,gpc->bsgp', uf, bbar_im)
    a_full_re = jnp.broadcast_to(abar_re, bu_re.shape)
    a_full_im = jnp.broadcast_to(abar_im, bu_im.shape)

    def combine(e1, e2):
        a1r, a1i, b1r, b1i = e1
        a2r, a2i, b2r, b2i = e2
        return (a2r * a1r - a2i * a1i,
                a2r * a1i + a2i * a1r,
                a2r * b1r - a2i * b1i + b2r,
                a2r * b1i + a2i * b1r + b2i)

    _, _, xr, xi = lax.associative_scan(combine, (a_full_re, a_full_im, bu_re, bu_im), axis=1)
    y = (jnp.einsum('bsgp,gcp->bsgc', xr, c_re.astype(f32))
         - jnp.einsum('bsgp,gcp->bsgc', xi, c_im.astype(f32))
         + d_skip.astype(f32).reshape(S5_GROUPS, S5_GROUP) * uf)
    y = jax.nn.gelu(y.reshape(b_, s_, S5_WIDTH))
    zg = jnp.einsum('bsw,wv->bsv', y, w_glu.astype(f32))
    return zg[..., :S5_WIDTH] * jax.nn.sigmoid(zg[..., S5_WIDTH:])


def fox_mixer(q, k, v, f_logit):
    b_, s_, _ = q.shape
    f32 = jnp.float32
    nb = s_ // Q_BLOCK

    def heads(t):
        return t.astype(f32).reshape(b_, s_, FOX_HEADS, FOX_DH).transpose(0, 2, 1, 3)

    qh, kh, vh = heads(q), heads(k), heads(v)
    c = jnp.cumsum(jax.nn.log_sigmoid(f_logit.astype(f32)), axis=1).transpose(0, 2, 1)
    pos = jnp.arange(s_)
    scale = FOX_DH ** -0.5
    q_blocks = qh.reshape(b_, FOX_HEADS, nb, Q_BLOCK, FOX_DH).transpose(2, 0, 1, 3, 4)
    c_blocks = c.reshape(b_, FOX_HEADS, nb, Q_BLOCK).transpose(2, 0, 1, 3)
    pos_blocks = pos.reshape(nb, Q_BLOCK)

    def block(args):
        qb, cb, pb = args
        s = (jnp.einsum('bhtd,bhsd->bhts', qb, kh) * scale
             + cb[..., None] - c[:, :, None, :])
        s = jnp.where(pb[:, None] >= pos[None, :], s, -jnp.inf)
        p = jax.nn.softmax(s, axis=-1)
        return jnp.einsum('bhts,bhsd->bhtd', p, vh)

    o = lax.map(block, (q_blocks, c_blocks, pos_blocks))
    return o.transpose(1, 0, 3, 2, 4).reshape(b_, s_, FOX_WIDTH)


def hybrid_layer(x, norm_g, w_in, b_gate, fox_bf, lb, hg_norm_g,
                 s5_a_re, s5_a_im, s5_log_dt, s5_b_re, s5_b_im, s5_c_re, s5_c_im,
                 s5_d, s5_w_glu, w_br_a, w_br_b, w_br_c, w_out):
    b_, s_, _ = x.shape
    dt = x.dtype
    h = rms_norm(x, norm_g)
    z = jnp.einsum('bsd,de->bse', h, w_in)
    (hq, hf, hi, hgate, su, sgate, fq, fk, fv, ff, fgate, mg) = split_columns(z)

    o_a = hgrn2_mixer(hq, hf, hi, lb)
    o_a = o_a * lax.rsqrt(jnp.mean(o_a * o_a, axis=-1, keepdims=True) + EPS)
    o_a = o_a.reshape(b_, s_, HG_VWIDTH) * hg_norm_g.astype(jnp.float32)
    o_a = (o_a * jax.nn.silu(hgate.astype(jnp.float32))).astype(dt)

    o_b = s5_mixer(su, s5_a_re, s5_a_im, s5_log_dt, s5_b_re, s5_b_im,
                   s5_c_re, s5_c_im, s5_d, s5_w_glu)
    o_b = (o_b * jax.nn.silu(sgate.astype(jnp.float32))).astype(dt)

    o_c = fox_mixer(fq, fk, fv, ff + fox_bf)
    o_c = (o_c * jax.nn.silu(fgate.astype(jnp.float32))).astype(dt)

    gates = jax.nn.sigmoid((mg + b_gate).astype(jnp.float32)).reshape(b_, s_, N_BRANCH, D_MODEL)
    merged = (gates[:, :, 0] * jnp.einsum('bsw,wd->bsd', o_a, w_br_a).astype(jnp.float32)
              + gates[:, :, 1] * jnp.einsum('bsw,wd->bsd', o_b, w_br_b).astype(jnp.float32)
              + gates[:, :, 2] * jnp.einsum('bsw,wd->bsd', o_c, w_br_c).astype(jnp.float32))
    return x + jnp.einsum('bsd,de->bse', merged.astype(dt), w_out).astype(dt)


def setup_inputs(seed: int = 0) -> dict:
    key = jax.random.key(seed)
    ks = jax.random.split(key, 24)
    f32 = jnp.float32
    nrm = lambda k, shape, s: jax.random.normal(k, shape, f32) * s
    n_idx = jnp.arange(S5_STATE, dtype=f32)
    return {
        'x': nrm(ks[0], (BATCH, SEQ, D_MODEL), 1.0),
        'norm_g': 1.0 + nrm(ks[1], (DEPTH, D_MODEL), 0.02),
        'w_in': nrm(ks[2], (DEPTH, D_MODEL, IN_WIDTH), D_MODEL ** -0.5),
        'b_gate': nrm(ks[3], (DEPTH, N_BRANCH * D_MODEL), 0.02),
        'fox_bf': jax.random.uniform(ks[4], (DEPTH, FOX_HEADS), f32, 1.0, 4.0),
        'hg_lb': nrm(ks[5], (DEPTH, HG_WIDTH), 0.1),
        'hg_norm_g': 1.0 + nrm(ks[6], (DEPTH, HG_VWIDTH), 0.02),
        's5_a_re': -0.5 + nrm(ks[7], (DEPTH, S5_GROUPS, S5_STATE), 0.01),
        's5_a_im': math.pi * n_idx + nrm(ks[8], (DEPTH, S5_GROUPS, S5_STATE), 0.01),
        's5_log_dt': jax.random.uniform(ks[9], (DEPTH, S5_GROUPS), f32, math.log(1e-3), math.log(1e-1)),
        's5_b_re': nrm(ks[10], (DEPTH, S5_GROUPS, S5_STATE, S5_GROUP), (2 * S5_GROUP) ** -0.5),
        's5_b_im': nrm(ks[11], (DEPTH, S5_GROUPS, S5_STATE, S5_GROUP), (2 * S5_GROUP) ** -0.5),
        's5_c_re': nrm(ks[12], (DEPTH, S5_GROUPS, S5_GROUP, S5_STATE), (2 * S5_STATE) ** -0.5),
        's5_c_im': nrm(ks[13], (DEPTH, S5_GROUPS, S5_GROUP, S5_STATE), (2 * S5_STATE) ** -0.5),
        's5_d': nrm(ks[14], (DEPTH, S5_WIDTH), 1.0),
        's5_w_glu': nrm(ks[15], (DEPTH, S5_WIDTH, 2 * S5_WIDTH), S5_WIDTH ** -0.5),
        'w_br_a': nrm(ks[16], (DEPTH, HG_VWIDTH, D_MODEL), HG_VWIDTH ** -0.5),
        'w_br_b': nrm(ks[17], (DEPTH, S5_WIDTH, D_MODEL), S5_WIDTH ** -0.5),
        'w_br_c': nrm(ks[18], (DEPTH, FOX_WIDTH, D_MODEL), FOX_WIDTH ** -0.5),
        'w_out': nrm(ks[19], (DEPTH, D_MODEL, D_MODEL), D_MODEL ** -0.5),
        'final_g': 1.0 + nrm(ks[20], (D_MODEL,), 0.02),
    }


def reference(x, norm_g, w_in, b_gate, fox_bf, hg_lb, hg_norm_g, s5_a_re, s5_a_im,
              s5_log_dt, s5_b_re, s5_b_im, s5_c_re, s5_c_im, s5_d, s5_w_glu,
              w_br_a, w_br_b, w_br_c, w_out, final_g):
    lbs = hgrn_lower_bounds(hg_lb)
    for l in range(DEPTH):
        x = hybrid_layer(x, norm_g[l], w_in[l], b_gate[l], fox_bf[l], lbs[l], hg_norm_g[l],
                         s5_a_re[l], s5_a_im[l], s5_log_dt[l], s5_b_re[l], s5_b_im[l],
                         s5_c_re[l], s5_c_im[l], s5_d[l], s5_w_glu[l],
                         w_br_a[l], w_br_b[l], w_br_c[l], w_out[l])
    return rms_norm(x, final_g)
```

```python
import functools
import math

import numpy as np
import jax
import jax.numpy as jnp
from jax import lax
from jax.experimental import pallas as pl
from jax.experimental.pallas import tpu as pltpu

F32 = jnp.float32
BF16 = jnp.bfloat16

EPS = 1e-6
LANES = 128
VMEM_LIMIT = 56 * 1024 * 1024

HEAD_DIM = 128
S5_GROUP = 16
S5_STATE = 64
S5_GPB = 8
N_BRANCH = 3

HG_SUB = 16


def _cparams(sem):
    return pltpu.CompilerParams(dimension_semantics=sem, vmem_limit_bytes=VMEM_LIMIT)


def _dot(a, b):
    return jnp.dot(a, b, preferred_element_type=F32)


def _dot_nt(a, b):
    return lax.dot_general(a, b, (((1,), (1,)), ((), ())), preferred_element_type=F32)


def _dot_tn(a, b):
    return lax.dot_general(a, b, (((0,), (0,)), ((), ())), preferred_element_type=F32)


def _split3(x):
    hi = x.astype(BF16)
    r1 = x - hi.astype(F32)
    mid = r1.astype(BF16)
    lo = (r1 - mid.astype(F32)).astype(BF16)
    return hi, mid, lo


def _dot01(m, x):
    hi, mid, lo = _split3(x)
    return _dot(m, hi) + _dot(m, mid) + _dot(m, lo)


def _log_sigmoid(z):
    return jnp.minimum(z, 0.0) - jnp.log1p(jnp.exp(-jnp.abs(z)))


def _sigmoid(z):
    return 1.0 / (1.0 + jnp.exp(-z))


def _silu(z):
    return z * _sigmoid(z)


def _rmsnorm_kernel(x_ref, g_ref, o_ref):
    x = x_ref[...]
    ms = jnp.mean(x * x, axis=-1, keepdims=True)
    o_ref[...] = (x * lax.rsqrt(ms + EPS) * g_ref[...]).astype(o_ref.dtype)


def _rmsnorm(x, g, out_dtype, tm=512):
    m, d = x.shape
    return pl.pallas_call(
        _rmsnorm_kernel,
        grid=(m // tm,),
        in_specs=[pl.BlockSpec((tm, d), lambda i: (i, 0)),
                  pl.BlockSpec((1, d), lambda i: (0, 0))],
        out_specs=pl.BlockSpec((tm, d), lambda i: (i, 0)),
        out_shape=jax.ShapeDtypeStruct((m, d), out_dtype),
        compiler_params=_cparams(("parallel",)),
        name="rmsnorm",
    )(x, g.reshape(1, d))


def _inproj_kernel(h_ref, w_ref, o_ref):
    o_ref[...] = _dot(h_ref[...], w_ref[...])


def _inproj(h, w, tm=512, tn=1024):
    m, k = h.shape
    n = w.shape[1]
    return pl.pallas_call(
        _inproj_kernel,
        grid=(n // tn, m // tm),
        in_specs=[pl.BlockSpec((tm, k), lambda j, i: (i, 0)),
                  pl.BlockSpec((k, tn), lambda j, i: (0, j))],
        out_specs=pl.BlockSpec((tm, tn), lambda j, i: (i, j)),
        out_shape=jax.ShapeDtypeStruct((m, n), F32),
        compiler_params=_cparams(("parallel", "arbitrary")),
        name="inproj",
    )(h, w)


def _foxc_kernel(h_ref, w_ref, b_ref, tri_ref, c_ref, carry_ref):
    @pl.when(pl.program_id(0) == 0)
    def _():
        carry_ref[...] = jnp.zeros_like(carry_ref)

    logits = _dot(h_ref[...], w_ref[...]) + b_ref[...]
    ls = _log_sigmoid(logits)
    cum = _dot01(tri_ref[...], ls) + carry_ref[0:1, :]
    c_ref[...] = cum
    tm = cum.shape[0]
    carry_ref[...] = jnp.broadcast_to(cum[tm - 1:tm, :], carry_ref.shape)


def _fox_cumlog(h, w_ff, b_ff, tm=512):
    m, k = h.shape
    tri = jnp.asarray(np.tril(np.ones((tm, tm), np.float32)), BF16)
    return pl.pallas_call(
        _foxc_kernel,
        grid=(m // tm,),
        in_specs=[pl.BlockSpec((tm, k), lambda i: (i, 0)),
                  pl.BlockSpec((k, LANES), lambda i: (0, 0)),
                  pl.BlockSpec((1, LANES), lambda i: (0, 0)),
                  pl.BlockSpec((tm, tm), lambda i: (0, 0))],
        out_specs=pl.BlockSpec((tm, LANES), lambda i: (i, 0)),
        out_shape=jax.ShapeDtypeStruct((m, LANES), F32),
        scratch_shapes=[pltpu.VMEM((8, LANES), F32)],
        compiler_params=_cparams(("arbitrary",)),
        name="fox_cumlog",
    )(h, w_ff, b_ff, tri)


NEG_BIG = -1e30


def _fox_attn_kernel(q_ref, k_ref, v_ref, ccol_ref, crow_ref, gate_ref, o_ref,
                     kb_ref, vb_ref, *, tq, tk, scale):
    qi = pl.program_id(1)

    @pl.when(qi == 0)
    def _():
        kb_ref[...] = k_ref[...].astype(BF16)
        vb_ref[...] = v_ref[...].astype(BF16)

    q = (q_ref[...] * scale).astype(BF16)
    ct = ccol_ref[0]
    row = lax.broadcasted_iota(jnp.int32, (tq, tk), 0)
    col = lax.broadcasted_iota(jnp.int32, (tq, tk), 1)

    def step(kb, carry, masked):
        m, l, acc = carry
        k0 = pl.multiple_of(kb * tk, tk)
        s = _dot_nt(q, kb_ref[pl.ds(k0, tk), :])
        s = s - crow_ref[0, :, pl.ds(k0, tk)]
        if masked:
            s = jnp.where(col + k0 <= row + qi * tq, s, NEG_BIG)
        m_new = jnp.maximum(m, jnp.max(s, axis=1, keepdims=True) + ct)
        p = jnp.exp(s - (m_new - ct))
        alpha = jnp.exp(m - m_new)
        l = alpha * l + jnp.sum(p, axis=1, keepdims=True)
        acc = alpha * acc + _dot(p.astype(BF16), vb_ref[pl.ds(k0, tk), :])
        return m_new, l, acc

    init = (jnp.full((tq, 1), NEG_BIG, F32), jnp.zeros((tq, 1), F32),
            jnp.zeros((tq, HEAD_DIM), F32))
    n_diag = tq // tk
    carry = lax.fori_loop(0, qi * n_diag, lambda kb, c: step(kb, c, False), init)
    for d in range(n_diag):
        carry = step(qi * n_diag + d, carry, True)
    _, l, acc = carry
    out = acc * (1.0 / l)
    o_ref[...] = (out * _silu(gate_ref[...])).astype(o_ref.dtype)


def _fox_attn(z, ccol, crow, q_off, k_off, v_off, g_off, n_heads, tq=512, tk=512):
    s_len = z.shape[0]
    qo, ko, vo, go = (o // HEAD_DIM for o in (q_off, k_off, v_off, g_off))
    kern = functools.partial(_fox_attn_kernel, tq=tq, tk=tk, scale=HEAD_DIM ** -0.5)
    return pl.pallas_call(
        kern,
        grid=(n_heads, s_len // tq),
        in_specs=[pl.BlockSpec((tq, HEAD_DIM), lambda h, i: (i, qo + h)),
                  pl.BlockSpec((s_len, HEAD_DIM), lambda h, i: (0, ko + h)),
                  pl.BlockSpec((s_len, HEAD_DIM), lambda h, i: (0, vo + h)),
                  pl.BlockSpec((1, tq, 1), lambda h, i: (h, i, 0)),
                  pl.BlockSpec((1, 1, s_len), lambda h, i: (h, 0, 0)),
                  pl.BlockSpec((tq, HEAD_DIM), lambda h, i: (i, go + h))],
        out_specs=pl.BlockSpec((tq, HEAD_DIM), lambda h, i: (i, h)),
        out_shape=jax.ShapeDtypeStruct((s_len, n_heads * HEAD_DIM), BF16),
        scratch_shapes=[pltpu.VMEM((s_len, HEAD_DIM), BF16),
                        pltpu.VMEM((s_len, HEAD_DIM), BF16)],
        compiler_params=_cparams(("parallel", "arbitrary")),
        name="fox_attn",
    )(z, z, z, ccol, crow, z)


def _hgrn_kernel(q_ref, f_ref, i_ref, gate_ref, lb_ref, ng_ref, tri_ref, blk_ref, ones_ref,
                 o_ref, st_ref, qt_ref, kt_ref, kk_ref, cum_ref, dd_ref, p_ref, acc_ref,
                 *, layer, rows):
    @pl.when(pl.program_id(1) == 0)
    def _():
        st_ref[...] = jnp.zeros_like(st_ref)

    z = f_ref[...]
    ls = _log_sigmoid(z)
    if layer == 0:
        g = ls
        kk = _sigmoid(-z)
    else:
        lbp = lb_ref[...]
        e = jnp.exp(lbp - jnp.max(lbp, axis=0, keepdims=True))
        p = e / jnp.sum(e, axis=0, keepdims=True)
        lb = jnp.sum(p[1:layer + 1, :], axis=0, keepdims=True)
        a = jnp.log(lb)
        b = jnp.log1p(-lb) + ls
        g = jnp.maximum(a, b) + jnp.log1p(jnp.exp(-jnp.abs(a - b)))
        kk = (1.0 - lb) * _sigmoid(-z)

    cum = _dot01(tri_ref[...], g)
    tot = _dot01(blk_ref[...], g)
    q = q_ref[...]
    qt_ref[...] = (q * jnp.exp(cum)).astype(BF16)
    kt_ref[...] = (kk * jnp.exp(tot - cum)).astype(BF16)
    kk_ref[...] = kk
    cum_ref[...] = cum
    dd_ref[...] = jnp.exp(tot)

    rid = lax.broadcasted_iota(jnp.int32, (HG_SUB, HEAD_DIM), 0)

    def group(j, carry):
        r0 = pl.multiple_of(j * HG_SUB, HG_SUB)
        rs = pl.ds(r0, HG_SUB)
        qj = q_ref[rs, :]
        kj = kk_ref[rs, :]
        cj = cum_ref[rs, :]
        vj = i_ref[rs, :]
        st = st_ref[...]
        o_inter = _dot_nt(qt_ref[rs, :], st.astype(BF16))
        for s in range(HG_SUB):
            d = cj - cj[s:s + 1, :]
            e = jnp.where(rid >= s, jnp.exp(d), 0.0)
            p_ref[s * HG_SUB:(s + 1) * HG_SUB, :] = (qj * kj[s:s + 1, :] * e).astype(BF16)
        sc = _dot(p_ref[...], ones_ref[...])
        o_intra = sc[0:HG_SUB, :] * vj[0:1, :]
        for s in range(1, HG_SUB):
            o_intra = o_intra + sc[s * HG_SUB:(s + 1) * HG_SUB, :] * vj[s:s + 1, :]
        acc_ref[rs, :] = o_inter + o_intra
        upd = _dot_tn(vj.astype(BF16), kt_ref[rs, :])
        st_ref[...] = st * dd_ref[pl.ds(r0, 8), :][0:1, :] + upd
        return carry

    lax.fori_loop(0, rows // HG_SUB, group, 0)

    o = acc_ref[...]
    ms = jnp.mean(o * o, axis=-1, keepdims=True)
    o = o * lax.rsqrt(ms + EPS) * ng_ref[...]
    o_ref[...] = (o * _silu(gate_ref[...])).astype(o_ref.dtype)


def _hgrn(z, hg_lb, norm_g, layer, q_off, f_off, i_off, g_off, n_heads, rows=256):
    s_len = z.shape[0]
    depth = hg_lb.shape[0]
    qo, fo, io, go = (o // HEAD_DIM for o in (q_off, f_off, i_off, g_off))
    r = np.arange(rows)
    same = (r[:, None] // HG_SUB) == (r[None, :] // HG_SUB)
    tri = jnp.asarray((same & (r[None, :] <= r[:, None])).astype(np.float32), BF16)
    blk = jnp.asarray(same.astype(np.float32), BF16)
    ones = jnp.ones((HEAD_DIM, HEAD_DIM), BF16)
    kern = functools.partial(_hgrn_kernel, layer=layer, rows=rows)
    blk_spec = lambda off: pl.BlockSpec((rows, HEAD_DIM), lambda h, i: (i, off + h))
    const = lambda shape: pl.BlockSpec(shape, lambda h, i: (0, 0))
    return pl.pallas_call(
        kern,
        grid=(n_heads, s_len // rows),
        in_specs=[blk_spec(qo), blk_spec(fo), blk_spec(io), blk_spec(go),
                  pl.BlockSpec((depth, HEAD_DIM), lambda h, i: (0, h)),
                  pl.BlockSpec((1, HEAD_DIM), lambda h, i: (0, h)),
                  const((rows, rows)), const((rows, rows)), const((HEAD_DIM, HEAD_DIM))],
        out_specs=pl.BlockSpec((rows, HEAD_DIM), lambda h, i: (i, h)),
        out_shape=jax.ShapeDtypeStruct((s_len, n_heads * HEAD_DIM), BF16),
        scratch_shapes=[pltpu.VMEM((HEAD_DIM, HEAD_DIM), F32),
                        pltpu.VMEM((rows, HEAD_DIM), BF16),
                        pltpu.VMEM((rows, HEAD_DIM), BF16),
                        pltpu.VMEM((rows, HEAD_DIM), F32),
                        pltpu.VMEM((rows, HEAD_DIM), F32),
                        pltpu.VMEM((rows, HEAD_DIM), F32),
                        pltpu.VMEM((HG_SUB * HG_SUB, HEAD_DIM), BF16),
                        pltpu.VMEM((rows, HEAD_DIM), F32)],
        compiler_params=_cparams(("parallel", "arbitrary")),
        name="hgrn2",
    )(z, z, z, z, hg_lb, norm_g.reshape(1, -1), tri, blk, ones)


def _gelu_tanh(x):
    c = math.sqrt(2.0 / math.pi)
    return 0.5 * x * (1.0 + jnp.tanh(c * (x + 0.044715 * (x * x * x))))


def _s5_kernel(u_ref, perm_ref, permt_ref, bm_ref, cm_ref, are_ref, aim_ref, alre_ref, alim_ref,
               d_ref, o_ref, state_ref, bu_ref, x_ref, yp_ref, *, rows, half):
    @pl.when(pl.program_id(0) == 0)
    def _():
        state_ref[...] = jnp.zeros_like(state_ref)

    nblk = bm_ref.shape[0]
    nt = rows // 8
    u = u_ref[...]
    up = _dot(perm_ref[...], u.astype(BF16)).astype(BF16)
    sub = lax.broadcasted_iota(jnp.int32, (8, half), 0)

    for b in range(nblk):
        bu_ref[...] = _dot(up[:, b * LANES:(b + 1) * LANES], bm_ref[b])
        are = are_ref[b]
        aim = aim_ref[b]

        def scan(t, carry):
            xr, xi = carry
            rs = pl.ds(pl.multiple_of(t * 8, 8), 8)
            nxr = are * xr - aim * xi + bu_ref[rs, 0:half]
            nxi = are * xi + aim * xr + bu_ref[rs, half:2 * half]
            x_ref[rs, 0:half] = nxr
            x_ref[rs, half:2 * half] = nxi
            return nxr, nxi

        zero = jnp.zeros((8, half), F32)
        er, ei = lax.fori_loop(0, nt, scan, (zero, zero))

        alre = alre_ref[b][0:1, :]
        alim = alim_ref[b][0:1, :]
        cr = state_ref[b, 0:1, 0:half]
        ci = state_ref[b, 0:1, half:2 * half]
        ctr = jnp.zeros((8, half), F32)
        cti = jnp.zeros((8, half), F32)
        for s in range(8):
            ctr = jnp.where(sub == s, cr, ctr)
            cti = jnp.where(sub == s, ci, cti)
            ncr = alre * cr - alim * ci + er[s:s + 1, :]
            nci = alre * ci + alim * cr + ei[s:s + 1, :]
            cr, ci = ncr, nci
        state_ref[b, :, 0:half] = jnp.broadcast_to(cr, (8, half))
        state_ref[b, :, half:2 * half] = jnp.broadcast_to(ci, (8, half))

        def fix(t, carry):
            pr, pi = carry
            rs = pl.ds(pl.multiple_of(t * 8, 8), 8)
            x_ref[rs, 0:half] = x_ref[rs, 0:half] + (pr * ctr - pi * cti)
            x_ref[rs, half:2 * half] = x_ref[rs, half:2 * half] + (pr * cti + pi * ctr)
            return are * pr - aim * pi, are * pi + aim * pr

        lax.fori_loop(0, nt, fix, (are, aim))
        yp_ref[:, b * LANES:(b + 1) * LANES] = _dot(x_ref[...].astype(BF16), cm_ref[b])

    y = _dot01(permt_ref[...], yp_ref[...]) + d_ref[...] * u
    o_ref[...] = _gelu_tanh(y).astype(o_ref.dtype)


def _s5(z, u_off, bm, cm, are, aim, alre, alim, d_skip, rows=256):
    s_len = z.shape[0]
    width = d_skip.shape[0]
    nblk, _, two_half = bm.shape
    half = two_half // 2
    seg = rows // 8
    rho = np.arange(rows)
    t_of = (rho % 8) * seg + rho // 8
    perm_np = np.zeros((rows, rows), np.float32)
    perm_np[rho, t_of] = 1.0
    perm = jnp.asarray(perm_np, BF16)
    permt = jnp.asarray(perm_np.T, BF16)
    uo = u_off // width
    kern = functools.partial(_s5_kernel, rows=rows, half=half)
    c2 = lambda shape: pl.BlockSpec(shape, lambda i: (0, 0))
    c3 = lambda shape: pl.BlockSpec(shape, lambda i: (0, 0, 0))
    return pl.pallas_call(
        kern,
        grid=(s_len // rows,),
        in_specs=[pl.BlockSpec((rows, width), lambda i: (i, uo)),
                  c2((rows, rows)), c2((rows, rows)),
                  c3(bm.shape), c3(cm.shape),
                  c3(are.shape), c3(aim.shape), c3(alre.shape), c3(alim.shape),
                  c2((1, width))],
        out_specs=pl.BlockSpec((rows, width), lambda i: (i, 0)),
        out_shape=jax.ShapeDtypeStruct((s_len, width), BF16),
        scratch_shapes=[pltpu.VMEM((nblk, 8, two_half), F32),
                        pltpu.VMEM((rows, two_half), F32),
                        pltpu.VMEM((rows, two_half), F32),
                        pltpu.VMEM((rows, width), F32)],
        compiler_params=_cparams(("arbitrary",)),
        name="s5",
    )(z, perm, permt, bm, cm, are, aim, alre, alim, d_skip.reshape(1, width))


def _s5_params(a_re, a_im, log_dt, b_re, b_im, c_re, c_im, seg):
    g_n, p_n = a_re.shape
    nblk = g_n // S5_GPB
    dt = jnp.exp(log_dt)[:, None]
    mag = jnp.exp(a_re * dt)
    ang = a_im * dt
    abar_re = mag * jnp.cos(ang)
    abar_im = mag * jnp.sin(ang)
    nr = abar_re - 1.0
    den = a_re * a_re + a_im * a_im
    zr = (nr * a_re + abar_im * a_im) / den
    zi = (abar_im * a_re - nr * a_im) / den
    bbar_re = zr[:, :, None] * b_re - zi[:, :, None] * b_im
    bbar_im = zr[:, :, None] * b_im + zi[:, :, None] * b_re
    eye = jnp.eye(S5_GPB, dtype=F32)

    def embed_b(bb):
        bb = bb.reshape(nblk, S5_GPB, p_n, S5_GROUP)
        m = jnp.einsum('bgpc,gh->bgchp', bb, eye)
        return m.reshape(nblk, S5_GPB * S5_GROUP, S5_GPB * p_n)

    bm = jnp.concatenate([embed_b(bbar_re), embed_b(bbar_im)], axis=2).astype(BF16)

    def embed_c(cc):
        cc = cc.reshape(nblk, S5_GPB, S5_GROUP, p_n)
        m = jnp.einsum('bgcp,gh->bgphc', cc, eye)
        return m.reshape(nblk, S5_GPB * p_n, S5_GPB * S5_GROUP)

    cm = jnp.concatenate([embed_c(c_re), embed_c(-c_im)], axis=1).astype(BF16)

    def tile8(v):
        v = v.reshape(nblk, 1, S5_GPB * p_n)
        return jnp.broadcast_to(v, (nblk, 8, S5_GPB * p_n))

    pr, pi = abar_re, abar_im
    n_sq = int(round(math.log2(seg)))
    assert 2 ** n_sq == seg
    for _ in range(n_sq):
        pr, pi = pr * pr - pi * pi, 2.0 * pr * pi
    return bm, cm, tile8(abar_re), tile8(abar_im), tile8(pr), tile8(pi)


def _glu_kernel(y_ref, w_ref, gate_ref, o_ref, *, width):
    zg = _dot(y_ref[...], w_ref[...])
    o = zg[:, :width] * _sigmoid(zg[:, width:]) * _silu(gate_ref[...])
    o_ref[...] = o.astype(o_ref.dtype)


def _glu(y, w_glu, z, g_off, tm=512):
    m, width = y.shape
    go = g_off // width
    return pl.pallas_call(
        functools.partial(_glu_kernel, width=width),
        grid=(m // tm,),
        in_specs=[pl.BlockSpec((tm, width), lambda i: (i, 0)),
                  pl.BlockSpec((width, 2 * width), lambda i: (0, 0)),
                  pl.BlockSpec((tm, width), lambda i: (i, go))],
        out_specs=pl.BlockSpec((tm, width), lambda i: (i, 0)),
        out_shape=jax.ShapeDtypeStruct((m, width), BF16),
        compiler_params=_cparams(("parallel",)),
        name="s5_glu",
    )(y, w_glu, z)


def _merge_kernel(oa_ref, ob_ref, oc_ref, ga_ref, gb_ref, gc_ref, bg_ref, wa_ref, wb_ref, wc_ref,
                  wo_ref, x_ref, ng_ref, xo_ref, ho_ref, *, d):
    def gate(g_ref, i):
        return _sigmoid(g_ref[...] + bg_ref[:, i * d:(i + 1) * d])

    merged = gate(ga_ref, 0) * _dot(oa_ref[...], wa_ref[...])
    merged = merged + gate(gb_ref, 1) * _dot(ob_ref[...], wb_ref[...])
    merged = merged + gate(gc_ref, 2) * _dot(oc_ref[...], wc_ref[...])
    xn = x_ref[...] + _dot(merged.astype(BF16), wo_ref[...])
    xo_ref[...] = xn
    ms = jnp.mean(xn * xn, axis=-1, keepdims=True)
    ho_ref[...] = (xn * lax.rsqrt(ms + EPS) * ng_ref[...]).astype(ho_ref.dtype)


def _merge(oa, ob, oc, z, mg_off, b_gate, wa, wb, wc, wo, x, next_g, h_dtype, tm=256):
    m, d = x.shape
    w = oa.shape[1]
    assert mg_off % d == 0
    mo = mg_off // d
    row = lambda width: pl.BlockSpec((tm, width), lambda i: (i, 0))
    gate = lambda k: pl.BlockSpec((tm, d), lambda i: (i, mo + k))
    resident = lambda shape: pl.BlockSpec(shape, lambda i: (0, 0), pipeline_mode=pl.Buffered(1))
    return pl.pallas_call(
        functools.partial(_merge_kernel, d=d),
        grid=(m // tm,),
        in_specs=[row(w), row(w), row(w), gate(0), gate(1), gate(2),
                  resident((1, N_BRANCH * d)),
                  resident((w, d)), resident((w, d)), resident((w, d)), resident((d, d)),
                  row(d), resident((1, d))],
        out_specs=[row(d), row(d)],
        out_shape=[jax.ShapeDtypeStruct((m, d), F32), jax.ShapeDtypeStruct((m, d), h_dtype)],
        compiler_params=_cparams(("parallel",)),
        name="merge_out",
    )(oa, ob, oc, z, z, z, b_gate.reshape(1, -1), wa, wb, wc, wo, x, next_g.reshape(1, -1))


def kernel(x, norm_g, w_in, b_gate, fox_bf, hg_lb, hg_norm_g, s5_a_re, s5_a_im, s5_log_dt,
           s5_b_re, s5_b_im, s5_c_re, s5_c_im, s5_d, s5_w_glu, w_br_a, w_br_b, w_br_c, w_out,
           final_g):
    bsz, s_len, d = x.shape
    depth = w_in.shape[0]
    hg_w = hg_lb.shape[1]
    s5_w = s5_d.shape[1]
    n_fox = fox_bf.shape[1]
    fox_w = n_fox * HEAD_DIM
    n_hg = hg_w // HEAD_DIM
    sizes = (hg_w, hg_w, hg_w, hg_w, s5_w, s5_w, fox_w, fox_w, fox_w, n_fox, fox_w, N_BRANCH * d)
    offs = np.concatenate([[0], np.cumsum(sizes)])
    (o_hq, o_hf, o_hi, o_hg, o_su, o_sg, o_fq, o_fk, o_fv, o_ff, o_fg, o_mg, o_end) = (int(v) for v in offs)
    shift = o_fg - o_ff
    o_fg2, o_mg2 = o_fg - shift, o_mg - shift
    s5_rows = 256
    outs = []
    for b in range(bsz):
        xb = x[b]
        h = _rmsnorm(xb, norm_g[0], BF16)
        for l in range(depth):
            wl = w_in[l]
            w_main = jnp.concatenate([wl[:, :o_ff], wl[:, o_fg:]], axis=1).astype(BF16)
            w_ff = jnp.pad(wl[:, o_ff:o_fg], ((0, 0), (0, LANES - n_fox))).astype(BF16)
            b_ff = jnp.pad(fox_bf[l], (0, LANES - n_fox)).reshape(1, LANES)
            z = _inproj(h, w_main)

            o_a = _hgrn(z, hg_lb, hg_norm_g[l], l, o_hq, o_hf, o_hi, o_hg, n_hg)

            s5p = _s5_params(s5_a_re[l], s5_a_im[l], s5_log_dt[l], s5_b_re[l], s5_b_im[l],
                             s5_c_re[l], s5_c_im[l], s5_rows // 8)
            y_b = _s5(z, o_su, *s5p, s5_d[l], rows=s5_rows)
            o_b = _glu(y_b, s5_w_glu[l].astype(BF16), z, o_sg)

            c = _fox_cumlog(h, w_ff, b_ff)
            ct = c[:, :n_fox].T
            o_c = _fox_attn(z, ct[:, :, None], ct[:, None, :], o_fq, o_fk, o_fv, o_fg2, n_fox)

            last = l == depth - 1
            next_g = final_g if last else norm_g[l + 1]
            xb, h = _merge(o_a, o_b, o_c, z, o_mg2, b_gate[l],
                           w_br_a[l].astype(BF16), w_br_b[l].astype(BF16), w_br_c[l].astype(BF16),
                           w_out[l].astype(BF16), xb, next_g, F32 if last else BF16)
        outs.append(h)
    return outs[0][None] if bsz == 1 else jnp.stack(outs, axis=0)
```

```python
import functools
import math

import numpy as np
import jax
import jax.numpy as jnp
from jax import lax
from jax.experimental import pallas as pl
from jax.experimental.pallas import tpu as pltpu

F32 = jnp.float32
BF16 = jnp.bfloat16

EPS = 1e-6
LANES = 128
VMEM_LIMIT = 56 * 1024 * 1024

HEAD_DIM = 128
S5_GROUP = 16
S5_STATE = 64
S5_GPB = 8
N_BRANCH = 3

HG_SUB = 16


def _cparams(sem):
    return pltpu.CompilerParams(dimension_semantics=sem, vmem_limit_bytes=VMEM_LIMIT)


def _dot(a, b):
    return jnp.dot(a, b, preferred_element_type=F32)


def _dot_nt(a, b):
    return lax.dot_general(a, b, (((1,), (1,)), ((), ())), preferred_element_type=F32)


def _dot_tn(a, b):
    return lax.dot_general(a, b, (((0,), (0,)), ((), ())), preferred_element_type=F32)


def _split3(x):
    hi = x.astype(BF16)
    r1 = x - hi.astype(F32)
    mid = r1.astype(BF16)
    lo = (r1 - mid.astype(F32)).astype(BF16)
    return hi, mid, lo


def _dot01(m, x):
    hi, mid, lo = _split3(x)
    return _dot(m, hi) + _dot(m, mid) + _dot(m, lo)


def _log_sigmoid(z):
    return jnp.minimum(z, 0.0) - jnp.log1p(jnp.exp(-jnp.abs(z)))


def _sigmoid(z):
    return 1.0 / (1.0 + jnp.exp(-z))


def _silu(z):
    return z * _sigmoid(z)


def _rmsnorm_kernel(x_ref, g_ref, o_ref):
    x = x_ref[...]
    ms = jnp.mean(x * x, axis=-1, keepdims=True)
    o_ref[...] = (x * lax.rsqrt(ms + EPS) * g_ref[...]).astype(o_ref.dtype)


def _rmsnorm(x, g, out_dtype, tm=512):
    m, d = x.shape
    return pl.pallas_call(
        _rmsnorm_kernel,
        grid=(m // tm,),
        in_specs=[pl.BlockSpec((tm, d), lambda i: (i, 0)),
                  pl.BlockSpec((1, d), lambda i: (0, 0))],
        out_specs=pl.BlockSpec((tm, d), lambda i: (i, 0)),
        out_shape=jax.ShapeDtypeStruct((m, d), out_dtype),
        compiler_params=_cparams(("parallel",)),
        name="rmsnorm",
    )(x, g.reshape(1, d))


def _inproj_kernel(h_ref, w_ref, o_ref):
    o_ref[...] = _dot(h_ref[...], w_ref[...])


def _inproj(h, w, tm=512, tn=1024):
    m, k = h.shape
    n = w.shape[1]
    return pl.pallas_call(
        _inproj_kernel,
        grid=(n // tn, m // tm),
        in_specs=[pl.BlockSpec((tm, k), lambda j, i: (i, 0)),
                  pl.BlockSpec((k, tn), lambda j, i: (0, j))],
        out_specs=pl.BlockSpec((tm, tn), lambda j, i: (i, j)),
        out_shape=jax.ShapeDtypeStruct((m, n), F32),
        compiler_params=_cparams(("parallel", "arbitrary")),
        name="inproj",
    )(h, w)


def _foxc_kernel(h_ref, w_ref, b_ref, tri_ref, c_ref, carry_ref):
    @pl.when(pl.program_id(0) == 0)
    def _():
        carry_ref[...] = jnp.zeros_like(carry_ref)

    logits = _dot(h_ref[...], w_ref[...]) + b_ref[...]
    ls = _log_sigmoid(logits)
    cum = _dot01(tri_ref[...], ls) + carry_ref[0:1, :]
    c_ref[...] = cum
    tm = cum.shape[0]
    carry_ref[...] = jnp.broadcast_to(cum[tm - 1:tm, :], carry_ref.shape)


def _fox_cumlog(h, w_ff, b_ff, tm=512):
    m, k = h.shape
    tri = jnp.asarray(np.tril(np.ones((tm, tm), np.float32)), BF16)
    return pl.pallas_call(
        _foxc_kernel,
        grid=(m // tm,),
        in_specs=[pl.BlockSpec((tm, k), lambda i: (i, 0)),
                  pl.BlockSpec((k, LANES), lambda i: (0, 0)),
                  pl.BlockSpec((1, LANES), lambda i: (0, 0)),
                  pl.BlockSpec((tm, tm), lambda i: (0, 0))],
        out_specs=pl.BlockSpec((tm, LANES), lambda i: (i, 0)),
        out_shape=jax.ShapeDtypeStruct((m, LANES), F32),
        scratch_shapes=[pltpu.VMEM((8, LANES), F32)],
        compiler_params=_cparams(("arbitrary",)),
        name="fox_cumlog",
    )(h, w_ff, b_ff, tri)


NEG_BIG = -1e30


LOG2E = 1.0 / math.log(2.0)
ATT_ROWS = 64


def _fox_attn_kernel(q_ref, k_ref, v_ref, ccol_ref, crow_ref, gate_ref, o_ref,
                     kb_ref, vb_ref, sa_ref, sb_ref, p_ref, acc_ref, m_ref, ct_ref, *, tq, scale):
    qi = pl.program_id(1)

    @pl.when(qi == 0)
    def _():
        kb_ref[...] = k_ref[...].astype(BF16)
        vb_ref[:, 0:HEAD_DIM] = v_ref[...].astype(BF16)
        vb_ref[:, HEAD_DIM:2 * HEAD_DIM] = jnp.ones((vb_ref.shape[0], HEAD_DIM), BF16)

    q = (q_ref[...] * (scale * LOG2E)).astype(BF16)
    ct_ref[...] = ccol_ref[0] * LOG2E
    m_ref[...] = jnp.full(m_ref.shape, NEG_BIG, F32)
    acc_ref[...] = jnp.zeros(acc_ref.shape, F32)

    def logits(kb, s_ref):
        k0 = pl.multiple_of(kb * tq, tq)
        s_ref[...] = _dot_nt(q, kb_ref[pl.ds(k0, tq), :])

    def softmax_pv(kb, s_ref, masked):
        k0 = pl.multiple_of(kb * tq, tq)
        crow = crow_ref[0, :, pl.ds(k0, tq)] * LOG2E
        for r in range(tq // ATT_ROWS):
            rs = slice(r * ATT_ROWS, (r + 1) * ATT_ROWS)
            s = s_ref[rs, :] - crow
            if masked:
                row = lax.broadcasted_iota(jnp.int32, (ATT_ROWS, tq), 0) + r * ATT_ROWS
                col = lax.broadcasted_iota(jnp.int32, (ATT_ROWS, tq), 1)
                s = jnp.where(col <= row, s, NEG_BIG)
            ct = ct_ref[rs, :]
            m_old = m_ref[rs, :]
            m_new = jnp.maximum(m_old, jnp.max(s, axis=1, keepdims=True) + ct)
            p_ref[rs, :] = jnp.exp2(s - (m_new - ct)).astype(BF16)
            m_ref[rs, :] = m_new
            acc_ref[rs, :] = jnp.exp2(m_old - m_new) * acc_ref[rs, :]
        acc_ref[...] += _dot(p_ref[...], vb_ref[pl.ds(k0, tq), :])

    logits(0, sa_ref)
    n_pairs = qi // 2

    def body(j, carry):
        logits(2 * j + 1, sb_ref)
        softmax_pv(2 * j, sa_ref, False)
        logits(2 * j + 2, sa_ref)
        softmax_pv(2 * j + 1, sb_ref, False)
        return carry

    lax.fori_loop(0, n_pairs, body, 0)

    @pl.when(qi % 2 == 0)
    def _():
        softmax_pv(qi, sa_ref, True)

    @pl.when(qi % 2 == 1)
    def _():
        logits(qi, sb_ref)
        softmax_pv(qi - 1, sa_ref, False)
        softmax_pv(qi, sb_ref, True)

    acc = acc_ref[...]
    out = acc[:, 0:HEAD_DIM] / acc[:, HEAD_DIM:2 * HEAD_DIM]
    o_ref[...] = (out * _silu(gate_ref[...])).astype(o_ref.dtype)


def _fox_attn(z, ccol, crow, q_off, k_off, v_off, g_off, n_heads, tq=512):
    s_len = z.shape[0]
    qo, ko, vo, go = (o // HEAD_DIM for o in (q_off, k_off, v_off, g_off))
    kern = functools.partial(_fox_attn_kernel, tq=tq, scale=HEAD_DIM ** -0.5)
    return pl.pallas_call(
        kern,
        grid=(n_heads, s_len // tq),
        in_specs=[pl.BlockSpec((tq, HEAD_DIM), lambda h, i: (i, qo + h)),
                  pl.BlockSpec((s_len, HEAD_DIM), lambda h, i: (0, ko + h)),
                  pl.BlockSpec((s_len, HEAD_DIM), lambda h, i: (0, vo + h)),
                  pl.BlockSpec((1, tq, 1), lambda h, i: (h, i, 0)),
                  pl.BlockSpec((1, 1, s_len), lambda h, i: (h, 0, 0)),
                  pl.BlockSpec((tq, HEAD_DIM), lambda h, i: (i, go + h))],
        out_specs=pl.BlockSpec((tq, HEAD_DIM), lambda h, i: (i, h)),
        out_shape=jax.ShapeDtypeStruct((s_len, n_heads * HEAD_DIM), BF16),
        scratch_shapes=[pltpu.VMEM((s_len, HEAD_DIM), BF16),
                        pltpu.VMEM((s_len, 2 * HEAD_DIM), BF16),
                        pltpu.VMEM((tq, tq), F32),
                        pltpu.VMEM((tq, tq), F32),
                        pltpu.VMEM((tq, tq), BF16),
                        pltpu.VMEM((tq, 2 * HEAD_DIM), F32),
                        pltpu.VMEM((tq, 1), F32),
                        pltpu.VMEM((tq, 1), F32)],
        compiler_params=_cparams(("parallel", "arbitrary")),
        name="fox_attn",
    )(z, z, z, ccol, crow, z)


HG_PROWS = 8 * HG_SUB + 8 * (HG_SUB // 2)


def _hgrn_kernel(q_ref, f_ref, i_ref, gate_ref, lb_ref, ng_ref, tri_ref, blk_ref, ones_ref,
                 mask_ref, o_ref, st_ref, qt_ref, kt_ref, w_ref, cum_ref, dd_ref, p_ref, sc_ref,
                 acc_ref, *, layer, rows):
    @pl.when(pl.program_id(1) == 0)
    def _():
        st_ref[...] = jnp.zeros_like(st_ref)

    z = f_ref[...]
    ls = _log_sigmoid(z)
    if layer == 0:
        g = ls
        logk = ls - z
    else:
        lbp = lb_ref[...]
        e = jnp.exp(lbp - jnp.max(lbp, axis=0, keepdims=True))
        p = e / jnp.sum(e, axis=0, keepdims=True)
        lb = jnp.sum(p[1:layer + 1, :], axis=0, keepdims=True)
        a = jnp.log(lb)
        l1m = jnp.log1p(-lb)
        b = l1m + ls
        g = jnp.maximum(a, b) + jnp.log1p(jnp.exp(-jnp.abs(a - b)))
        logk = l1m + (ls - z)

    cum = _dot01(tri_ref[...], g)
    tot = _dot01(blk_ref[...], g)
    w = cum - logk
    qt_ref[...] = (q_ref[...] * jnp.exp(cum)).astype(BF16)
    kt_ref[...] = jnp.exp(tot - w).astype(BF16)
    w_ref[...] = w
    cum_ref[...] = cum
    dd_ref[...] = jnp.exp(tot)

    half = HG_SUB // 2
    n_groups = rows // HG_SUB

    def bcast_row(ref, r):
        return jnp.broadcast_to(ref[r:r + 1, :], (half, HEAD_DIM))

    for g_i in range(n_groups):
        r0 = g_i * HG_SUB
        p0 = g_i * HG_PROWS
        c_lo, c_hi = cum_ref[r0:r0 + half, :], cum_ref[r0 + half:r0 + HG_SUB, :]
        q_lo, q_hi = q_ref[r0:r0 + half, :], q_ref[r0 + half:r0 + HG_SUB, :]
        for s in range(half):
            w_s = bcast_row(w_ref, r0 + s)
            p_lo = q_lo * jnp.exp(c_lo - w_s + mask_ref[s * half:(s + 1) * half, :])
            p_hi = q_hi * jnp.exp(c_hi - w_s)
            p_ref[p0 + s * HG_SUB:p0 + (s + 1) * HG_SUB, :] = (
                jnp.concatenate([p_lo, p_hi], axis=0).astype(BF16))
        for s in range(0, half, 2):
            pa = q_hi * jnp.exp(c_hi - bcast_row(w_ref, r0 + half + s)
                                + mask_ref[s * half:(s + 1) * half, :])
            pb = q_hi * jnp.exp(c_hi - bcast_row(w_ref, r0 + half + s + 1)
                                + mask_ref[(s + 1) * half:(s + 2) * half, :])
            base = p0 + half * HG_SUB + s * half
            p_ref[base:base + HG_SUB, :] = jnp.concatenate([pa, pb], axis=0).astype(BF16)

    sc_ref[...] = _dot(p_ref[...], ones_ref[...])

    upds = [_dot_tn(i_ref[g_i * HG_SUB:(g_i + 1) * HG_SUB, :].astype(BF16),
                    kt_ref[g_i * HG_SUB:(g_i + 1) * HG_SUB, :]) for g_i in range(n_groups)]
    st = st_ref[...]
    for g_i in range(n_groups):
        r0 = g_i * HG_SUB
        p0 = g_i * HG_PROWS
        o_lo = jnp.zeros((half, HEAD_DIM), F32)
        o_hi = jnp.zeros((half, HEAD_DIM), F32)
        for s in range(half):
            v_s = bcast_row(i_ref, r0 + s)
            o_lo = o_lo + sc_ref[p0 + s * HG_SUB:p0 + s * HG_SUB + half, :] * v_s
            o_hi = o_hi + sc_ref[p0 + s * HG_SUB + half:p0 + (s + 1) * HG_SUB, :] * v_s
        for s in range(half):
            base = p0 + half * HG_SUB + s * half
            o_hi = o_hi + sc_ref[base:base + half, :] * bcast_row(i_ref, r0 + half + s)
        o_inter = _dot_nt(qt_ref[r0:r0 + HG_SUB, :], st.astype(BF16))
        acc_ref[r0:r0 + HG_SUB, :] = o_inter + jnp.concatenate([o_lo, o_hi], axis=0)
        st = st * dd_ref[r0:r0 + 1, :] + upds[g_i]
    st_ref[...] = st

    o = acc_ref[...]
    ms = jnp.mean(o * o, axis=-1, keepdims=True)
    o = o * lax.rsqrt(ms + EPS) * ng_ref[...]
    o_ref[...] = (o * _silu(gate_ref[...])).astype(o_ref.dtype)


def _hgrn(z, hg_lb, norm_g, layer, q_off, f_off, i_off, g_off, n_heads, rows=256):
    s_len = z.shape[0]
    depth = hg_lb.shape[0]
    qo, fo, io, go = (o // HEAD_DIM for o in (q_off, f_off, i_off, g_off))
    r = np.arange(rows)
    same = (r[:, None] // HG_SUB) == (r[None, :] // HG_SUB)
    tri = jnp.asarray((same & (r[None, :] <= r[:, None])).astype(np.float32), BF16)
    blk = jnp.asarray(same.astype(np.float32), BF16)
    ones = jnp.ones((HEAD_DIM, HEAD_DIM), BF16)
    half = HG_SUB // 2
    t_idx = np.arange(half)
    mask_np = np.where(t_idx[None, :, None] >= t_idx[:, None, None], 0.0, NEG_BIG)
    mask = jnp.asarray(np.broadcast_to(mask_np, (half, half, HEAD_DIM)).reshape(half * half, HEAD_DIM), F32)
    n_prows = (rows // HG_SUB) * HG_PROWS
    kern = functools.partial(_hgrn_kernel, layer=layer, rows=rows)
    blk_spec = lambda off: pl.BlockSpec((rows, HEAD_DIM), lambda h, i: (i, off + h))
    const = lambda shape: pl.BlockSpec(shape, lambda h, i: (0, 0))
    return pl.pallas_call(
        kern,
        grid=(n_heads, s_len // rows),
        in_specs=[blk_spec(qo), blk_spec(fo), blk_spec(io), blk_spec(go),
                  pl.BlockSpec((depth, HEAD_DIM), lambda h, i: (0, h)),
                  pl.BlockSpec((1, HEAD_DIM), lambda h, i: (0, h)),
                  const((rows, rows)), const((rows, rows)), const((HEAD_DIM, HEAD_DIM)),
                  const((half * half, HEAD_DIM))],
        out_specs=pl.BlockSpec((rows, HEAD_DIM), lambda h, i: (i, h)),
        out_shape=jax.ShapeDtypeStruct((s_len, n_heads * HEAD_DIM), BF16),
        scratch_shapes=[pltpu.VMEM((HEAD_DIM, HEAD_DIM), F32),
                        pltpu.VMEM((rows, HEAD_DIM), BF16),
                        pltpu.VMEM((rows, HEAD_DIM), BF16),
                        pltpu.VMEM((rows, HEAD_DIM), F32),
                        pltpu.VMEM((rows, HEAD_DIM), F32),
                        pltpu.VMEM((rows, HEAD_DIM), F32),
                        pltpu.VMEM((n_prows, HEAD_DIM), BF16),
                        pltpu.VMEM((n_prows, HEAD_DIM), F32),
                        pltpu.VMEM((rows, HEAD_DIM), F32)],
        compiler_params=_cparams(("parallel", "arbitrary")),
        name="hgrn2",
    )(z, z, z, z, hg_lb, norm_g.reshape(1, -1), tri, blk, ones, mask)


def _gelu_tanh(x):
    c = math.sqrt(2.0 / math.pi)
    return 0.5 * x * (1.0 + jnp.tanh(c * (x + 0.044715 * (x * x * x))))


def _s5_kernel(u_ref, perm_ref, permt_ref, bm_ref, cm_ref, are_ref, aim_ref, alre_ref, alim_ref,
               d_ref, o_ref, state_ref, bu_ref, x_ref, yp_ref, *, rows, half):
    @pl.when(pl.program_id(0) == 0)
    def _():
        state_ref[...] = jnp.zeros_like(state_ref)

    nblk = bm_ref.shape[0]
    nt = rows // 8
    u = u_ref[...]
    up = _dot(perm_ref[...], u.astype(BF16)).astype(BF16)
    sub = lax.broadcasted_iota(jnp.int32, (8, half), 0)

    for b in range(nblk):
        bu_ref[...] = _dot(up[:, b * LANES:(b + 1) * LANES], bm_ref[b])
        are = are_ref[b]
        aim = aim_ref[b]

        def scan(t, carry):
            xr, xi = carry
            rs = pl.ds(pl.multiple_of(t * 8, 8), 8)
            nxr = are * xr - aim * xi + bu_ref[rs, 0:half]
            nxi = are * xi + aim * xr + bu_ref[rs, half:2 * half]
            x_ref[rs, 0:half] = nxr
            x_ref[rs, half:2 * half] = nxi
            return nxr, nxi

        zero = jnp.zeros((8, half), F32)
        er, ei = lax.fori_loop(0, nt, scan, (zero, zero))

        alre = alre_ref[b][0:1, :]
        alim = alim_ref[b][0:1, :]
        cr = state_ref[b, 0:1, 0:half]
        ci = state_ref[b, 0:1, half:2 * half]
        ctr = jnp.zeros((8, half), F32)
        cti = jnp.zeros((8, half), F32)
        for s in range(8):
            ctr = jnp.where(sub == s, cr, ctr)
            cti = jnp.where(sub == s, ci, cti)
            ncr = alre * cr - alim * ci + er[s:s + 1, :]
            nci = alre * ci + alim * cr + ei[s:s + 1, :]
            cr, ci = ncr, nci
        state_ref[b, :, 0:half] = jnp.broadcast_to(cr, (8, half))
        state_ref[b, :, half:2 * half] = jnp.broadcast_to(ci, (8, half))

        def fix(t, carry):
            pr, pi = carry
            rs = pl.ds(pl.multiple_of(t * 8, 8), 8)
            x_ref[rs, 0:half] = x_ref[rs, 0:half] + (pr * ctr - pi * cti)
            x_ref[rs, half:2 * half] = x_ref[rs, half:2 * half] + (pr * cti + pi * ctr)
            return are * pr - aim * pi, are * pi + aim * pr

        lax.fori_loop(0, nt, fix, (are, aim))
        yp_ref[:, b * LANES:(b + 1) * LANES] = _dot(x_ref[...].astype(BF16), cm_ref[b])

    y = _dot01(permt_ref[...], yp_ref[...]) + d_ref[...] * u
    o_ref[...] = _gelu_tanh(y).astype(o_ref.dtype)


def _s5(z, u_off, bm, cm, are, aim, alre, alim, d_skip, rows=256):
    s_len = z.shape[0]
    width = d_skip.shape[0]
    nblk, _, two_half = bm.shape
    half = two_half // 2
    seg = rows // 8
    rho = np.arange(rows)
    t_of = (rho % 8) * seg + rho // 8
    perm_np = np.zeros((rows, rows), np.float32)
    perm_np[rho, t_of] = 1.0
    perm = jnp.asarray(perm_np, BF16)
    permt = jnp.asarray(perm_np.T, BF16)
    uo = u_off // width
    kern = functools.partial(_s5_kernel, rows=rows, half=half)
    c2 = lambda shape: pl.BlockSpec(shape, lambda i: (0, 0))
    c3 = lambda shape: pl.BlockSpec(shape, lambda i: (0, 0, 0))
    return pl.pallas_call(
        kern,
        grid=(s_len // rows,),
        in_specs=[pl.BlockSpec((rows, width), lambda i: (i, uo)),
                  c2((rows, rows)), c2((rows, rows)),
                  c3(bm.shape), c3(cm.shape),
                  c3(are.shape), c3(aim.shape), c3(alre.shape), c3(alim.shape),
                  c2((1, width))],
        out_specs=pl.BlockSpec((rows, width), lambda i: (i, 0)),
        out_shape=jax.ShapeDtypeStruct((s_len, width), BF16),
        scratch_shapes=[pltpu.VMEM((nblk, 8, two_half), F32),
                        pltpu.VMEM((rows, two_half), F32),
                        pltpu.VMEM((rows, two_half), F32),
                        pltpu.VMEM((rows, width), F32)],
        compiler_params=_cparams(("arbitrary",)),
        name="s5",
    )(z, perm, permt, bm, cm, are, aim, alre, alim, d_skip.reshape(1, width))


def _s5_params(a_re, a_im, log_dt, b_re, b_im, c_re, c_im, seg):
    g_n, p_n = a_re.shape
    nblk = g_n // S5_GPB
    dt = jnp.exp(log_dt)[:, None]
    mag = jnp.exp(a_re * dt)
    ang = a_im * dt
    abar_re = mag * jnp.cos(ang)
    abar_im = mag * jnp.sin(ang)
    nr = abar_re - 1.0
    den = a_re * a_re + a_im * a_im
    zr = (nr * a_re + abar_im * a_im) / den
    zi = (abar_im * a_re - nr * a_im) / den
    bbar_re = zr[:, :, None] * b_re - zi[:, :, None] * b_im
    bbar_im = zr[:, :, None] * b_im + zi[:, :, None] * b_re
    eye = jnp.eye(S5_GPB, dtype=F32)

    def embed_b(bb):
        bb = bb.reshape(nblk, S5_GPB, p_n, S5_GROUP)
        m = jnp.einsum('bgpc,gh->bgchp', bb, eye)
        return m.reshape(nblk, S5_GPB * S5_GROUP, S5_GPB * p_n)

    bm = jnp.concatenate([embed_b(bbar_re), embed_b(bbar_im)], axis=2).astype(BF16)

    def embed_c(cc):
        cc = cc.reshape(nblk, S5_GPB, S5_GROUP, p_n)
        m = jnp.einsum('bgcp,gh->bgphc', cc, eye)
        return m.reshape(nblk, S5_GPB * p_n, S5_GPB * S5_GROUP)

    cm = jnp.concatenate([embed_c(c_re), embed_c(-c_im)], axis=1).astype(BF16)

    def tile8(v):
        v = v.reshape(nblk, 1, S5_GPB * p_n)
        return jnp.broadcast_to(v, (nblk, 8, S5_GPB * p_n))

    pr, pi = abar_re, abar_im
    n_sq = int(round(math.log2(seg)))
    assert 2 ** n_sq == seg
    for _ in range(n_sq):
        pr, pi = pr * pr - pi * pi, 2.0 * pr * pi
    return bm, cm, tile8(abar_re), tile8(abar_im), tile8(pr), tile8(pi)


def _glu_kernel(y_ref, w_ref, gate_ref, o_ref, *, width):
    zg = _dot(y_ref[...], w_ref[...])
    o = zg[:, :width] * _sigmoid(zg[:, width:]) * _silu(gate_ref[...])
    o_ref[...] = o.astype(o_ref.dtype)


def _glu(y, w_glu, z, g_off, tm=512):
    m, width = y.shape
    go = g_off // width
    return pl.pallas_call(
        functools.partial(_glu_kernel, width=width),
        grid=(m // tm,),
        in_specs=[pl.BlockSpec((tm, width), lambda i: (i, 0)),
                  pl.BlockSpec((width, 2 * width), lambda i: (0, 0)),
                  pl.BlockSpec((tm, width), lambda i: (i, go))],
        out_specs=pl.BlockSpec((tm, width), lambda i: (i, 0)),
        out_shape=jax.ShapeDtypeStruct((m, width), BF16),
        compiler_params=_cparams(("parallel",)),
        name="s5_glu",
    )(y, w_glu, z)


def _merge_kernel(oa_ref, ob_ref, oc_ref, ga_ref, gb_ref, gc_ref, bg_ref, wa_ref, wb_ref, wc_ref,
                  wo_ref, x_ref, ng_ref, xo_ref, ho_ref, *, d):
    def gate(g_ref, i):
        return _sigmoid(g_ref[...] + bg_ref[:, i * d:(i + 1) * d])

    merged = gate(ga_ref, 0) * _dot(oa_ref[...], wa_ref[...])
    merged = merged + gate(gb_ref, 1) * _dot(ob_ref[...], wb_ref[...])
    merged = merged + gate(gc_ref, 2) * _dot(oc_ref[...], wc_ref[...])
    xn = x_ref[...] + _dot(merged.astype(BF16), wo_ref[...])
    xo_ref[...] = xn
    ms = jnp.mean(xn * xn, axis=-1, keepdims=True)
    ho_ref[...] = (xn * lax.rsqrt(ms + EPS) * ng_ref[...]).astype(ho_ref.dtype)


def _merge(oa, ob, oc, z, mg_off, b_gate, wa, wb, wc, wo, x, next_g, h_dtype, tm=256):
    m, d = x.shape
    w = oa.shape[1]
    assert mg_off % d == 0
    mo = mg_off // d
    row = lambda width: pl.BlockSpec((tm, width), lambda i: (i, 0))
    gate = lambda k: pl.BlockSpec((tm, d), lambda i: (i, mo + k))
    resident = lambda shape: pl.BlockSpec(shape, lambda i: (0, 0), pipeline_mode=pl.Buffered(1))
    return pl.pallas_call(
        functools.partial(_merge_kernel, d=d),
        grid=(m // tm,),
        in_specs=[row(w), row(w), row(w), gate(0), gate(1), gate(2),
                  resident((1, N_BRANCH * d)),
                  resident((w, d)), resident((w, d)), resident((w, d)), resident((d, d)),
                  row(d), resident((1, d))],
        out_specs=[row(d), row(d)],
        out_shape=[jax.ShapeDtypeStruct((m, d), F32), jax.ShapeDtypeStruct((m, d), h_dtype)],
        compiler_params=_cparams(("parallel",)),
        name="merge_out",
    )(oa, ob, oc, z, z, z, b_gate.reshape(1, -1), wa, wb, wc, wo, x, next_g.reshape(1, -1))


def kernel(x, norm_g, w_in, b_gate, fox_bf, hg_lb, hg_norm_g, s5_a_re, s5_a_im, s5_log_dt,
           s5_b_re, s5_b_im, s5_c_re, s5_c_im, s5_d, s5_w_glu, w_br_a, w_br_b, w_br_c, w_out,
           final_g):
    bsz, s_len, d = x.shape
    depth = w_in.shape[0]
    hg_w = hg_lb.shape[1]
    s5_w = s5_d.shape[1]
    n_fox = fox_bf.shape[1]
    fox_w = n_fox * HEAD_DIM
    n_hg = hg_w // HEAD_DIM
    sizes = (hg_w, hg_w, hg_w, hg_w, s5_w, s5_w, fox_w, fox_w, fox_w, n_fox, fox_w, N_BRANCH * d)
    offs = np.concatenate([[0], np.cumsum(sizes)])
    (o_hq, o_hf, o_hi, o_hg, o_su, o_sg, o_fq, o_fk, o_fv, o_ff, o_fg, o_mg, o_end) = (int(v) for v in offs)
    shift = o_fg - o_ff
    o_fg2, o_mg2 = o_fg - shift, o_mg - shift
    s5_rows = 256
    outs = []
    for b in range(bsz):
        xb = x[b]
        h = _rmsnorm(xb, norm_g[0], BF16)
        for l in range(depth):
            wl = w_in[l]
            w_main = jnp.concatenate([wl[:, :o_ff], wl[:, o_fg:]], axis=1).astype(BF16)
            w_ff = jnp.pad(wl[:, o_ff:o_fg], ((0, 0), (0, LANES - n_fox))).astype(BF16)
            b_ff = jnp.pad(fox_bf[l], (0, LANES - n_fox)).reshape(1, LANES)
            z = _inproj(h, w_main)

            o_a = _hgrn(z, hg_lb, hg_norm_g[l], l, o_hq, o_hf, o_hi, o_hg, n_hg)

            s5p = _s5_params(s5_a_re[l], s5_a_im[l], s5_log_dt[l], s5_b_re[l], s5_b_im[l],
                             s5_c_re[l], s5_c_im[l], s5_rows // 8)
            y_b = _s5(z, o_su, *s5p, s5_d[l], rows=s5_rows)
            o_b = _glu(y_b, s5_w_glu[l].astype(BF16), z, o_sg)

            c = _fox_cumlog(h, w_ff, b_ff)
            ct = c[:, :n_fox].T
            o_c = _fox_attn(z, ct[:, :, None], ct[:, None, :], o_fq, o_fk, o_fv, o_fg2, n_fox)

            last = l == depth - 1
            next_g = final_g if last else norm_g[l + 1]
            xb, h = _merge(o_a, o_b, o_c, z, o_mg2, b_gate[l],
                           w_br_a[l].astype(BF16), w_br_b[l].astype(BF16), w_br_c[l].astype(BF16),
                           w_out[l].astype(BF16), xb, next_g, F32 if last else BF16)
        outs.append(h)
    return outs[0][None] if bsz == 1 else jnp.stack(outs, axis=0)
```

```python
import functools
import math

import numpy as np
import jax
import jax.numpy as jnp
from jax import lax
from jax.experimental import pallas as pl
from jax.experimental.pallas import tpu as pltpu

F32 = jnp.float32
BF16 = jnp.bfloat16

EPS = 1e-6
LANES = 128
VMEM_LIMIT = 56 * 1024 * 1024

HEAD_DIM = 128
S5_GROUP = 16
S5_STATE = 64
S5_GPB = 8
N_BRANCH = 3

HG_SUB = 16


def _cparams(sem, vmem=VMEM_LIMIT):
    return pltpu.CompilerParams(dimension_semantics=sem, vmem_limit_bytes=vmem)


def _dot(a, b):
    return jnp.dot(a, b, preferred_element_type=F32)


def _dot_nt(a, b):
    return lax.dot_general(a, b, (((1,), (1,)), ((), ())), preferred_element_type=F32)


def _dot_tn(a, b):
    return lax.dot_general(a, b, (((0,), (0,)), ((), ())), preferred_element_type=F32)


def _split3(x):
    hi = x.astype(BF16)
    r1 = x - hi.astype(F32)
    mid = r1.astype(BF16)
    lo = (r1 - mid.astype(F32)).astype(BF16)
    return hi, mid, lo


def _dot01(m, x):
    hi, mid, lo = _split3(x)
    return _dot(m, hi) + _dot(m, mid) + _dot(m, lo)


def _log_sigmoid(z):
    return jnp.minimum(z, 0.0) - jnp.log1p(jnp.exp(-jnp.abs(z)))


def _sigmoid(z):
    return 1.0 / (1.0 + jnp.exp(-z))


def _silu(z):
    return z * _sigmoid(z)


def _rmsnorm_kernel(x_ref, g_ref, o_ref):
    x = x_ref[...]
    ms = jnp.mean(x * x, axis=-1, keepdims=True)
    o_ref[...] = (x * lax.rsqrt(ms + EPS) * g_ref[...]).astype(o_ref.dtype)


def _rmsnorm(x, g, out_dtype, tm=512):
    m, d = x.shape
    return pl.pallas_call(
        _rmsnorm_kernel,
        grid=(m // tm,),
        in_specs=[pl.BlockSpec((tm, d), lambda i: (i, 0)),
                  pl.BlockSpec((1, d), lambda i: (0, 0))],
        out_specs=pl.BlockSpec((tm, d), lambda i: (i, 0)),
        out_shape=jax.ShapeDtypeStruct((m, d), out_dtype),
        compiler_params=_cparams(("parallel",)),
        name="rmsnorm",
    )(x, g.reshape(1, d))


def _inproj_kernel(h_ref, w_ref, o_ref, *, first_tile_scale):
    acc = _dot(h_ref[...], w_ref[...])
    if first_tile_scale is not None:
        acc = acc * jnp.where(pl.program_id(0) == 0, first_tile_scale, 1.0)
    o_ref[...] = acc.astype(o_ref.dtype)


def _inproj(h, w, col0, ncols, out_dtype, first_tile_scale=None, tm=512, tn=1024):
    m, k = h.shape
    assert col0 % tn == 0 and ncols % tn == 0
    j0 = col0 // tn
    return pl.pallas_call(
        functools.partial(_inproj_kernel, first_tile_scale=first_tile_scale),
        grid=(ncols // tn, m // tm),
        in_specs=[pl.BlockSpec((tm, k), lambda j, i: (i, 0)),
                  pl.BlockSpec((k, tn), lambda j, i: (0, j0 + j))],
        out_specs=pl.BlockSpec((tm, tn), lambda j, i: (i, j)),
        out_shape=jax.ShapeDtypeStruct((m, ncols), out_dtype),
        compiler_params=_cparams(("parallel", "arbitrary")),
        name="inproj",
    )(h, w)


def _foxc_kernel(h_ref, w_ref, b_ref, tri_ref, c_ref, carry_ref):
    @pl.when(pl.program_id(0) == 0)
    def _():
        carry_ref[...] = jnp.zeros_like(carry_ref)

    logits = _dot(h_ref[...], w_ref[...]) + b_ref[...]
    ls = _log_sigmoid(logits)
    cum = _dot01(tri_ref[...], ls) + carry_ref[0:1, :]
    c_ref[...] = cum
    tm = cum.shape[0]
    carry_ref[...] = jnp.broadcast_to(cum[tm - 1:tm, :], carry_ref.shape)


def _fox_cumlog(h, w_ff, b_ff, tm=512):
    m, k = h.shape
    tri = jnp.asarray(np.tril(np.ones((tm, tm), np.float32)), BF16)
    return pl.pallas_call(
        _foxc_kernel,
        grid=(m // tm,),
        in_specs=[pl.BlockSpec((tm, k), lambda i: (i, 0)),
                  pl.BlockSpec((k, LANES), lambda i: (0, 0)),
                  pl.BlockSpec((1, LANES), lambda i: (0, 0)),
                  pl.BlockSpec((tm, tm), lambda i: (0, 0))],
        out_specs=pl.BlockSpec((tm, LANES), lambda i: (i, 0)),
        out_shape=jax.ShapeDtypeStruct((m, LANES), F32),
        scratch_shapes=[pltpu.VMEM((8, LANES), F32)],
        compiler_params=_cparams(("arbitrary",)),
        name="fox_cumlog",
    )(h, w_ff, b_ff, tri)


NEG_BIG = -1e30


LOG2E = 1.0 / math.log(2.0)
ATT_ROWS = 64
ATT_UNROLL = 2


def _fox_attn_kernel(tab_ref, q_ref, k_ref, v_ref, ccol_ref, crow_ref, gate_ref, o_ref,
                     va_ref, sa_ref, sb_ref, pa_ref, pb_ref, ala_ref, alb_ref, acc_ref, m_ref,
                     *, tq, n_off, n_diag):
    @pl.when(pl.program_id(0) == 0)
    def _():
        va_ref[:, HEAD_DIM:2 * HEAD_DIM] = jnp.ones((va_ref.shape[0], HEAD_DIM), BF16)

    va_ref[:, 0:HEAD_DIM] = v_ref[...]
    m_ref[...] = jnp.full(m_ref.shape, NEG_BIG, F32)
    acc_ref[...] = jnp.zeros(acc_ref.shape, F32)

    def tile(n):
        q0 = pl.multiple_of(tab_ref[0, n] * tq, tq)
        k0 = pl.multiple_of(tab_ref[1, n] * tq, tq)
        return q0, k0

    def logits(n, s_ref):
        q0, k0 = tile(n)
        s_ref[...] = _dot_nt(q_ref[pl.ds(q0, tq), :], k_ref[pl.ds(k0, tq), :])

    def softmax(n, s_ref, p_ref, al_ref, masked):
        q0, k0 = tile(n)
        crow = crow_ref[0, :, pl.ds(k0, tq)] * LOG2E
        for r in range(tq // ATT_ROWS):
            rs = slice(r * ATT_ROWS, (r + 1) * ATT_ROWS)
            qs = pl.ds(q0 + r * ATT_ROWS, ATT_ROWS)
            s = s_ref[rs, :] - crow
            if masked:
                row = lax.broadcasted_iota(jnp.int32, (ATT_ROWS, tq), 0) + r * ATT_ROWS
                col = lax.broadcasted_iota(jnp.int32, (ATT_ROWS, tq), 1)
                s = jnp.where(col <= row, s, NEG_BIG)
            ct = ccol_ref[0, qs, :] * LOG2E
            m_old = m_ref[qs, :]
            m_new = jnp.maximum(m_old, jnp.max(s, axis=1, keepdims=True) + ct)
            p_ref[rs, :] = jnp.exp2(s - (m_new - ct)).astype(BF16)
            m_ref[qs, :] = m_new
            al_ref[rs, :] = jnp.exp2(m_old - m_new)

    def accumulate(n, p_ref, al_ref):
        q0, k0 = tile(n)
        qs = pl.ds(q0, tq)
        acc_ref[qs, :] = al_ref[...] * acc_ref[qs, :] + _dot(p_ref[...], va_ref[pl.ds(k0, tq), :])

    def run(first, count, masked):
        if count == 0:
            return
        last = first + count - 1
        nxt = lambda n: jnp.minimum(n, last)
        s_buf = (sa_ref, sb_ref)
        p_buf = ((pa_ref, ala_ref), (pb_ref, alb_ref))
        logits(first, s_buf[0])
        softmax(first, s_buf[0], *p_buf[0], masked)
        logits(nxt(first + 1), s_buf[1])
        n_loop = (count - 1) // ATT_UNROLL

        def body(j, carry):
            n = first + ATT_UNROLL * j
            for u in range(ATT_UNROLL):
                accumulate(n + u, *p_buf[u % 2])
                softmax(n + u + 1, s_buf[(u + 1) % 2], *p_buf[(u + 1) % 2], masked)
                logits(nxt(n + u + 2), s_buf[u % 2])
            return carry

        lax.fori_loop(0, n_loop, body, 0)
        n = first + ATT_UNROLL * n_loop
        rest = count - 1 - ATT_UNROLL * n_loop
        for u in range(rest + 1):
            accumulate(n + u, *p_buf[u % 2])
            if u + 1 <= rest:
                softmax(n + u + 1, s_buf[(u + 1) % 2], *p_buf[(u + 1) % 2], masked)
            if u + 2 <= rest:
                logits(n + u + 2, s_buf[u % 2])

    run(0, n_off, False)
    run(n_off, n_diag, True)

    acc = acc_ref[...]
    out = acc[:, 0:HEAD_DIM] / acc[:, HEAD_DIM:2 * HEAD_DIM]
    o_ref[...] = (out * _silu(gate_ref[...])).astype(o_ref.dtype)


def _fox_attn(zqkv, z, ccol, crow, g_off, n_heads, tq=512):
    s_len = zqkv.shape[0]
    nq = s_len // tq
    go = g_off // HEAD_DIM
    off = [(qi, kb) for kb in range(nq) for qi in range(kb + 1, nq)]
    diag = [(i, i) for i in range(nq)]
    tab = jnp.asarray(np.array(off + diag, np.int32).T)
    kern = functools.partial(_fox_attn_kernel, tq=tq, n_off=len(off), n_diag=len(diag))
    once = pl.Buffered(1)
    head_col = lambda base: pl.BlockSpec((s_len, HEAD_DIM), lambda h, t: (0, base + h),
                                         pipeline_mode=once)
    grid_spec = pltpu.PrefetchScalarGridSpec(
        num_scalar_prefetch=1,
        grid=(n_heads,),
        in_specs=[head_col(0), head_col(n_heads), head_col(2 * n_heads),
                  pl.BlockSpec((1, s_len, 1), lambda h, t: (h, 0, 0), pipeline_mode=once),
                  pl.BlockSpec((1, 1, s_len), lambda h, t: (h, 0, 0)),
                  head_col(go)],
        out_specs=pl.BlockSpec((s_len, HEAD_DIM), lambda h, t: (0, h)),
        scratch_shapes=[pltpu.VMEM((s_len, 2 * HEAD_DIM), BF16),
                        pltpu.VMEM((tq, tq), F32),
                        pltpu.VMEM((tq, tq), F32),
                        pltpu.VMEM((tq, tq), BF16),
                        pltpu.VMEM((tq, tq), BF16),
                        pltpu.VMEM((tq, 1), F32),
                        pltpu.VMEM((tq, 1), F32),
                        pltpu.VMEM((s_len, 2 * HEAD_DIM), F32),
                        pltpu.VMEM((s_len, 1), F32)])
    return pl.pallas_call(
        kern,
        grid_spec=grid_spec,
        out_shape=jax.ShapeDtypeStruct((s_len, n_heads * HEAD_DIM), BF16),
        compiler_params=_cparams(("arbitrary",)),
        name="fox_attn",
    )(tab, zqkv, zqkv, zqkv, ccol, crow, z)


HG_PROWS = 8 * HG_SUB + 8 * (HG_SUB // 2)


def _hgrn_kernel(q_ref, f_ref, i_ref, gate_ref, lb_ref, ng_ref, tb_ref, ones_ref,
                 mask_ref, o_ref, st_ref, qt_ref, kt_ref, w_ref, cum_ref, dd_ref, p_ref, sc_ref,
                 acc_ref, *, layer, rows):
    @pl.when(pl.program_id(1) == 0)
    def _():
        st_ref[...] = jnp.zeros_like(st_ref)

    z = f_ref[...]
    ls = _log_sigmoid(z)
    if layer == 0:
        g = ls
        logk = ls - z
    else:
        lbp = lb_ref[...]
        e = jnp.exp(lbp - jnp.max(lbp, axis=0, keepdims=True))
        p = e / jnp.sum(e, axis=0, keepdims=True)
        lb = jnp.sum(p[1:layer + 1, :], axis=0, keepdims=True)
        a = jnp.log(lb)
        l1m = jnp.log1p(-lb)
        b = l1m + ls
        g = jnp.maximum(a, b) + jnp.log1p(jnp.exp(-jnp.abs(a - b)))
        logk = l1m + (ls - z)

    cums, tots = [], []
    for r0 in range(0, rows, LANES):
        ct = _dot01(tb_ref[...], g[r0:r0 + LANES, :])
        cums.append(ct[0:LANES, :])
        tots.append(ct[LANES:2 * LANES, :])
    cum = jnp.concatenate(cums, axis=0) * LOG2E
    tot = jnp.concatenate(tots, axis=0) * LOG2E
    w = cum - logk * LOG2E
    qt_ref[...] = (q_ref[...] * jnp.exp2(cum)).astype(BF16)
    kt_ref[...] = jnp.exp2(tot - w).astype(BF16)
    w_ref[...] = w
    cum_ref[...] = cum
    dd_ref[...] = jnp.exp2(tot)

    half = HG_SUB // 2
    n_groups = rows // HG_SUB

    def bcast_row(ref, r):
        return jnp.broadcast_to(ref[r:r + 1, :], (half, HEAD_DIM))

    for g_i in range(n_groups):
        r0 = g_i * HG_SUB
        p0 = g_i * HG_PROWS
        c_lo, c_hi = cum_ref[r0:r0 + half, :], cum_ref[r0 + half:r0 + HG_SUB, :]
        q_lo, q_hi = q_ref[r0:r0 + half, :], q_ref[r0 + half:r0 + HG_SUB, :]
        for s in range(half):
            w_s = bcast_row(w_ref, r0 + s)
            p_lo = q_lo * jnp.exp2(c_lo - w_s + mask_ref[s * half:(s + 1) * half, :])
            p_hi = q_hi * jnp.exp2(c_hi - w_s)
            p_ref[p0 + s * HG_SUB:p0 + (s + 1) * HG_SUB, :] = (
                jnp.concatenate([p_lo, p_hi], axis=0).astype(BF16))
        for s in range(0, half, 2):
            pa = q_hi * jnp.exp2(c_hi - bcast_row(w_ref, r0 + half + s)
                                + mask_ref[s * half:(s + 1) * half, :])
            pb = q_hi * jnp.exp2(c_hi - bcast_row(w_ref, r0 + half + s + 1)
                                + mask_ref[(s + 1) * half:(s + 2) * half, :])
            base = p0 + half * HG_SUB + s * half
            p_ref[base:base + HG_SUB, :] = jnp.concatenate([pa, pb], axis=0).astype(BF16)

    sc_ref[...] = _dot(p_ref[...], ones_ref[...])

    upds = [_dot_tn(i_ref[g_i * HG_SUB:(g_i + 1) * HG_SUB, :].astype(BF16),
                    kt_ref[g_i * HG_SUB:(g_i + 1) * HG_SUB, :]) for g_i in range(n_groups)]
    st = st_ref[...]
    for g_i in range(n_groups):
        r0 = g_i * HG_SUB
        p0 = g_i * HG_PROWS
        o_lo = jnp.zeros((half, HEAD_DIM), F32)
        o_hi = jnp.zeros((half, HEAD_DIM), F32)
        for s in range(half):
            v_s = bcast_row(i_ref, r0 + s)
            o_lo = o_lo + sc_ref[p0 + s * HG_SUB:p0 + s * HG_SUB + half, :] * v_s
            o_hi = o_hi + sc_ref[p0 + s * HG_SUB + half:p0 + (s + 1) * HG_SUB, :] * v_s
        for s in range(half):
            base = p0 + half * HG_SUB + s * half
            o_hi = o_hi + sc_ref[base:base + half, :] * bcast_row(i_ref, r0 + half + s)
        o_inter = _dot_nt(qt_ref[r0:r0 + HG_SUB, :], st.astype(BF16))
        acc_ref[r0:r0 + HG_SUB, :] = o_inter + jnp.concatenate([o_lo, o_hi], axis=0)
        st = st * dd_ref[r0:r0 + 1, :] + upds[g_i]
    st_ref[...] = st

    o = acc_ref[...]
    ms = jnp.mean(o * o, axis=-1, keepdims=True)
    o = o * lax.rsqrt(ms + EPS) * ng_ref[...]
    o_ref[...] = (o * _silu(gate_ref[...])).astype(o_ref.dtype)


def _hgrn(z, hg_lb, norm_g, layer, q_off, f_off, i_off, g_off, n_heads, rows=512):
    s_len = z.shape[0]
    depth = hg_lb.shape[0]
    qo, fo, io, go = (o // HEAD_DIM for o in (q_off, f_off, i_off, g_off))
    r = np.arange(LANES)
    same = (r[:, None] // HG_SUB) == (r[None, :] // HG_SUB)
    tb = jnp.asarray(np.concatenate([same & (r[None, :] <= r[:, None]), same]).astype(np.float32), BF16)
    ones = jnp.ones((HEAD_DIM, HEAD_DIM), BF16)
    half = HG_SUB // 2
    t_idx = np.arange(half)
    mask_np = np.where(t_idx[None, :, None] >= t_idx[:, None, None], 0.0, NEG_BIG)
    mask = jnp.asarray(np.broadcast_to(mask_np, (half, half, HEAD_DIM)).reshape(half * half, HEAD_DIM), F32)
    n_prows = (rows // HG_SUB) * HG_PROWS
    kern = functools.partial(_hgrn_kernel, layer=layer, rows=rows)
    blk_spec = lambda off: pl.BlockSpec((rows, HEAD_DIM), lambda h, i: (i, off + h))
    const = lambda shape: pl.BlockSpec(shape, lambda h, i: (0, 0))
    return pl.pallas_call(
        kern,
        grid=(n_heads, s_len // rows),
        in_specs=[blk_spec(qo), blk_spec(fo), blk_spec(io), blk_spec(go),
                  pl.BlockSpec((depth, HEAD_DIM), lambda h, i: (0, h)),
                  pl.BlockSpec((1, HEAD_DIM), lambda h, i: (0, h)),
                  const((2 * LANES, LANES)), const((HEAD_DIM, HEAD_DIM)),
                  const((half * half, HEAD_DIM))],
        out_specs=pl.BlockSpec((rows, HEAD_DIM), lambda h, i: (i, h)),
        out_shape=jax.ShapeDtypeStruct((s_len, n_heads * HEAD_DIM), BF16),
        scratch_shapes=[pltpu.VMEM((HEAD_DIM, HEAD_DIM), F32),
                        pltpu.VMEM((rows, HEAD_DIM), BF16),
                        pltpu.VMEM((rows, HEAD_DIM), BF16),
                        pltpu.VMEM((rows, HEAD_DIM), F32),
                        pltpu.VMEM((rows, HEAD_DIM), F32),
                        pltpu.VMEM((rows, HEAD_DIM), F32),
                        pltpu.VMEM((n_prows, HEAD_DIM), BF16),
                        pltpu.VMEM((n_prows, HEAD_DIM), F32),
                        pltpu.VMEM((rows, HEAD_DIM), F32)],
        compiler_params=_cparams(("parallel", "arbitrary")),
        name="hgrn2",
    )(z, z, z, z, hg_lb, norm_g.reshape(1, -1), tb, ones, mask)


def _gelu_tanh(x):
    c = math.sqrt(2.0 / math.pi)
    return 0.5 * x * (1.0 + jnp.tanh(c * (x + 0.044715 * (x * x * x))))


def _s5_kernel(u_ref, perm_ref, permt_ref, bm_ref, cm_ref, are_ref, aim_ref, pw_ref,
               d_ref, o_ref, state_ref, x_ref, yp_ref, *, rows, half):
    @pl.when(pl.program_id(0) == 0)
    def _():
        state_ref[...] = jnp.zeros_like(state_ref)

    nblk = bm_ref.shape[0]
    nt = rows // 8
    u = u_ref[...]
    up = _dot(perm_ref[...], u.astype(BF16)).astype(BF16)
    sub = lax.broadcasted_iota(jnp.int32, (8, half), 0)

    re, im = slice(0, half), slice(half, 2 * half)
    for b in range(nblk):
        x_ref[b] = _dot(up[:, b * LANES:(b + 1) * LANES], bm_ref[b])
        are = are_ref[b]
        aim = aim_ref[b]
        xr = jnp.zeros((8, half), F32)
        xi = jnp.zeros((8, half), F32)
        for t in range(nt):
            rs = slice(t * 8, (t + 1) * 8)
            xr, xi = (are * xr - aim * xi + x_ref[b, rs, re],
                      are * xi + aim * xr + x_ref[b, rs, im])
            x_ref[b, rs, re] = xr
            x_ref[b, rs, im] = xi

        er, ei = xr, xi
        alre = pw_ref[b, nt - 1:nt, re]
        alim = pw_ref[b, nt - 1:nt, im]
        cr = state_ref[b, 0:1, re]
        ci = state_ref[b, 0:1, im]
        ctr = jnp.zeros((8, half), F32)
        cti = jnp.zeros((8, half), F32)
        for s in range(8):
            ctr = jnp.where(sub == s, cr, ctr)
            cti = jnp.where(sub == s, ci, cti)
            cr, ci = (alre * cr - alim * ci + er[s:s + 1, :],
                      alre * ci + alim * cr + ei[s:s + 1, :])
        state_ref[b, :, re] = jnp.broadcast_to(cr, (8, half))
        state_ref[b, :, im] = jnp.broadcast_to(ci, (8, half))

        for t in range(nt):
            rs = slice(t * 8, (t + 1) * 8)
            pr = pw_ref[b, t:t + 1, re]
            pi = pw_ref[b, t:t + 1, im]
            x_ref[b, rs, re] = x_ref[b, rs, re] + (pr * ctr - pi * cti)
            x_ref[b, rs, im] = x_ref[b, rs, im] + (pr * cti + pi * ctr)
        yp_ref[:, b * LANES:(b + 1) * LANES] = _dot(x_ref[b].astype(BF16), cm_ref[b])

    y = _dot01(permt_ref[...], yp_ref[...]) + d_ref[...] * u
    o_ref[...] = _gelu_tanh(y).astype(o_ref.dtype)


def _s5(z, u_off, bm, cm, are, aim, pw, d_skip, rows=256):
    s_len = z.shape[0]
    width = d_skip.shape[0]
    nblk, _, two_half = bm.shape
    half = two_half // 2
    seg = rows // 8
    rho = np.arange(rows)
    t_of = (rho % 8) * seg + rho // 8
    perm_np = np.zeros((rows, rows), np.float32)
    perm_np[rho, t_of] = 1.0
    perm = jnp.asarray(perm_np, BF16)
    permt = jnp.asarray(perm_np.T, BF16)
    uo = u_off // width
    kern = functools.partial(_s5_kernel, rows=rows, half=half)
    c2 = lambda shape: pl.BlockSpec(shape, lambda i: (0, 0))
    c3 = lambda shape: pl.BlockSpec(shape, lambda i: (0, 0, 0))
    return pl.pallas_call(
        kern,
        grid=(s_len // rows,),
        in_specs=[pl.BlockSpec((rows, width), lambda i: (i, uo)),
                  c2((rows, rows)), c2((rows, rows)),
                  c3(bm.shape), c3(cm.shape),
                  c3(are.shape), c3(aim.shape), c3(pw.shape),
                  c2((1, width))],
        out_specs=pl.BlockSpec((rows, width), lambda i: (i, 0)),
        out_shape=jax.ShapeDtypeStruct((s_len, width), BF16),
        scratch_shapes=[pltpu.VMEM((nblk, 8, two_half), F32),
                        pltpu.VMEM((nblk, rows, two_half), F32),
                        pltpu.VMEM((rows, width), F32)],
        compiler_params=_cparams(("arbitrary",)),
        name="s5",
    )(z, perm, permt, bm, cm, are, aim, pw, d_skip.reshape(1, width))


def _s5_params(a_re, a_im, log_dt, b_re, b_im, c_re, c_im, seg):
    g_n, p_n = a_re.shape
    nblk = g_n // S5_GPB
    dt = jnp.exp(log_dt)[:, None]
    mag = jnp.exp(a_re * dt)
    ang = a_im * dt
    abar_re = mag * jnp.cos(ang)
    abar_im = mag * jnp.sin(ang)
    nr = abar_re - 1.0
    den = a_re * a_re + a_im * a_im
    zr = (nr * a_re + abar_im * a_im) / den
    zi = (abar_im * a_re - nr * a_im) / den
    bbar_re = zr[:, :, None] * b_re - zi[:, :, None] * b_im
    bbar_im = zr[:, :, None] * b_im + zi[:, :, None] * b_re
    eye = jnp.eye(S5_GPB, dtype=F32)

    def embed_b(bb):
        bb = bb.reshape(nblk, S5_GPB, p_n, S5_GROUP)
        m = jnp.einsum('bgpc,gh->bgchp', bb, eye)
        return m.reshape(nblk, S5_GPB * S5_GROUP, S5_GPB * p_n)

    bm = jnp.concatenate([embed_b(bbar_re), embed_b(bbar_im)], axis=2).astype(BF16)

    def embed_c(cc):
        cc = cc.reshape(nblk, S5_GPB, S5_GROUP, p_n)
        m = jnp.einsum('bgcp,gh->bgphc', cc, eye)
        return m.reshape(nblk, S5_GPB * p_n, S5_GPB * S5_GROUP)

    cm = jnp.concatenate([embed_c(c_re), embed_c(-c_im)], axis=1).astype(BF16)

    def tile8(v):
        v = v.reshape(nblk, 1, S5_GPB * p_n)
        return jnp.broadcast_to(v, (nblk, 8, S5_GPB * p_n))

    pr, pi = abar_re, abar_im
    pows_re, pows_im = [pr], [pi]
    for _ in range(seg - 1):
        pr, pi = pr * abar_re - pi * abar_im, pr * abar_im + pi * abar_re
        pows_re.append(pr)
        pows_im.append(pi)

    def table(ps):
        return jnp.stack(ps).reshape(seg, nblk, S5_GPB * p_n).transpose(1, 0, 2)

    pw = jnp.concatenate([table(pows_re), table(pows_im)], axis=2)
    return bm, cm, tile8(abar_re), tile8(abar_im), pw


def _glu_kernel(y_ref, w_ref, gate_ref, o_ref, *, width):
    zg = _dot(y_ref[...], w_ref[...])
    o = zg[:, :width] * _sigmoid(zg[:, width:]) * _silu(gate_ref[...])
    o_ref[...] = o.astype(o_ref.dtype)


def _glu(y, w_glu, z, g_off, tm=512):
    m, width = y.shape
    go = g_off // width
    return pl.pallas_call(
        functools.partial(_glu_kernel, width=width),
        grid=(m // tm,),
        in_specs=[pl.BlockSpec((tm, width), lambda i: (i, 0)),
                  pl.BlockSpec((width, 2 * width), lambda i: (0, 0)),
                  pl.BlockSpec((tm, width), lambda i: (i, go))],
        out_specs=pl.BlockSpec((tm, width), lambda i: (i, 0)),
        out_shape=jax.ShapeDtypeStruct((m, width), BF16),
        compiler_params=_cparams(("parallel",)),
        name="s5_glu",
    )(y, w_glu, z)


def _merge_kernel(oa_ref, ob_ref, oc_ref, ga_ref, gb_ref, gc_ref, bg_ref, wa_ref, wb_ref, wc_ref,
                  wo_ref, x_ref, ng_ref, xo_ref, ho_ref, *, d):
    def gate(g_ref, i):
        return _sigmoid(g_ref[...] + bg_ref[:, i * d:(i + 1) * d])

    merged = gate(ga_ref, 0) * _dot(oa_ref[...], wa_ref[...])
    merged = merged + gate(gb_ref, 1) * _dot(ob_ref[...], wb_ref[...])
    merged = merged + gate(gc_ref, 2) * _dot(oc_ref[...], wc_ref[...])
    xn = x_ref[...] + _dot(merged.astype(BF16), wo_ref[...])
    xo_ref[...] = xn
    ms = jnp.mean(xn * xn, axis=-1, keepdims=True)
    ho_ref[...] = (xn * lax.rsqrt(ms + EPS) * ng_ref[...]).astype(ho_ref.dtype)


def _merge(oa, ob, oc, z, mg_off, b_gate, wa, wb, wc, wo, x, next_g, h_dtype, tm=256):
    m, d = x.shape
    w = oa.shape[1]
    assert mg_off % d == 0
    mo = mg_off // d
    row = lambda width: pl.BlockSpec((tm, width), lambda i: (i, 0))
    gate = lambda k: pl.BlockSpec((tm, d), lambda i: (i, mo + k))
    resident = lambda shape: pl.BlockSpec(shape, lambda i: (0, 0), pipeline_mode=pl.Buffered(1))
    return pl.pallas_call(
        functools.partial(_merge_kernel, d=d),
        grid=(m // tm,),
        in_specs=[row(w), row(w), row(w), gate(0), gate(1), gate(2),
                  resident((1, N_BRANCH * d)),
                  resident((w, d)), resident((w, d)), resident((w, d)), resident((d, d)),
                  row(d), resident((1, d))],
        out_specs=[row(d), row(d)],
        out_shape=[jax.ShapeDtypeStruct((m, d), F32), jax.ShapeDtypeStruct((m, d), h_dtype)],
        compiler_params=_cparams(("parallel",)),
        name="merge_out",
    )(oa, ob, oc, z, z, z, b_gate.reshape(1, -1), wa, wb, wc, wo, x, next_g.reshape(1, -1))


def kernel(x, norm_g, w_in, b_gate, fox_bf, hg_lb, hg_norm_g, s5_a_re, s5_a_im, s5_log_dt,
           s5_b_re, s5_b_im, s5_c_re, s5_c_im, s5_d, s5_w_glu, w_br_a, w_br_b, w_br_c, w_out,
           final_g):
    bsz, s_len, d = x.shape
    depth = w_in.shape[0]
    hg_w = hg_lb.shape[1]
    s5_w = s5_d.shape[1]
    n_fox = fox_bf.shape[1]
    fox_w = n_fox * HEAD_DIM
    n_hg = hg_w // HEAD_DIM
    sizes = (hg_w, hg_w, hg_w, hg_w, s5_w, s5_w, fox_w, fox_w, fox_w, n_fox, fox_w, N_BRANCH * d)
    offs = np.concatenate([[0], np.cumsum(sizes)])
    (o_hq, o_hf, o_hi, o_hg, o_su, o_sg, o_fq, o_fk, o_fv, o_ff, o_fg, o_mg, o_end) = (int(v) for v in offs)
    o_mg2 = o_fq
    o_fg2 = o_mg2 + (o_end - o_mg)
    n_f32 = o_fg2 + fox_w
    s5_rows = 256
    outs = []
    for b in range(bsz):
        xb = x[b]
        h = _rmsnorm(xb, norm_g[0], BF16)
        for l in range(depth):
            wl = w_in[l]
            w_main = jnp.concatenate([wl[:, :o_fq], wl[:, o_mg:], wl[:, o_fg:o_mg], wl[:, o_fq:o_ff]],
                                     axis=1).astype(BF16)
            w_ff = jnp.pad(wl[:, o_ff:o_fg], ((0, 0), (0, LANES - n_fox))).astype(BF16)
            b_ff = jnp.pad(fox_bf[l], (0, LANES - n_fox)).reshape(1, LANES)
            z = _inproj(h, w_main, 0, n_f32, F32)
            zqkv = _inproj(h, w_main, n_f32, 3 * fox_w, BF16,
                           first_tile_scale=HEAD_DIM ** -0.5 * LOG2E, tn=fox_w)

            o_a = _hgrn(z, hg_lb, hg_norm_g[l], l, o_hq, o_hf, o_hi, o_hg, n_hg)

            s5p = _s5_params(s5_a_re[l], s5_a_im[l], s5_log_dt[l], s5_b_re[l], s5_b_im[l],
                             s5_c_re[l], s5_c_im[l], s5_rows // 8)
            y_b = _s5(z, o_su, *s5p, s5_d[l], rows=s5_rows)
            o_b = _glu(y_b, s5_w_glu[l].astype(BF16), z, o_sg)

            c = _fox_cumlog(h, w_ff, b_ff)
            ct = c[:, :n_fox].T
            o_c = _fox_attn(zqkv, z, ct[:, :, None], ct[:, None, :], o_fg2, n_fox)

            last = l == depth - 1
            next_g = final_g if last else norm_g[l + 1]
            xb, h = _merge(o_a, o_b, o_c, z, o_mg2, b_gate[l],
                           w_br_a[l].astype(BF16), w_br_b[l].astype(BF16), w_br_c[l].astype(BF16),
                           w_out[l].astype(BF16), xb, next_g, F32 if last else BF16)
        outs.append(h)
    return outs[0][None] if bsz == 1 else jnp.stack(outs, axis=0)
```

```python
import functools
import math

import numpy as np
import jax
import jax.numpy as jnp
from jax import lax
from jax.experimental import pallas as pl
from jax.experimental.pallas import tpu as pltpu

F32 = jnp.float32
BF16 = jnp.bfloat16

EPS = 1e-6
LANES = 128
VMEM_LIMIT = 56 * 1024 * 1024

HEAD_DIM = 128
S5_GROUP = 16
S5_STATE = 64
S5_GPB = 8
N_BRANCH = 3

HG_SUB = 16


def _cparams(sem, vmem=VMEM_LIMIT):
    return pltpu.CompilerParams(dimension_semantics=sem, vmem_limit_bytes=vmem)


def _dot(a, b):
    return jnp.dot(a, b, preferred_element_type=F32)


def _dot_nt(a, b):
    return lax.dot_general(a, b, (((1,), (1,)), ((), ())), preferred_element_type=F32)


def _dot_tn(a, b):
    return lax.dot_general(a, b, (((0,), (0,)), ((), ())), preferred_element_type=F32)


def _split3(x):
    hi = x.astype(BF16)
    r1 = x - hi.astype(F32)
    mid = r1.astype(BF16)
    lo = (r1 - mid.astype(F32)).astype(BF16)
    return hi, mid, lo


def _dot01(m, x):
    hi, mid, lo = _split3(x)
    return _dot(m, hi) + _dot(m, mid) + _dot(m, lo)


def _log_sigmoid(z):
    return jnp.minimum(z, 0.0) - jnp.log1p(jnp.exp(-jnp.abs(z)))


def _sigmoid(z):
    return 1.0 / (1.0 + jnp.exp(-z))


def _silu(z):
    return z * _sigmoid(z)


def _rmsnorm_kernel(x_ref, g_ref, o_ref):
    x = x_ref[...]
    ms = jnp.mean(x * x, axis=-1, keepdims=True)
    o_ref[...] = (x * lax.rsqrt(ms + EPS) * g_ref[...]).astype(o_ref.dtype)


def _rmsnorm(x, g, out_dtype, tm=512):
    m, d = x.shape
    return pl.pallas_call(
        _rmsnorm_kernel,
        grid=(m // tm,),
        in_specs=[pl.BlockSpec((tm, d), lambda i: (i, 0)),
                  pl.BlockSpec((1, d), lambda i: (0, 0))],
        out_specs=pl.BlockSpec((tm, d), lambda i: (i, 0)),
        out_shape=jax.ShapeDtypeStruct((m, d), out_dtype),
        compiler_params=_cparams(("parallel",)),
        name="rmsnorm",
    )(x, g.reshape(1, d))


def _inproj_kernel(h_ref, w_ref, o_ref, wb_ref, *, first_tile_scale):
    @pl.when(pl.program_id(1) == 0)
    def _():
        wb_ref[...] = w_ref[...].astype(BF16)

    acc = _dot(h_ref[...], wb_ref[...])
    if first_tile_scale is not None:
        acc = acc * jnp.where(pl.program_id(0) == 0, first_tile_scale, 1.0)
    o_ref[...] = acc.astype(o_ref.dtype)


def _inproj(h, w, layer, col0, ncols, out_dtype, first_tile_scale=None, tm=512, tn=1024):
    m, k = h.shape
    assert col0 % tn == 0 and ncols % tn == 0
    j0 = col0 // tn
    return pl.pallas_call(
        functools.partial(_inproj_kernel, first_tile_scale=first_tile_scale),
        grid=(ncols // tn, m // tm),
        in_specs=[pl.BlockSpec((tm, k), lambda j, i: (i, 0)),
                  pl.BlockSpec((None, k, tn), lambda j, i: (layer, 0, j0 + j))],
        out_specs=pl.BlockSpec((tm, tn), lambda j, i: (i, j)),
        out_shape=jax.ShapeDtypeStruct((m, ncols), out_dtype),
        scratch_shapes=[pltpu.VMEM((k, tn), BF16)],
        compiler_params=_cparams(("parallel", "arbitrary")),
        name="inproj",
    )(h, w)


def _foxc_kernel(h_ref, w_ref, b_ref, tri_ref, c_ref, carry_ref):
    @pl.when(pl.program_id(0) == 0)
    def _():
        carry_ref[...] = jnp.zeros_like(carry_ref)

    logits = _dot(h_ref[...], w_ref[...]) + b_ref[...]
    ls = _log_sigmoid(logits)
    cum = _dot01(tri_ref[...], ls) + carry_ref[0:1, :]
    c_ref[...] = cum
    tm = cum.shape[0]
    carry_ref[...] = jnp.broadcast_to(cum[tm - 1:tm, :], carry_ref.shape)


def _fox_cumlog(h, w_ff, b_ff, tm=512):
    m, k = h.shape
    tri = jnp.asarray(np.tril(np.ones((tm, tm), np.float32)), BF16)
    return pl.pallas_call(
        _foxc_kernel,
        grid=(m // tm,),
        in_specs=[pl.BlockSpec((tm, k), lambda i: (i, 0)),
                  pl.BlockSpec((k, LANES), lambda i: (0, 0)),
                  pl.BlockSpec((1, LANES), lambda i: (0, 0)),
                  pl.BlockSpec((tm, tm), lambda i: (0, 0))],
        out_specs=pl.BlockSpec((tm, LANES), lambda i: (i, 0)),
        out_shape=jax.ShapeDtypeStruct((m, LANES), F32),
        scratch_shapes=[pltpu.VMEM((8, LANES), F32)],
        compiler_params=_cparams(("arbitrary",)),
        name="fox_cumlog",
    )(h, w_ff, b_ff, tri)


NEG_BIG = -1e30


LOG2E = 1.0 / math.log(2.0)
ATT_ROWS = 64
ATT_UNROLL = 2


def _fox_attn_kernel(tab_ref, q_ref, k_ref, v_ref, ccol_ref, crow_ref, gate_ref, o_ref,
                     va_ref, sa_ref, sb_ref, pa_ref, pb_ref, ala_ref, alb_ref, acc_ref, m_ref,
                     *, tq, n_off, n_diag):
    @pl.when(pl.program_id(0) == 0)
    def _():
        va_ref[:, HEAD_DIM:2 * HEAD_DIM] = jnp.ones((va_ref.shape[0], HEAD_DIM), BF16)

    va_ref[:, 0:HEAD_DIM] = v_ref[...]
    m_ref[...] = jnp.full(m_ref.shape, NEG_BIG, F32)
    acc_ref[...] = jnp.zeros(acc_ref.shape, F32)

    def tile(n):
        q0 = pl.multiple_of(tab_ref[0, n] * tq, tq)
        k0 = pl.multiple_of(tab_ref[1, n] * tq, tq)
        return q0, k0

    def logits(n, s_ref):
        q0, k0 = tile(n)
        s_ref[...] = _dot_nt(q_ref[pl.ds(q0, tq), :], k_ref[pl.ds(k0, tq), :])

    def softmax(n, s_ref, p_ref, al_ref, masked):
        q0, k0 = tile(n)
        crow = crow_ref[0, :, pl.ds(k0, tq)] * LOG2E
        for r in range(tq // ATT_ROWS):
            rs = slice(r * ATT_ROWS, (r + 1) * ATT_ROWS)
            qs = pl.ds(q0 + r * ATT_ROWS, ATT_ROWS)
            s = s_ref[rs, :] - crow
            if masked:
                row = lax.broadcasted_iota(jnp.int32, (ATT_ROWS, tq), 0) + r * ATT_ROWS
                col = lax.broadcasted_iota(jnp.int32, (ATT_ROWS, tq), 1)
                s = jnp.where(col <= row, s, NEG_BIG)
            ct = ccol_ref[0, qs, :] * LOG2E
            m_old = m_ref[qs, :]
            m_new = jnp.maximum(m_old, jnp.max(s, axis=1, keepdims=True) + ct)
            p_ref[rs, :] = jnp.exp2(s - (m_new - ct)).astype(BF16)
            m_ref[qs, :] = m_new
            al_ref[rs, :] = jnp.exp2(m_old - m_new)

    def accumulate(n, p_ref, al_ref):
        q0, k0 = tile(n)
        qs = pl.ds(q0, tq)
        acc_ref[qs, :] = al_ref[...] * acc_ref[qs, :] + _dot(p_ref[...], va_ref[pl.ds(k0, tq), :])

    def run(first, count, masked):
        if count == 0:
            return
        last = first + count - 1
        nxt = lambda n: jnp.minimum(n, last)
        s_buf = (sa_ref, sb_ref)
        p_buf = ((pa_ref, ala_ref), (pb_ref, alb_ref))
        logits(first, s_buf[0])
        softmax(first, s_buf[0], *p_buf[0], masked)
        logits(nxt(first + 1), s_buf[1])
        n_loop = (count - 1) // ATT_UNROLL

        def body(j, carry):
            n = first + ATT_UNROLL * j
            for u in range(ATT_UNROLL):
                accumulate(n + u, *p_buf[u % 2])
                softmax(n + u + 1, s_buf[(u + 1) % 2], *p_buf[(u + 1) % 2], masked)
                logits(nxt(n + u + 2), s_buf[u % 2])
            return carry

        lax.fori_loop(0, n_loop, body, 0)
        n = first + ATT_UNROLL * n_loop
        rest = count - 1 - ATT_UNROLL * n_loop
        for u in range(rest + 1):
            accumulate(n + u, *p_buf[u % 2])
            if u + 1 <= rest:
                softmax(n + u + 1, s_buf[(u + 1) % 2], *p_buf[(u + 1) % 2], masked)
            if u + 2 <= rest:
                logits(n + u + 2, s_buf[u % 2])

    run(0, n_off, False)
    run(n_off, n_diag, True)

    acc = acc_ref[...]
    out = acc[:, 0:HEAD_DIM] / acc[:, HEAD_DIM:2 * HEAD_DIM]
    o_ref[...] = (out * _silu(gate_ref[...])).astype(o_ref.dtype)


def _fox_attn(zqkv, z, ccol, crow, g_off, n_heads, tq=512):
    s_len = zqkv.shape[0]
    nq = s_len // tq
    go = g_off // HEAD_DIM
    off = [(qi, kb) for kb in range(nq) for qi in range(kb + 1, nq)]
    diag = [(i, i) for i in range(nq)]
    tab = jnp.asarray(np.array(off + diag, np.int32).T)
    kern = functools.partial(_fox_attn_kernel, tq=tq, n_off=len(off), n_diag=len(diag))
    once = pl.Buffered(1)
    head_col = lambda base: pl.BlockSpec((s_len, HEAD_DIM), lambda h, t: (0, base + h),
                                         pipeline_mode=once)
    grid_spec = pltpu.PrefetchScalarGridSpec(
        num_scalar_prefetch=1,
        grid=(n_heads,),
        in_specs=[head_col(0), head_col(n_heads), head_col(2 * n_heads),
                  pl.BlockSpec((1, s_len, 1), lambda h, t: (h, 0, 0), pipeline_mode=once),
                  pl.BlockSpec((1, 1, s_len), lambda h, t: (h, 0, 0)),
                  head_col(go)],
        out_specs=pl.BlockSpec((s_len, HEAD_DIM), lambda h, t: (0, h)),
        scratch_shapes=[pltpu.VMEM((s_len, 2 * HEAD_DIM), BF16),
                        pltpu.VMEM((tq, tq), F32),
                        pltpu.VMEM((tq, tq), F32),
                        pltpu.VMEM((tq, tq), BF16),
                        pltpu.VMEM((tq, tq), BF16),
                        pltpu.VMEM((tq, 1), F32),
                        pltpu.VMEM((tq, 1), F32),
                        pltpu.VMEM((s_len, 2 * HEAD_DIM), F32),
                        pltpu.VMEM((s_len, 1), F32)])
    return pl.pallas_call(
        kern,
        grid_spec=grid_spec,
        out_shape=jax.ShapeDtypeStruct((s_len, n_heads * HEAD_DIM), BF16),
        compiler_params=_cparams(("arbitrary",)),
        name="fox_attn",
    )(tab, zqkv, zqkv, zqkv, ccol, crow, z)


HG_PROWS = 8 * HG_SUB + 8 * (HG_SUB // 2)


def _hgrn_kernel(q_ref, f_ref, i_ref, gate_ref, lb_ref, ng_ref, tb_ref, ones_ref,
                 mask_ref, o_ref, st_ref, qt_ref, kt_ref, w_ref, cum_ref, dd_ref, p_ref, sc_ref,
                 acc_ref, *, layer, rows):
    @pl.when(pl.program_id(1) == 0)
    def _():
        st_ref[...] = jnp.zeros_like(st_ref)

    z = f_ref[...]
    ls = _log_sigmoid(z)
    if layer == 0:
        g = ls
        logk = ls - z
    else:
        lbp = lb_ref[...]
        e = jnp.exp(lbp - jnp.max(lbp, axis=0, keepdims=True))
        p = e / jnp.sum(e, axis=0, keepdims=True)
        lb = jnp.sum(p[1:layer + 1, :], axis=0, keepdims=True)
        a = jnp.log(lb)
        l1m = jnp.log1p(-lb)
        b = l1m + ls
        g = jnp.maximum(a, b) + jnp.log1p(jnp.exp(-jnp.abs(a - b)))
        logk = l1m + (ls - z)

    cums, tots = [], []
    for r0 in range(0, rows, LANES):
        ct = _dot01(tb_ref[...], g[r0:r0 + LANES, :])
        cums.append(ct[0:LANES, :])
        tots.append(ct[LANES:2 * LANES, :])
    cum = jnp.concatenate(cums, axis=0) * LOG2E
    tot = jnp.concatenate(tots, axis=0) * LOG2E
    w = cum - logk * LOG2E
    qt_ref[...] = (q_ref[...] * jnp.exp2(cum)).astype(BF16)
    kt_ref[...] = jnp.exp2(tot - w).astype(BF16)
    w_ref[...] = w
    cum_ref[...] = cum
    dd_ref[...] = jnp.exp2(tot)

    half = HG_SUB // 2
    n_groups = rows // HG_SUB

    def bcast_row(ref, r):
        return jnp.broadcast_to(ref[r:r + 1, :], (half, HEAD_DIM))

    for g_i in range(n_groups):
        r0 = g_i * HG_SUB
        p0 = g_i * HG_PROWS
        c_lo, c_hi = cum_ref[r0:r0 + half, :], cum_ref[r0 + half:r0 + HG_SUB, :]
        q_lo, q_hi = q_ref[r0:r0 + half, :], q_ref[r0 + half:r0 + HG_SUB, :]
        for s in range(half):
            w_s = bcast_row(w_ref, r0 + s)
            p_lo = q_lo * jnp.exp2(c_lo - w_s + mask_ref[s * half:(s + 1) * half, :])
            p_hi = q_hi * jnp.exp2(c_hi - w_s)
            p_ref[p0 + s * HG_SUB:p0 + (s + 1) * HG_SUB, :] = (
                jnp.concatenate([p_lo, p_hi], axis=0).astype(BF16))
        for s in range(0, half, 2):
            pa = q_hi * jnp.exp2(c_hi - bcast_row(w_ref, r0 + half + s)
                                + mask_ref[s * half:(s + 1) * half, :])
            pb = q_hi * jnp.exp2(c_hi - bcast_row(w_ref, r0 + half + s + 1)
                                + mask_ref[(s + 1) * half:(s + 2) * half, :])
            base = p0 + half * HG_SUB + s * half
            p_ref[base:base + HG_SUB, :] = jnp.concatenate([pa, pb], axis=0).astype(BF16)

    sc_ref[...] = _dot(p_ref[...], ones_ref[...])

    upds = [_dot_tn(i_ref[g_i * HG_SUB:(g_i + 1) * HG_SUB, :].astype(BF16),
                    kt_ref[g_i * HG_SUB:(g_i + 1) * HG_SUB, :]) for g_i in range(n_groups)]
    st = st_ref[...]
    for g_i in range(n_groups):
        r0 = g_i * HG_SUB
        p0 = g_i * HG_PROWS
        o_lo = jnp.zeros((half, HEAD_DIM), F32)
        o_hi = jnp.zeros((half, HEAD_DIM), F32)
        for s in range(half):
            v_s = bcast_row(i_ref, r0 + s)
            o_lo = o_lo + sc_ref[p0 + s * HG_SUB:p0 + s * HG_SUB + half, :] * v_s
            o_hi = o_hi + sc_ref[p0 + s * HG_SUB + half:p0 + (s + 1) * HG_SUB, :] * v_s
        for s in range(half):
            base = p0 + half * HG_SUB + s * half
            o_hi = o_hi + sc_ref[base:base + half, :] * bcast_row(i_ref, r0 + half + s)
        o_inter = _dot_nt(qt_ref[r0:r0 + HG_SUB, :], st.astype(BF16))
        acc_ref[r0:r0 + HG_SUB, :] = o_inter + jnp.concatenate([o_lo, o_hi], axis=0)
        st = st * dd_ref[r0:r0 + 1, :] + upds[g_i]
    st_ref[...] = st

    o = acc_ref[...]
    ms = jnp.mean(o * o, axis=-1, keepdims=True)
    o = o * lax.rsqrt(ms + EPS) * ng_ref[...]
    o_ref[...] = (o * _silu(gate_ref[...])).astype(o_ref.dtype)


def _hgrn(z, hg_lb, norm_g, layer, q_off, f_off, i_off, g_off, n_heads, rows=512):
    s_len = z.shape[0]
    depth = hg_lb.shape[0]
    qo, fo, io, go = (o // HEAD_DIM for o in (q_off, f_off, i_off, g_off))
    r = np.arange(LANES)
    same = (r[:, None] // HG_SUB) == (r[None, :] // HG_SUB)
    tb = jnp.asarray(np.concatenate([same & (r[None, :] <= r[:, None]), same]).astype(np.float32), BF16)
    ones = jnp.ones((HEAD_DIM, HEAD_DIM), BF16)
    half = HG_SUB // 2
    t_idx = np.arange(half)
    mask_np = np.where(t_idx[None, :, None] >= t_idx[:, None, None], 0.0, NEG_BIG)
    mask = jnp.asarray(np.broadcast_to(mask_np, (half, half, HEAD_DIM)).reshape(half * half, HEAD_DIM), F32)
    n_prows = (rows // HG_SUB) * HG_PROWS
    kern = functools.partial(_hgrn_kernel, layer=layer, rows=rows)
    blk_spec = lambda off: pl.BlockSpec((rows, HEAD_DIM), lambda h, i: (i, off + h))
    const = lambda shape: pl.BlockSpec(shape, lambda h, i: (0, 0))
    return pl.pallas_call(
        kern,
        grid=(n_heads, s_len // rows),
        in_specs=[blk_spec(qo), blk_spec(fo), blk_spec(io), blk_spec(go),
                  pl.BlockSpec((depth, HEAD_DIM), lambda h, i: (0, h)),
                  pl.BlockSpec((1, HEAD_DIM), lambda h, i: (0, h)),
                  const((2 * LANES, LANES)), const((HEAD_DIM, HEAD_DIM)),
                  const((half * half, HEAD_DIM))],
        out_specs=pl.BlockSpec((rows, HEAD_DIM), lambda h, i: (i, h)),
        out_shape=jax.ShapeDtypeStruct((s_len, n_heads * HEAD_DIM), BF16),
        scratch_shapes=[pltpu.VMEM((HEAD_DIM, HEAD_DIM), F32),
                        pltpu.VMEM((rows, HEAD_DIM), BF16),
                        pltpu.VMEM((rows, HEAD_DIM), BF16),
                        pltpu.VMEM((rows, HEAD_DIM), F32),
                        pltpu.VMEM((rows, HEAD_DIM), F32),
                        pltpu.VMEM((rows, HEAD_DIM), F32),
                        pltpu.VMEM((n_prows, HEAD_DIM), BF16),
                        pltpu.VMEM((n_prows, HEAD_DIM), F32),
                        pltpu.VMEM((rows, HEAD_DIM), F32)],
        compiler_params=_cparams(("parallel", "arbitrary")),
        name="hgrn2",
    )(z, z, z, z, hg_lb, norm_g.reshape(1, -1), tb, ones, mask)


def _gelu_tanh(x):
    c = math.sqrt(2.0 / math.pi)
    return 0.5 * x * (1.0 + jnp.tanh(c * (x + 0.044715 * (x * x * x))))


def _s5_kernel(u_ref, perm_ref, permt_ref, bm_ref, cm_ref, are_ref, aim_ref, pw_ref,
               d_ref, o_ref, state_ref, x_ref, yp_ref, *, rows, half):
    @pl.when(pl.program_id(0) == 0)
    def _():
        state_ref[...] = jnp.zeros_like(state_ref)

    nblk = bm_ref.shape[0]
    nt = rows // 8
    u = u_ref[...]
    up = _dot(perm_ref[...], u.astype(BF16)).astype(BF16)
    sub = lax.broadcasted_iota(jnp.int32, (8, half), 0)

    re, im = slice(0, half), slice(half, 2 * half)
    for b in range(nblk):
        x_ref[b] = _dot(up[:, b * LANES:(b + 1) * LANES], bm_ref[b])
        are = are_ref[b]
        aim = aim_ref[b]
        xr = jnp.zeros((8, half), F32)
        xi = jnp.zeros((8, half), F32)
        for t in range(nt):
            rs = slice(t * 8, (t + 1) * 8)
            xr, xi = (are * xr - aim * xi + x_ref[b, rs, re],
                      are * xi + aim * xr + x_ref[b, rs, im])
            x_ref[b, rs, re] = xr
            x_ref[b, rs, im] = xi

        er, ei = xr, xi
        alre = pw_ref[b, nt - 1:nt, re]
        alim = pw_ref[b, nt - 1:nt, im]
        cr = state_ref[b, 0:1, re]
        ci = state_ref[b, 0:1, im]
        ctr = jnp.zeros((8, half), F32)
        cti = jnp.zeros((8, half), F32)
        for s in range(8):
            ctr = jnp.where(sub == s, cr, ctr)
            cti = jnp.where(sub == s, ci, cti)
            cr, ci = (alre * cr - alim * ci + er[s:s + 1, :],
                      alre * ci + alim * cr + ei[s:s + 1, :])
        state_ref[b, :, re] = jnp.broadcast_to(cr, (8, half))
        state_ref[b, :, im] = jnp.broadcast_to(ci, (8, half))

        for t in range(nt):
            rs = slice(t * 8, (t + 1) * 8)
            pr = pw_ref[b, t:t + 1, re]
            pi = pw_ref[b, t:t + 1, im]
            x_ref[b, rs, re] = x_ref[b, rs, re] + (pr * ctr - pi * cti)
            x_ref[b, rs, im] = x_ref[b, rs, im] + (pr * cti + pi * ctr)
        yp_ref[:, b * LANES:(b + 1) * LANES] = _dot(x_ref[b].astype(BF16), cm_ref[b])

    y = _dot01(permt_ref[...], yp_ref[...]) + d_ref[...] * u
    o_ref[...] = _gelu_tanh(y).astype(o_ref.dtype)


def _s5(z, u_off, bm, cm, are, aim, pw, d_skip, rows=256):
    s_len = z.shape[0]
    width = d_skip.shape[0]
    nblk, _, two_half = bm.shape
    half = two_half // 2
    seg = rows // 8
    rho = np.arange(rows)
    t_of = (rho % 8) * seg + rho // 8
    perm_np = np.zeros((rows, rows), np.float32)
    perm_np[rho, t_of] = 1.0
    perm = jnp.asarray(perm_np, BF16)
    permt = jnp.asarray(perm_np.T, BF16)
    uo = u_off // width
    kern = functools.partial(_s5_kernel, rows=rows, half=half)
    c2 = lambda shape: pl.BlockSpec(shape, lambda i: (0, 0))
    c3 = lambda shape: pl.BlockSpec(shape, lambda i: (0, 0, 0))
    return pl.pallas_call(
        kern,
        grid=(s_len // rows,),
        in_specs=[pl.BlockSpec((rows, width), lambda i: (i, uo)),
                  c2((rows, rows)), c2((rows, rows)),
                  c3(bm.shape), c3(cm.shape),
                  c3(are.shape), c3(aim.shape), c3(pw.shape),
                  c2((1, width))],
        out_specs=pl.BlockSpec((rows, width), lambda i: (i, 0)),
        out_shape=jax.ShapeDtypeStruct((s_len, width), BF16),
        scratch_shapes=[pltpu.VMEM((nblk, 8, two_half), F32),
                        pltpu.VMEM((nblk, rows, two_half), F32),
                        pltpu.VMEM((rows, width), F32)],
        compiler_params=_cparams(("arbitrary",)),
        name="s5",
    )(z, perm, permt, bm, cm, are, aim, pw, d_skip.reshape(1, width))


def _s5_params(a_re, a_im, log_dt, b_re, b_im, c_re, c_im, seg):
    g_n, p_n = a_re.shape
    nblk = g_n // S5_GPB
    dt = jnp.exp(log_dt)[:, None]
    mag = jnp.exp(a_re * dt)
    ang = a_im * dt
    abar_re = mag * jnp.cos(ang)
    abar_im = mag * jnp.sin(ang)
    nr = abar_re - 1.0
    den = a_re * a_re + a_im * a_im
    zr = (nr * a_re + abar_im * a_im) / den
    zi = (abar_im * a_re - nr * a_im) / den
    bbar_re = zr[:, :, None] * b_re - zi[:, :, None] * b_im
    bbar_im = zr[:, :, None] * b_im + zi[:, :, None] * b_re
    same = jnp.asarray(np.eye(S5_GPB, dtype=bool))

    def embed_b(bb):
        bb = bb.reshape(nblk, S5_GPB, 1, p_n, S5_GROUP).transpose(0, 1, 4, 2, 3)
        m = jnp.where(same[None, :, None, :, None], bb, 0.0)
        return m.reshape(nblk, S5_GPB * S5_GROUP, S5_GPB * p_n)

    bm = jnp.concatenate([embed_b(bbar_re), embed_b(bbar_im)], axis=2).astype(BF16)

    def embed_c(cc):
        cc = cc.reshape(nblk, S5_GPB, S5_GROUP, 1, p_n).transpose(0, 1, 4, 3, 2)
        m = jnp.where(same[None, :, None, :, None], cc, 0.0)
        return m.reshape(nblk, S5_GPB * p_n, S5_GPB * S5_GROUP)

    cm = jnp.concatenate([embed_c(c_re), embed_c(-c_im)], axis=1).astype(BF16)

    def tile8(v):
        v = v.reshape(nblk, 1, S5_GPB * p_n)
        return jnp.broadcast_to(v, (nblk, 8, S5_GPB * p_n))

    k = jnp.arange(1, seg + 1, dtype=F32)[:, None, None]
    mag_k = jnp.exp(k * (a_re * dt))
    ang_k = k * ang

    def table(t):
        return t.reshape(seg, nblk, S5_GPB * p_n).transpose(1, 0, 2)

    pw = jnp.concatenate([table(mag_k * jnp.cos(ang_k)), table(mag_k * jnp.sin(ang_k))], axis=2)
    return bm, cm, tile8(abar_re), tile8(abar_im), pw


def _glu_kernel(y_ref, w_ref, gate_ref, o_ref, *, width):
    zg = _dot(y_ref[...], w_ref[...])
    o = zg[:, :width] * _sigmoid(zg[:, width:]) * _silu(gate_ref[...])
    o_ref[...] = o.astype(o_ref.dtype)


def _glu(y, w_glu, layer, z, g_off, tm=512):
    m, width = y.shape
    go = g_off // width
    return pl.pallas_call(
        functools.partial(_glu_kernel, width=width),
        grid=(m // tm,),
        in_specs=[pl.BlockSpec((tm, width), lambda i: (i, 0)),
                  pl.BlockSpec((None, width, 2 * width), lambda i: (layer, 0, 0)),
                  pl.BlockSpec((tm, width), lambda i: (i, go))],
        out_specs=pl.BlockSpec((tm, width), lambda i: (i, 0)),
        out_shape=jax.ShapeDtypeStruct((m, width), BF16),
        compiler_params=_cparams(("parallel",)),
        name="s5_glu",
    )(y, w_glu, z)


def _merge_kernel(oa_ref, ob_ref, oc_ref, ga_ref, gb_ref, gc_ref, bg_ref, wa_ref, wb_ref, wc_ref,
                  wo_ref, x_ref, ng_ref, xo_ref, ho_ref, *, d):
    def gate(g_ref, i):
        return _sigmoid(g_ref[...] + bg_ref[:, i * d:(i + 1) * d])

    merged = gate(ga_ref, 0) * _dot(oa_ref[...], wa_ref[...])
    merged = merged + gate(gb_ref, 1) * _dot(ob_ref[...], wb_ref[...])
    merged = merged + gate(gc_ref, 2) * _dot(oc_ref[...], wc_ref[...])
    xn = x_ref[...] + _dot(merged.astype(BF16), wo_ref[...])
    xo_ref[...] = xn
    ms = jnp.mean(xn * xn, axis=-1, keepdims=True)
    ho_ref[...] = (xn * lax.rsqrt(ms + EPS) * ng_ref[...]).astype(ho_ref.dtype)


def _merge(oa, ob, oc, z, mg_off, b_gate, wa, wb, wc, wo, layer, x, next_g, h_dtype, tm=256):
    m, d = x.shape
    w = oa.shape[1]
    assert mg_off % d == 0
    mo = mg_off // d
    row = lambda width: pl.BlockSpec((tm, width), lambda i: (i, 0))
    gate = lambda k: pl.BlockSpec((tm, d), lambda i: (i, mo + k))
    resident = lambda shape: pl.BlockSpec(shape, lambda i: (0, 0), pipeline_mode=pl.Buffered(1))
    weight = lambda rows: pl.BlockSpec((None, rows, d), lambda i: (layer, 0, 0),
                                       pipeline_mode=pl.Buffered(1))
    return pl.pallas_call(
        functools.partial(_merge_kernel, d=d),
        grid=(m // tm,),
        in_specs=[row(w), row(w), row(w), gate(0), gate(1), gate(2),
                  resident((1, N_BRANCH * d)),
                  weight(w), weight(w), weight(w), weight(d),
                  row(d), resident((1, d))],
        out_specs=[row(d), row(d)],
        out_shape=[jax.ShapeDtypeStruct((m, d), F32), jax.ShapeDtypeStruct((m, d), h_dtype)],
        compiler_params=_cparams(("parallel",)),
        name="merge_out",
    )(oa, ob, oc, z, z, z, b_gate.reshape(1, -1), wa, wb, wc, wo, x, next_g.reshape(1, -1))


def kernel(x, norm_g, w_in, b_gate, fox_bf, hg_lb, hg_norm_g, s5_a_re, s5_a_im, s5_log_dt,
           s5_b_re, s5_b_im, s5_c_re, s5_c_im, s5_d, s5_w_glu, w_br_a, w_br_b, w_br_c, w_out,
           final_g):
    bsz, s_len, d = x.shape
    depth = w_in.shape[0]
    hg_w = hg_lb.shape[1]
    s5_w = s5_d.shape[1]
    n_fox = fox_bf.shape[1]
    fox_w = n_fox * HEAD_DIM
    n_hg = hg_w // HEAD_DIM
    sizes = (hg_w, hg_w, hg_w, hg_w, s5_w, s5_w, fox_w, fox_w, fox_w, n_fox, fox_w, N_BRANCH * d)
    offs = np.concatenate([[0], np.cumsum(sizes)])
    (o_hq, o_hf, o_hi, o_hg, o_su, o_sg, o_fq, o_fk, o_fv, o_ff, o_fg, o_mg, o_end) = (int(v) for v in offs)
    w_tail = jnp.concatenate([w_in[:, :, o_mg:], w_in[:, :, o_fg:o_mg]], axis=2).astype(BF16)
    o_mg2, o_fg2 = 0, o_end - o_mg
    w_ff = jnp.pad(w_in[:, :, o_ff:o_fg], ((0, 0), (0, 0), (0, LANES - n_fox))).astype(BF16)
    b_ff = jnp.pad(fox_bf, ((0, 0), (0, LANES - n_fox)))
    w_glu, w_a, w_b, w_c, w_o = (w.astype(BF16) for w in (s5_w_glu, w_br_a, w_br_b, w_br_c, w_out))
    s5_rows = 256
    outs = []
    for b in range(bsz):
        xb = x[b]
        h = _rmsnorm(xb, norm_g[0], BF16)
        for l in range(depth):
            z = _inproj(h, w_in, l, 0, o_fq, F32)
            zqkv = _inproj(h, w_in, l, o_fq, 3 * fox_w, BF16,
                           first_tile_scale=HEAD_DIM ** -0.5 * LOG2E, tn=fox_w)
            zg = _inproj(h, w_tail, l, 0, w_tail.shape[2], F32)

            o_a = _hgrn(z, hg_lb, hg_norm_g[l], l, o_hq, o_hf, o_hi, o_hg, n_hg)

            s5p = _s5_params(s5_a_re[l], s5_a_im[l], s5_log_dt[l], s5_b_re[l], s5_b_im[l],
                             s5_c_re[l], s5_c_im[l], s5_rows // 8)
            y_b = _s5(z, o_su, *s5p, s5_d[l], rows=s5_rows)
            o_b = _glu(y_b, w_glu, l, z, o_sg)

            c = _fox_cumlog(h, w_ff[l], b_ff[l:l + 1])
            ct = c[:, :n_fox].T
            o_c = _fox_attn(zqkv, zg, ct[:, :, None], ct[:, None, :], o_fg2, n_fox)

            last = l == depth - 1
            next_g = final_g if last else norm_g[l + 1]
            xb, h = _merge(o_a, o_b, o_c, zg, o_mg2, b_gate[l], w_a, w_b, w_c, w_o, l,
                           xb, next_g, F32 if last else BF16)
        outs.append(h)
    return outs[0][None] if bsz == 1 else jnp.stack(outs, axis=0)
```

```python
import functools
import math

import numpy as np
import jax
import jax.numpy as jnp
from jax import lax
from jax.experimental import pallas as pl
from jax.experimental.pallas import tpu as pltpu

F32 = jnp.float32
BF16 = jnp.bfloat16

EPS = 1e-6
LANES = 128
VMEM_LIMIT = 56 * 1024 * 1024

HEAD_DIM = 128
S5_GROUP = 16
S5_STATE = 64
S5_GPB = 8
N_BRANCH = 3

HG_SUB = 16


def _cparams(sem, vmem=VMEM_LIMIT):
    return pltpu.CompilerParams(dimension_semantics=sem, vmem_limit_bytes=vmem)


def _dot(a, b):
    return jnp.dot(a, b, preferred_element_type=F32)


def _dot_nt(a, b):
    return lax.dot_general(a, b, (((1,), (1,)), ((), ())), preferred_element_type=F32)


def _dot_tn(a, b):
    return lax.dot_general(a, b, (((0,), (0,)), ((), ())), preferred_element_type=F32)


def _split3(x):
    hi = x.astype(BF16)
    r1 = x - hi.astype(F32)
    mid = r1.astype(BF16)
    lo = (r1 - mid.astype(F32)).astype(BF16)
    return hi, mid, lo


def _dot01(m, x):
    hi, mid, lo = _split3(x)
    return _dot(m, hi) + _dot(m, mid) + _dot(m, lo)


def _log_sigmoid(z):
    return jnp.minimum(z, 0.0) - jnp.log1p(jnp.exp(-jnp.abs(z)))


def _sigmoid(z):
    return 1.0 / (1.0 + jnp.exp(-z))


def _silu(z):
    return z * _sigmoid(z)


def _rmsnorm_kernel(x_ref, g_ref, o_ref):
    x = x_ref[...]
    ms = jnp.mean(x * x, axis=-1, keepdims=True)
    o_ref[...] = (x * lax.rsqrt(ms + EPS) * g_ref[...]).astype(o_ref.dtype)


def _rmsnorm(x, g, out_dtype, tm=512):
    m, d = x.shape
    return pl.pallas_call(
        _rmsnorm_kernel,
        grid=(m // tm,),
        in_specs=[pl.BlockSpec((tm, d), lambda i: (i, 0)),
                  pl.BlockSpec((1, d), lambda i: (0, 0))],
        out_specs=pl.BlockSpec((tm, d), lambda i: (i, 0)),
        out_shape=jax.ShapeDtypeStruct((m, d), out_dtype),
        compiler_params=_cparams(("parallel",)),
        name="rmsnorm",
    )(x, g.reshape(1, d))


def _inproj_kernel(starts_ref, h_ref, w_ref, o_ref, wb_ref, *, first_tile_scale):
    del starts_ref
    @pl.when(pl.program_id(1) == 0)
    def _():
        wb_ref[...] = w_ref[0].astype(BF16)

    acc = _dot_nt(h_ref[...], wb_ref[...])
    if first_tile_scale is not None:
        acc = acc * jnp.where(pl.program_id(0) == 0, first_tile_scale, 1.0)
    o_ref[...] = acc.astype(o_ref.dtype)


def _inproj(h, wt, layer, row_starts, out_dtype, first_tile_scale=None, tm=512, tn=1024):
    m, k = h.shape
    n_tiles = len(row_starts)
    assert all(r % 8 == 0 for r in row_starts)
    starts = jnp.asarray(np.asarray(row_starts, np.int32) // 8)
    grid_spec = pltpu.PrefetchScalarGridSpec(
        num_scalar_prefetch=1,
        grid=(n_tiles, m // tm),
        in_specs=[pl.BlockSpec((tm, k), lambda j, i, st: (i, 0)),
                  pl.BlockSpec((pl.Element(1), pl.Element(tn), pl.Element(k)),
                               lambda j, i, st: (layer, st[j] * 8, 0))],
        out_specs=pl.BlockSpec((tm, tn), lambda j, i, st: (i, j)),
        scratch_shapes=[pltpu.VMEM((tn, k), BF16)])
    return pl.pallas_call(
        functools.partial(_inproj_kernel, first_tile_scale=first_tile_scale),
        grid_spec=grid_spec,
        out_shape=jax.ShapeDtypeStruct((m, n_tiles * tn), out_dtype),
        compiler_params=_cparams(("parallel", "arbitrary")),
        name="inproj",
    )(starts, h, wt)


def _foxc_kernel(h_ref, w_ref, b_ref, triu_ref, c_ref, carry_ref):
    @pl.when(pl.program_id(0) == 0)
    def _():
        carry_ref[...] = jnp.zeros_like(carry_ref)

    logits = _dot_nt(w_ref[...].astype(BF16), h_ref[...]) + b_ref[...]
    hi, mid, lo = _split3(_log_sigmoid(logits))
    tri = triu_ref[...]
    cum = _dot(hi, tri) + _dot(mid, tri) + _dot(lo, tri) + carry_ref[:, 0:1]
    c_ref[...] = cum
    tm = cum.shape[1]
    carry_ref[...] = jnp.broadcast_to(cum[:, tm - 1:tm], carry_ref.shape)


def _fox_cumlog(h, wt, layer, row0, b_ff, tm=512):
    m, k = h.shape
    n_heads = b_ff.shape[0]
    assert row0 % n_heads == 0
    triu = jnp.asarray(np.triu(np.ones((tm, tm), np.float32)), BF16)
    return pl.pallas_call(
        _foxc_kernel,
        grid=(m // tm,),
        in_specs=[pl.BlockSpec((tm, k), lambda i: (i, 0)),
                  pl.BlockSpec((None, n_heads, k), lambda i: (layer, row0 // n_heads, 0)),
                  pl.BlockSpec((n_heads, 1), lambda i: (0, 0)),
                  pl.BlockSpec((tm, tm), lambda i: (0, 0))],
        out_specs=pl.BlockSpec((n_heads, tm), lambda i: (0, i)),
        out_shape=jax.ShapeDtypeStruct((n_heads, m), F32),
        scratch_shapes=[pltpu.VMEM((n_heads, LANES), F32)],
        compiler_params=_cparams(("arbitrary",)),
        name="fox_cumlog",
    )(h, wt, b_ff.reshape(n_heads, 1), triu)


NEG_BIG = -1e30


LOG2E = 1.0 / math.log(2.0)
ATT_ROWS = 64
ATT_UNROLL = 2


def _fox_attn_kernel(tab_ref, q_ref, k_ref, v_ref, ccol_ref, crow_ref, gate_ref, o_ref,
                     va_ref, sa_ref, sb_ref, pa_ref, pb_ref, ala_ref, alb_ref, acc_ref, m_ref,
                     *, tq, n_off, n_diag):
    @pl.when(pl.program_id(0) == 0)
    def _():
        va_ref[:, HEAD_DIM:2 * HEAD_DIM] = jnp.ones((va_ref.shape[0], HEAD_DIM), BF16)

    va_ref[:, 0:HEAD_DIM] = v_ref[...]
    m_ref[...] = jnp.full(m_ref.shape, NEG_BIG, F32)
    acc_ref[...] = jnp.zeros(acc_ref.shape, F32)

    def tile(n):
        q0 = pl.multiple_of(tab_ref[0, n] * tq, tq)
        k0 = pl.multiple_of(tab_ref[1, n] * tq, tq)
        return q0, k0

    def logits(n, s_ref):
        q0, k0 = tile(n)
        s_ref[...] = _dot_nt(q_ref[pl.ds(q0, tq), :], k_ref[pl.ds(k0, tq), :])

    def softmax(n, s_ref, p_ref, al_ref, masked):
        q0, k0 = tile(n)
        crow = crow_ref[0, :, pl.ds(k0, tq)] * LOG2E
        for r in range(tq // ATT_ROWS):
            rs = slice(r * ATT_ROWS, (r + 1) * ATT_ROWS)
            qs = pl.ds(q0 + r * ATT_ROWS, ATT_ROWS)
            s = s_ref[rs, :] - crow
            if masked:
                row = lax.broadcasted_iota(jnp.int32, (ATT_ROWS, tq), 0) + r * ATT_ROWS
                col = lax.broadcasted_iota(jnp.int32, (ATT_ROWS, tq), 1)
                s = jnp.where(col <= row, s, NEG_BIG)
            ct = ccol_ref[0, qs, :] * LOG2E
            m_old = m_ref[qs, :]
            m_new = jnp.maximum(m_old, jnp.max(s, axis=1, keepdims=True) + ct)
            p_ref[rs, :] = jnp.exp2(s - (m_new - ct)).astype(BF16)
            m_ref[qs, :] = m_new
            al_ref[rs, :] = jnp.exp2(m_old - m_new)

    def accumulate(n, p_ref, al_ref):
        q0, k0 = tile(n)
        qs = pl.ds(q0, tq)
        acc_ref[qs, :] = al_ref[...] * acc_ref[qs, :] + _dot(p_ref[...], va_ref[pl.ds(k0, tq), :])

    def run(first, count, masked):
        if count == 0:
            return
        last = first + count - 1
        nxt = lambda n: jnp.minimum(n, last)
        s_buf = (sa_ref, sb_ref)
        p_buf = ((pa_ref, ala_ref), (pb_ref, alb_ref))
        logits(first, s_buf[0])
        softmax(first, s_buf[0], *p_buf[0], masked)
        logits(nxt(first + 1), s_buf[1])
        n_loop = (count - 1) // ATT_UNROLL

        def body(j, carry):
            n = first + ATT_UNROLL * j
            for u in range(ATT_UNROLL):
                accumulate(n + u, *p_buf[u % 2])
                softmax(n + u + 1, s_buf[(u + 1) % 2], *p_buf[(u + 1) % 2], masked)
                logits(nxt(n + u + 2), s_buf[u % 2])
            return carry

        lax.fori_loop(0, n_loop, body, 0)
        n = first + ATT_UNROLL * n_loop
        rest = count - 1 - ATT_UNROLL * n_loop
        for u in range(rest + 1):
            accumulate(n + u, *p_buf[u % 2])
            if u + 1 <= rest:
                softmax(n + u + 1, s_buf[(u + 1) % 2], *p_buf[(u + 1) % 2], masked)
            if u + 2 <= rest:
                logits(n + u + 2, s_buf[u % 2])

    run(0, n_off, False)
    run(n_off, n_diag, True)

    acc = acc_ref[...]
    out = acc[:, 0:HEAD_DIM] / acc[:, HEAD_DIM:2 * HEAD_DIM]
    o_ref[...] = (out * _silu(gate_ref[...])).astype(o_ref.dtype)


def _fox_attn(zqkv, z, ccol, crow, g_off, n_heads, tq=512):
    s_len = zqkv.shape[0]
    nq = s_len // tq
    go = g_off // HEAD_DIM
    off = [(qi, kb) for kb in range(nq) for qi in range(kb + 1, nq)]
    diag = [(i, i) for i in range(nq)]
    tab = jnp.asarray(np.array(off + diag, np.int32).T)
    kern = functools.partial(_fox_attn_kernel, tq=tq, n_off=len(off), n_diag=len(diag))
    once = pl.Buffered(1)
    head_col = lambda base: pl.BlockSpec((s_len, HEAD_DIM), lambda h, t: (0, base + h),
                                         pipeline_mode=once)
    grid_spec = pltpu.PrefetchScalarGridSpec(
        num_scalar_prefetch=1,
        grid=(n_heads,),
        in_specs=[head_col(0), head_col(n_heads), head_col(2 * n_heads),
                  pl.BlockSpec((1, s_len, 1), lambda h, t: (h, 0, 0), pipeline_mode=once),
                  pl.BlockSpec((1, 1, s_len), lambda h, t: (h, 0, 0)),
                  head_col(go)],
        out_specs=pl.BlockSpec((s_len, HEAD_DIM), lambda h, t: (0, h)),
        scratch_shapes=[pltpu.VMEM((s_len, 2 * HEAD_DIM), BF16),
                        pltpu.VMEM((tq, tq), F32),
                        pltpu.VMEM((tq, tq), F32),
                        pltpu.VMEM((tq, tq), BF16),
                        pltpu.VMEM((tq, tq), BF16),
                        pltpu.VMEM((tq, 1), F32),
                        pltpu.VMEM((tq, 1), F32),
                        pltpu.VMEM((s_len, 2 * HEAD_DIM), F32),
                        pltpu.VMEM((s_len, 1), F32)])
    return pl.pallas_call(
        kern,
        grid_spec=grid_spec,
        out_shape=jax.ShapeDtypeStruct((s_len, n_heads * HEAD_DIM), BF16),
        compiler_params=_cparams(("arbitrary",)),
        name="fox_attn",
    )(tab, zqkv, zqkv, zqkv, ccol, crow, z)


HG_PROWS = 8 * HG_SUB + 8 * (HG_SUB // 2)


def _hgrn_kernel(q_ref, f_ref, i_ref, gate_ref, lb_ref, ng_ref, tb_ref, ones_ref,
                 mask_ref, o_ref, st_ref, qt_ref, kt_ref, w_ref, cum_ref, dd_ref, p_ref, sc_ref,
                 acc_ref, *, layer, rows):
    @pl.when(pl.program_id(1) == 0)
    def _():
        st_ref[...] = jnp.zeros_like(st_ref)

    z = f_ref[...]
    ls = _log_sigmoid(z)
    if layer == 0:
        g = ls
        logk = ls - z
    else:
        lbp = lb_ref[...]
        e = jnp.exp(lbp - jnp.max(lbp, axis=0, keepdims=True))
        p = e / jnp.sum(e, axis=0, keepdims=True)
        lb = jnp.sum(p[1:layer + 1, :], axis=0, keepdims=True)
        a = jnp.log(lb)
        l1m = jnp.log1p(-lb)
        b = l1m + ls
        g = jnp.maximum(a, b) + jnp.log1p(jnp.exp(-jnp.abs(a - b)))
        logk = l1m + (ls - z)

    cums, tots = [], []
    for r0 in range(0, rows, LANES):
        ct = _dot01(tb_ref[...], g[r0:r0 + LANES, :])
        cums.append(ct[0:LANES, :])
        tots.append(ct[LANES:2 * LANES, :])
    cum = jnp.concatenate(cums, axis=0) * LOG2E
    tot = jnp.concatenate(tots, axis=0) * LOG2E
    w = cum - logk * LOG2E
    qt_ref[...] = (q_ref[...] * jnp.exp2(cum)).astype(BF16)
    kt_ref[...] = jnp.exp2(tot - w).astype(BF16)
    w_ref[...] = w
    cum_ref[...] = cum
    dd_ref[...] = jnp.exp2(tot)

    half = HG_SUB // 2
    n_groups = rows // HG_SUB

    def bcast_row(ref, r):
        return jnp.broadcast_to(ref[r:r + 1, :], (half, HEAD_DIM))

    for g_i in range(n_groups):
        r0 = g_i * HG_SUB
        p0 = g_i * HG_PROWS
        c_lo, c_hi = cum_ref[r0:r0 + half, :], cum_ref[r0 + half:r0 + HG_SUB, :]
        q_lo, q_hi = q_ref[r0:r0 + half, :], q_ref[r0 + half:r0 + HG_SUB, :]
        for s in range(half):
            w_s = bcast_row(w_ref, r0 + s)
            p_lo = q_lo * jnp.exp2(c_lo - w_s + mask_ref[s * half:(s + 1) * half, :])
            p_hi = q_hi * jnp.exp2(c_hi - w_s)
            p_ref[p0 + s * HG_SUB:p0 + (s + 1) * HG_SUB, :] = (
                jnp.concatenate([p_lo, p_hi], axis=0).astype(BF16))
        for s in range(0, half, 2):
            pa = q_hi * jnp.exp2(c_hi - bcast_row(w_ref, r0 + half + s)
                                + mask_ref[s * half:(s + 1) * half, :])
            pb = q_hi * jnp.exp2(c_hi - bcast_row(w_ref, r0 + half + s + 1)
                                + mask_ref[(s + 1) * half:(s + 2) * half, :])
            base = p0 + half * HG_SUB + s * half
            p_ref[base:base + HG_SUB, :] = jnp.concatenate([pa, pb], axis=0).astype(BF16)

    sc_ref[...] = _dot(p_ref[...], ones_ref[...])

    upds = [_dot_tn(i_ref[g_i * HG_SUB:(g_i + 1) * HG_SUB, :].astype(BF16),
                    kt_ref[g_i * HG_SUB:(g_i + 1) * HG_SUB, :]) for g_i in range(n_groups)]
    st = st_ref[...]
    for g_i in range(n_groups):
        r0 = g_i * HG_SUB
        p0 = g_i * HG_PROWS
        o_lo = jnp.zeros((half, HEAD_DIM), F32)
        o_hi = jnp.zeros((half, HEAD_DIM), F32)
        for s in range(half):
            v_s = bcast_row(i_ref, r0 + s)
            o_lo = o_lo + sc_ref[p0 + s * HG_SUB:p0 + s * HG_SUB + half, :] * v_s
            o_hi = o_hi + sc_ref[p0 + s * HG_SUB + half:p0 + (s + 1) * HG_SUB, :] * v_s
        for s in range(half):
            base = p0 + half * HG_SUB + s * half
            o_hi = o_hi + sc_ref[base:base + half, :] * bcast_row(i_ref, r0 + half + s)
        o_inter = _dot_nt(qt_ref[r0:r0 + HG_SUB, :], st.astype(BF16))
        acc_ref[r0:r0 + HG_SUB, :] = o_inter + jnp.concatenate([o_lo, o_hi], axis=0)
        st = st * dd_ref[r0:r0 + 1, :] + upds[g_i]
    st_ref[...] = st

    o = acc_ref[...]
    ms = jnp.mean(o * o, axis=-1, keepdims=True)
    o = o * lax.rsqrt(ms + EPS) * ng_ref[...]
    o_ref[...] = (o * _silu(gate_ref[...])).astype(o_ref.dtype)


def _hgrn(z, hg_lb, norm_g, layer, q_off, f_off, i_off, g_off, n_heads, rows=512):
    s_len = z.shape[0]
    depth = hg_lb.shape[0]
    qo, fo, io, go = (o // HEAD_DIM for o in (q_off, f_off, i_off, g_off))
    r = np.arange(LANES)
    same = (r[:, None] // HG_SUB) == (r[None, :] // HG_SUB)
    tb = jnp.asarray(np.concatenate([same & (r[None, :] <= r[:, None]), same]).astype(np.float32), BF16)
    ones = jnp.ones((HEAD_DIM, HEAD_DIM), BF16)
    half = HG_SUB // 2
    t_idx = np.arange(half)
    mask_np = np.where(t_idx[None, :, None] >= t_idx[:, None, None], 0.0, NEG_BIG)
    mask = jnp.asarray(np.broadcast_to(mask_np, (half, half, HEAD_DIM)).reshape(half * half, HEAD_DIM), F32)
    n_prows = (rows // HG_SUB) * HG_PROWS
    kern = functools.partial(_hgrn_kernel, layer=layer, rows=rows)
    blk_spec = lambda off: pl.BlockSpec((rows, HEAD_DIM), lambda h, i: (i, off + h))
    const = lambda shape: pl.BlockSpec(shape, lambda h, i: (0, 0))
    return pl.pallas_call(
        kern,
        grid=(n_heads, s_len // rows),
        in_specs=[blk_spec(qo), blk_spec(fo), blk_spec(io), blk_spec(go),
                  pl.BlockSpec((depth, HEAD_DIM), lambda h, i: (0, h)),
                  pl.BlockSpec((1, HEAD_DIM), lambda h, i: (0, h)),
                  const((2 * LANES, LANES)), const((HEAD_DIM, HEAD_DIM)),
                  const((half * half, HEAD_DIM))],
        out_specs=pl.BlockSpec((rows, HEAD_DIM), lambda h, i: (i, h)),
        out_shape=jax.ShapeDtypeStruct((s_len, n_heads * HEAD_DIM), BF16),
        scratch_shapes=[pltpu.VMEM((HEAD_DIM, HEAD_DIM), F32),
                        pltpu.VMEM((rows, HEAD_DIM), BF16),
                        pltpu.VMEM((rows, HEAD_DIM), BF16),
                        pltpu.VMEM((rows, HEAD_DIM), F32),
                        pltpu.VMEM((rows, HEAD_DIM), F32),
                        pltpu.VMEM((rows, HEAD_DIM), F32),
                        pltpu.VMEM((n_prows, HEAD_DIM), BF16),
                        pltpu.VMEM((n_prows, HEAD_DIM), F32),
                        pltpu.VMEM((rows, HEAD_DIM), F32)],
        compiler_params=_cparams(("parallel", "arbitrary")),
        name="hgrn2",
    )(z, z, z, z, hg_lb, norm_g.reshape(1, -1), tb, ones, mask)


def _gelu_tanh(x):
    c = math.sqrt(2.0 / math.pi)
    return 0.5 * x * (1.0 + jnp.tanh(c * (x + 0.044715 * (x * x * x))))


def _s5_kernel(u_ref, perm_ref, permt_ref, bm_ref, cm_ref, are_ref, aim_ref, pw_ref,
               d_ref, o_ref, state_ref, x_ref, yp_ref, *, rows, half):
    @pl.when(pl.program_id(0) == 0)
    def _():
        state_ref[...] = jnp.zeros_like(state_ref)

    nblk = bm_ref.shape[0]
    nt = rows // 8
    u = u_ref[...]
    up = _dot(perm_ref[...], u.astype(BF16)).astype(BF16)
    sub = lax.broadcasted_iota(jnp.int32, (8, half), 0)

    re, im = slice(0, half), slice(half, 2 * half)
    for b in range(nblk):
        x_ref[b] = _dot(up[:, b * LANES:(b + 1) * LANES], bm_ref[b])
        are = are_ref[b]
        aim = aim_ref[b]
        xr = jnp.zeros((8, half), F32)
        xi = jnp.zeros((8, half), F32)
        for t in range(nt):
            rs = slice(t * 8, (t + 1) * 8)
            xr, xi = (are * xr - aim * xi + x_ref[b, rs, re],
                      are * xi + aim * xr + x_ref[b, rs, im])
            x_ref[b, rs, re] = xr
            x_ref[b, rs, im] = xi

        er, ei = xr, xi
        alre = pw_ref[b, nt - 1:nt, re]
        alim = pw_ref[b, nt - 1:nt, im]
        cr = state_ref[b, 0:1, re]
        ci = state_ref[b, 0:1, im]
        ctr = jnp.zeros((8, half), F32)
        cti = jnp.zeros((8, half), F32)
        for s in range(8):
            ctr = jnp.where(sub == s, cr, ctr)
            cti = jnp.where(sub == s, ci, cti)
            cr, ci = (alre * cr - alim * ci + er[s:s + 1, :],
                      alre * ci + alim * cr + ei[s:s + 1, :])
        state_ref[b, :, re] = jnp.broadcast_to(cr, (8, half))
        state_ref[b, :, im] = jnp.broadcast_to(ci, (8, half))

        for t in range(nt):
            rs = slice(t * 8, (t + 1) * 8)
            pr = pw_ref[b, t:t + 1, re]
            pi = pw_ref[b, t:t + 1, im]
            x_ref[b, rs, re] = x_ref[b, rs, re] + (pr * ctr - pi * cti)
            x_ref[b, rs, im] = x_ref[b, rs, im] + (pr * cti + pi * ctr)
        yp_ref[:, b * LANES:(b + 1) * LANES] = _dot(x_ref[b].astype(BF16), cm_ref[b])

    y = _dot01(permt_ref[...], yp_ref[...]) + d_ref[...] * u
    o_ref[...] = _gelu_tanh(y).astype(o_ref.dtype)


def _s5(z, u_off, bm, cm, are, aim, pw, d_skip, rows=256):
    s_len = z.shape[0]
    width = d_skip.shape[0]
    nblk, _, two_half = bm.shape
    half = two_half // 2
    seg = rows // 8
    rho = np.arange(rows)
    t_of = (rho % 8) * seg + rho // 8
    perm_np = np.zeros((rows, rows), np.float32)
    perm_np[rho, t_of] = 1.0
    perm = jnp.asarray(perm_np, BF16)
    permt = jnp.asarray(perm_np.T, BF16)
    uo = u_off // width
    kern = functools.partial(_s5_kernel, rows=rows, half=half)
    c2 = lambda shape: pl.BlockSpec(shape, lambda i: (0, 0))
    c3 = lambda shape: pl.BlockSpec(shape, lambda i: (0, 0, 0))
    return pl.pallas_call(
        kern,
        grid=(s_len // rows,),
        in_specs=[pl.BlockSpec((rows, width), lambda i: (i, uo)),
                  c2((rows, rows)), c2((rows, rows)),
                  c3(bm.shape), c3(cm.shape),
                  c3(are.shape), c3(aim.shape), c3(pw.shape),
                  c2((1, width))],
        out_specs=pl.BlockSpec((rows, width), lambda i: (i, 0)),
        out_shape=jax.ShapeDtypeStruct((s_len, width), BF16),
        scratch_shapes=[pltpu.VMEM((nblk, 8, two_half), F32),
                        pltpu.VMEM((nblk, rows, two_half), F32),
                        pltpu.VMEM((rows, width), F32)],
        compiler_params=_cparams(("arbitrary",)),
        name="s5",
    )(z, perm, permt, bm, cm, are, aim, pw, d_skip.reshape(1, width))


def _s5_params(a_re, a_im, log_dt, b_re, b_im, c_re, c_im, seg):
    g_n, p_n = a_re.shape
    nblk = g_n // S5_GPB
    dt = jnp.exp(log_dt)[:, None]
    mag = jnp.exp(a_re * dt)
    ang = a_im * dt
    abar_re = mag * jnp.cos(ang)
    abar_im = mag * jnp.sin(ang)
    nr = abar_re - 1.0
    den = a_re * a_re + a_im * a_im
    zr = (nr * a_re + abar_im * a_im) / den
    zi = (abar_im * a_re - nr * a_im) / den
    bbar_re = zr[:, :, None] * b_re - zi[:, :, None] * b_im
    bbar_im = zr[:, :, None] * b_im + zi[:, :, None] * b_re
    same = jnp.asarray(np.eye(S5_GPB, dtype=bool))

    def embed_b(bb):
        bb = bb.reshape(nblk, S5_GPB, 1, p_n, S5_GROUP).transpose(0, 1, 4, 2, 3)
        m = jnp.where(same[None, :, None, :, None], bb, 0.0)
        return m.reshape(nblk, S5_GPB * S5_GROUP, S5_GPB * p_n)

    bm = jnp.concatenate([embed_b(bbar_re), embed_b(bbar_im)], axis=2).astype(BF16)

    def embed_c(cc):
        cc = cc.reshape(nblk, S5_GPB, S5_GROUP, 1, p_n).transpose(0, 1, 4, 3, 2)
        m = jnp.where(same[None, :, None, :, None], cc, 0.0)
        return m.reshape(nblk, S5_GPB * p_n, S5_GPB * S5_GROUP)

    cm = jnp.concatenate([embed_c(c_re), embed_c(-c_im)], axis=1).astype(BF16)

    def tile8(v):
        v = v.reshape(nblk, 1, S5_GPB * p_n)
        return jnp.broadcast_to(v, (nblk, 8, S5_GPB * p_n))

    k = jnp.arange(1, seg + 1, dtype=F32)[:, None, None]
    mag_k = jnp.exp(k * (a_re * dt))
    ang_k = k * ang

    def table(t):
        return t.reshape(seg, nblk, S5_GPB * p_n).transpose(1, 0, 2)

    pw = jnp.concatenate([table(mag_k * jnp.cos(ang_k)), table(mag_k * jnp.sin(ang_k))], axis=2)
    return bm, cm, tile8(abar_re), tile8(abar_im), pw


def _glu_kernel(y_ref, w_ref, gate_ref, o_ref, *, width):
    zg = _dot(y_ref[...], w_ref[...])
    o = zg[:, :width] * _sigmoid(zg[:, width:]) * _silu(gate_ref[...])
    o_ref[...] = o.astype(o_ref.dtype)


def _glu(y, w_glu, layer, z, g_off, tm=512):
    m, width = y.shape
    go = g_off // width
    return pl.pallas_call(
        functools.partial(_glu_kernel, width=width),
        grid=(m // tm,),
        in_specs=[pl.BlockSpec((tm, width), lambda i: (i, 0)),
                  pl.BlockSpec((None, width, 2 * width), lambda i: (layer, 0, 0)),
                  pl.BlockSpec((tm, width), lambda i: (i, go))],
        out_specs=pl.BlockSpec((tm, width), lambda i: (i, 0)),
        out_shape=jax.ShapeDtypeStruct((m, width), BF16),
        compiler_params=_cparams(("parallel",)),
        name="s5_glu",
    )(y, w_glu, z)


def _merge_kernel(oa_ref, ob_ref, oc_ref, ga_ref, gb_ref, gc_ref, bg_ref, wa_ref, wb_ref, wc_ref,
                  wo_ref, x_ref, ng_ref, xo_ref, ho_ref, *, d):
    def gate(g_ref, i):
        return _sigmoid(g_ref[...] + bg_ref[:, i * d:(i + 1) * d])

    merged = gate(ga_ref, 0) * _dot(oa_ref[...], wa_ref[...])
    merged = merged + gate(gb_ref, 1) * _dot(ob_ref[...], wb_ref[...])
    merged = merged + gate(gc_ref, 2) * _dot(oc_ref[...], wc_ref[...])
    xn = x_ref[...] + _dot(merged.astype(BF16), wo_ref[...])
    xo_ref[...] = xn
    ms = jnp.mean(xn * xn, axis=-1, keepdims=True)
    ho_ref[...] = (xn * lax.rsqrt(ms + EPS) * ng_ref[...]).astype(ho_ref.dtype)


def _merge(oa, ob, oc, z, mg_off, b_gate, wa, wb, wc, wo, layer, x, next_g, h_dtype, tm=256):
    m, d = x.shape
    w = oa.shape[1]
    assert mg_off % d == 0
    mo = mg_off // d
    row = lambda width: pl.BlockSpec((tm, width), lambda i: (i, 0))
    gate = lambda k: pl.BlockSpec((tm, d), lambda i: (i, mo + k))
    resident = lambda shape: pl.BlockSpec(shape, lambda i: (0, 0), pipeline_mode=pl.Buffered(1))
    weight = lambda rows: pl.BlockSpec((None, rows, d), lambda i: (layer, 0, 0),
                                       pipeline_mode=pl.Buffered(1))
    return pl.pallas_call(
        functools.partial(_merge_kernel, d=d),
        grid=(m // tm,),
        in_specs=[row(w), row(w), row(w), gate(0), gate(1), gate(2),
                  resident((1, N_BRANCH * d)),
                  weight(w), weight(w), weight(w), weight(d),
                  row(d), resident((1, d))],
        out_specs=[row(d), row(d)],
        out_shape=[jax.ShapeDtypeStruct((m, d), F32), jax.ShapeDtypeStruct((m, d), h_dtype)],
        compiler_params=_cparams(("parallel",)),
        name="merge_out",
    )(oa, ob, oc, z, z, z, b_gate.reshape(1, -1), wa, wb, wc, wo, x, next_g.reshape(1, -1))


def kernel(x, norm_g, w_in, b_gate, fox_bf, hg_lb, hg_norm_g, s5_a_re, s5_a_im, s5_log_dt,
           s5_b_re, s5_b_im, s5_c_re, s5_c_im, s5_d, s5_w_glu, w_br_a, w_br_b, w_br_c, w_out,
           final_g):
    bsz, s_len, d = x.shape
    depth = w_in.shape[0]
    hg_w = hg_lb.shape[1]
    s5_w = s5_d.shape[1]
    n_fox = fox_bf.shape[1]
    fox_w = n_fox * HEAD_DIM
    n_hg = hg_w // HEAD_DIM
    sizes = (hg_w, hg_w, hg_w, hg_w, s5_w, s5_w, fox_w, fox_w, fox_w, n_fox, fox_w, N_BRANCH * d)
    offs = np.concatenate([[0], np.cumsum(sizes)])
    (o_hq, o_hf, o_hi, o_hg, o_su, o_sg, o_fq, o_fk, o_fv, o_ff, o_fg, o_mg, o_end) = (int(v) for v in offs)
    wt = jnp.transpose(w_in, (0, 2, 1))
    tn = fox_w
    tiles = lambda start, stop: list(range(start, stop, tn))
    o_mg2, o_fg2 = 0, o_end - o_mg
    w_glu, w_a, w_b, w_c, w_o = (w.astype(BF16) for w in (s5_w_glu, w_br_a, w_br_b, w_br_c, w_out))
    s5_rows = 256
    outs = []
    for b in range(bsz):
        xb = x[b]
        h = _rmsnorm(xb, norm_g[0], BF16)
        for l in range(depth):
            z = _inproj(h, wt, l, tiles(0, o_fq), F32, tn=tn)
            zqkv = _inproj(h, wt, l, tiles(o_fq, o_ff), BF16,
                           first_tile_scale=HEAD_DIM ** -0.5 * LOG2E, tn=tn)
            zg = _inproj(h, wt, l, tiles(o_mg, o_end) + tiles(o_fg, o_mg), F32, tn=tn)

            o_a = _hgrn(z, hg_lb, hg_norm_g[l], l, o_hq, o_hf, o_hi, o_hg, n_hg)

            s5p = _s5_params(s5_a_re[l], s5_a_im[l], s5_log_dt[l], s5_b_re[l], s5_b_im[l],
                             s5_c_re[l], s5_c_im[l], s5_rows // 8)
            y_b = _s5(z, o_su, *s5p, s5_d[l], rows=s5_rows)
            o_b = _glu(y_b, w_glu, l, z, o_sg)

            ct = _fox_cumlog(h, wt, l, o_ff, fox_bf[l])
            o_c = _fox_attn(zqkv, zg, ct[:, :, None], ct[:, None, :], o_fg2, n_fox)

            last = l == depth - 1
            next_g = final_g if last else norm_g[l + 1]
            xb, h = _merge(o_a, o_b, o_c, zg, o_mg2, b_gate[l], w_a, w_b, w_c, w_o, l,
                           xb, next_g, F32 if last else BF16)
        outs.append(h)
    return outs[0][None] if bsz == 1 else jnp.stack(outs, axis=0)
```

```python
import functools
import math

import numpy as np
import jax
import jax.numpy as jnp
from jax import lax
from jax.experimental import pallas as pl
from jax.experimental.pallas import tpu as pltpu

F32 = jnp.float32
BF16 = jnp.bfloat16

EPS = 1e-6
LANES = 128
VMEM_LIMIT = 56 * 1024 * 1024

HEAD_DIM = 128
S5_GROUP = 16
S5_STATE = 64
S5_GPB = 8
N_BRANCH = 3

HG_SUB = 16


def _cparams(sem, vmem=VMEM_LIMIT):
    return pltpu.CompilerParams(dimension_semantics=sem, vmem_limit_bytes=vmem)


def _dot(a, b):
    return jnp.dot(a, b, preferred_element_type=F32)


def _dot_nt(a, b):
    return lax.dot_general(a, b, (((1,), (1,)), ((), ())), preferred_element_type=F32)


def _dot_tn(a, b):
    return lax.dot_general(a, b, (((0,), (0,)), ((), ())), preferred_element_type=F32)


def _split3(x):
    hi = x.astype(BF16)
    r1 = x - hi.astype(F32)
    mid = r1.astype(BF16)
    lo = (r1 - mid.astype(F32)).astype(BF16)
    return hi, mid, lo


def _dot01(m, x):
    hi, mid, lo = _split3(x)
    return _dot(m, hi) + _dot(m, mid) + _dot(m, lo)


def _log_sigmoid(z):
    return jnp.minimum(z, 0.0) - jnp.log1p(jnp.exp(-jnp.abs(z)))


def _sigmoid(z):
    return 1.0 / (1.0 + jnp.exp(-z))


def _silu(z):
    return z * _sigmoid(z)


def _rmsnorm_kernel(x_ref, g_ref, o_ref):
    x = x_ref[...]
    ms = jnp.mean(x * x, axis=-1, keepdims=True)
    o_ref[...] = (x * lax.rsqrt(ms + EPS) * g_ref[...]).astype(o_ref.dtype)


def _rmsnorm(x, g, out_dtype, tm=512):
    m, d = x.shape
    return pl.pallas_call(
        _rmsnorm_kernel,
        grid=(m // tm,),
        in_specs=[pl.BlockSpec((tm, d), lambda i: (i, 0)),
                  pl.BlockSpec((1, d), lambda i: (0, 0))],
        out_specs=pl.BlockSpec((tm, d), lambda i: (i, 0)),
        out_shape=jax.ShapeDtypeStruct((m, d), out_dtype),
        compiler_params=_cparams(("parallel",)),
        name="rmsnorm",
    )(x, g.reshape(1, d))


def _inproj_kernel(starts_ref, h_ref, w_ref, o_ref, wb_ref, *, first_tile_scale):
    del starts_ref
    @pl.when(pl.program_id(1) == 0)
    def _():
        wb_ref[...] = w_ref[0].astype(BF16)

    acc = _dot_nt(h_ref[...], wb_ref[...])
    if first_tile_scale is not None:
        acc = acc * jnp.where(pl.program_id(0) == 0, first_tile_scale, 1.0)
    o_ref[...] = acc.astype(o_ref.dtype)


def _inproj(h, wt, layer, row_starts, out_dtype, first_tile_scale=None, tm=512, tn=1024):
    m, k = h.shape
    n_tiles = len(row_starts)
    assert all(r % 8 == 0 for r in row_starts)
    starts = jnp.asarray(np.asarray(row_starts, np.int32) // 8)
    grid_spec = pltpu.PrefetchScalarGridSpec(
        num_scalar_prefetch=1,
        grid=(n_tiles, m // tm),
        in_specs=[pl.BlockSpec((tm, k), lambda j, i, st: (i, 0)),
                  pl.BlockSpec((pl.Element(1), pl.Element(tn), pl.Element(k)),
                               lambda j, i, st: (layer, st[j] * 8, 0))],
        out_specs=pl.BlockSpec((tm, tn), lambda j, i, st: (i, j)),
        scratch_shapes=[pltpu.VMEM((tn, k), BF16)])
    return pl.pallas_call(
        functools.partial(_inproj_kernel, first_tile_scale=first_tile_scale),
        grid_spec=grid_spec,
        out_shape=jax.ShapeDtypeStruct((m, n_tiles * tn), out_dtype),
        compiler_params=_cparams(("parallel", "arbitrary")),
        name="inproj",
    )(starts, h, wt)


def _foxc_kernel(h_ref, w_ref, b_ref, triu_ref, c_ref, carry_ref):
    @pl.when(pl.program_id(0) == 0)
    def _():
        carry_ref[...] = jnp.zeros_like(carry_ref)

    logits = _dot_nt(w_ref[...].astype(BF16), h_ref[...]) + b_ref[...]
    hi, mid, lo = _split3(_log_sigmoid(logits))
    tri = triu_ref[...]
    cum = _dot(hi, tri) + _dot(mid, tri) + _dot(lo, tri) + carry_ref[:, 0:1]
    c_ref[...] = cum
    tm = cum.shape[1]
    carry_ref[...] = jnp.broadcast_to(cum[:, tm - 1:tm], carry_ref.shape)


def _fox_cumlog(h, wt, layer, row0, b_ff, tm=512):
    m, k = h.shape
    n_heads = b_ff.shape[0]
    assert row0 % n_heads == 0
    triu = jnp.asarray(np.triu(np.ones((tm, tm), np.float32)), BF16)
    return pl.pallas_call(
        _foxc_kernel,
        grid=(m // tm,),
        in_specs=[pl.BlockSpec((tm, k), lambda i: (i, 0)),
                  pl.BlockSpec((None, n_heads, k), lambda i: (layer, row0 // n_heads, 0)),
                  pl.BlockSpec((n_heads, 1), lambda i: (0, 0)),
                  pl.BlockSpec((tm, tm), lambda i: (0, 0))],
        out_specs=pl.BlockSpec((n_heads, tm), lambda i: (0, i)),
        out_shape=jax.ShapeDtypeStruct((n_heads, m), F32),
        scratch_shapes=[pltpu.VMEM((n_heads, LANES), F32)],
        compiler_params=_cparams(("arbitrary",)),
        name="fox_cumlog",
    )(h, wt, b_ff.reshape(n_heads, 1), triu)


NEG_BIG = -1e30


LOG2E = 1.0 / math.log(2.0)
ATT_ROWS = 32
ATT_UNROLL = 2


def _fox_attn_kernel(tab_ref, q_ref, k_ref, v_ref, ccol_ref, crow_ref, gate_ref, o_ref,
                     va_ref, sa_ref, sb_ref, pa_ref, pb_ref, ala_ref, alb_ref, acc_ref, m_ref,
                     *, tq, n_off, n_diag):
    @pl.when(pl.program_id(0) == 0)
    def _():
        va_ref[:, HEAD_DIM:2 * HEAD_DIM] = jnp.ones((va_ref.shape[0], HEAD_DIM), BF16)

    va_ref[:, 0:HEAD_DIM] = v_ref[...]
    m_ref[...] = jnp.full(m_ref.shape, NEG_BIG, F32)
    acc_ref[...] = jnp.zeros(acc_ref.shape, F32)

    def tile(n):
        q0 = pl.multiple_of(tab_ref[0, n] * tq, tq)
        k0 = pl.multiple_of(tab_ref[1, n] * tq, tq)
        return q0, k0

    def logits(n, s_ref):
        q0, k0 = tile(n)
        s_ref[...] = _dot_nt(q_ref[pl.ds(q0, tq), :], k_ref[pl.ds(k0, tq), :])

    def softmax(n, s_ref, p_ref, al_ref, masked):
        q0, k0 = tile(n)
        crow = crow_ref[0, :, pl.ds(k0, tq)] * LOG2E
        for r in range(tq // ATT_ROWS):
            rs = slice(r * ATT_ROWS, (r + 1) * ATT_ROWS)
            qs = pl.ds(q0 + r * ATT_ROWS, ATT_ROWS)
            s = s_ref[rs, :] - crow
            if masked:
                row = lax.broadcasted_iota(jnp.int32, (ATT_ROWS, tq), 0) + r * ATT_ROWS
                col = lax.broadcasted_iota(jnp.int32, (ATT_ROWS, tq), 1)
                s = jnp.where(col <= row, s, NEG_BIG)
            ct = ccol_ref[0, qs, :] * LOG2E
            m_old = m_ref[qs, :]
            m_new = jnp.maximum(m_old, jnp.max(s, axis=1, keepdims=True) + ct)
            p_ref[rs, :] = jnp.exp2(s - (m_new - ct)).astype(BF16)
            m_ref[qs, :] = m_new
            al_ref[rs, :] = jnp.exp2(m_old - m_new)

    def accumulate(n, p_ref, al_ref):
        q0, k0 = tile(n)
        qs = pl.ds(q0, tq)
        acc_ref[qs, :] = al_ref[...] * acc_ref[qs, :] + _dot(p_ref[...], va_ref[pl.ds(k0, tq), :])

    def run(first, count, masked):
        if count == 0:
            return
        last = first + count - 1
        nxt = lambda n: jnp.minimum(n, last)
        s_buf = (sa_ref, sb_ref)
        p_buf = ((pa_ref, ala_ref), (pb_ref, alb_ref))
        logits(first, s_buf[0])
        softmax(first, s_buf[0], *p_buf[0], masked)
        logits(nxt(first + 1), s_buf[1])
        n_loop = (count - 1) // ATT_UNROLL

        def body(j, carry):
            n = first + ATT_UNROLL * j
            for u in range(ATT_UNROLL):
                accumulate(n + u, *p_buf[u % 2])
                softmax(n + u + 1, s_buf[(u + 1) % 2], *p_buf[(u + 1) % 2], masked)
                logits(nxt(n + u + 2), s_buf[u % 2])
            return carry

        lax.fori_loop(0, n_loop, body, 0)
        n = first + ATT_UNROLL * n_loop
        rest = count - 1 - ATT_UNROLL * n_loop
        for u in range(rest + 1):
            accumulate(n + u, *p_buf[u % 2])
            if u + 1 <= rest:
                softmax(n + u + 1, s_buf[(u + 1) % 2], *p_buf[(u + 1) % 2], masked)
            if u + 2 <= rest:
                logits(n + u + 2, s_buf[u % 2])

    run(0, n_off, False)
    run(n_off, n_diag, True)

    acc = acc_ref[...]
    out = acc[:, 0:HEAD_DIM] / acc[:, HEAD_DIM:2 * HEAD_DIM]
    o_ref[...] = (out * _silu(gate_ref[...])).astype(o_ref.dtype)


def _fox_attn(zqkv, z, ccol, crow, g_off, n_heads, tq=512):
    s_len = zqkv.shape[0]
    nq = s_len // tq
    go = g_off // HEAD_DIM
    off = [(qi, kb) for kb in range(nq) for qi in range(kb + 1, nq)]
    diag = [(i, i) for i in range(nq)]
    tab = jnp.asarray(np.array(off + diag, np.int32).T)
    kern = functools.partial(_fox_attn_kernel, tq=tq, n_off=len(off), n_diag=len(diag))
    once = pl.Buffered(1)
    head_col = lambda base: pl.BlockSpec((s_len, HEAD_DIM), lambda h, t: (0, base + h),
                                         pipeline_mode=once)
    grid_spec = pltpu.PrefetchScalarGridSpec(
        num_scalar_prefetch=1,
        grid=(n_heads,),
        in_specs=[head_col(0), head_col(n_heads), head_col(2 * n_heads),
                  pl.BlockSpec((1, s_len, 1), lambda h, t: (h, 0, 0), pipeline_mode=once),
                  pl.BlockSpec((1, 1, s_len), lambda h, t: (h, 0, 0)),
                  head_col(go)],
        out_specs=pl.BlockSpec((s_len, HEAD_DIM), lambda h, t: (0, h)),
        scratch_shapes=[pltpu.VMEM((s_len, 2 * HEAD_DIM), BF16),
                        pltpu.VMEM((tq, tq), F32),
                        pltpu.VMEM((tq, tq), F32),
                        pltpu.VMEM((tq, tq), BF16),
                        pltpu.VMEM((tq, tq), BF16),
                        pltpu.VMEM((tq, 1), F32),
                        pltpu.VMEM((tq, 1), F32),
                        pltpu.VMEM((s_len, 2 * HEAD_DIM), F32),
                        pltpu.VMEM((s_len, 1), F32)])
    return pl.pallas_call(
        kern,
        grid_spec=grid_spec,
        out_shape=jax.ShapeDtypeStruct((s_len, n_heads * HEAD_DIM), BF16),
        compiler_params=_cparams(("arbitrary",)),
        name="fox_attn",
    )(tab, zqkv, zqkv, zqkv, ccol, crow, z)


HG_PROWS = 8 * HG_SUB + 8 * (HG_SUB // 2)


HG_HEADS_PER_STEP = 2


def _hgrn_kernel(q_ref, f_ref, i_ref, gate_ref, lb_ref, ng_ref, tb_ref, ones_ref, mask_ref, o_ref,
                 *scratch, layer, rows):
    @pl.when(pl.program_id(1) == 0)
    def _():
        scratch[0][...] = jnp.zeros_like(scratch[0])

    for hh in range(HG_HEADS_PER_STEP):
        lanes = pl.ds(hh * HEAD_DIM, HEAD_DIM)
        _hgrn_head(*(r.at[:, lanes] for r in (q_ref, f_ref, i_ref, gate_ref, lb_ref, ng_ref)),
                   tb_ref, ones_ref, mask_ref, o_ref.at[:, lanes], *(s.at[hh] for s in scratch),
                   layer=layer, rows=rows)


def _hgrn_head(q_ref, f_ref, i_ref, gate_ref, lb_ref, ng_ref, tb_ref, ones_ref,
               mask_ref, o_ref, st_ref, qt_ref, kt_ref, w_ref, cum_ref, dd_ref, p_ref, sc_ref,
               acc_ref, *, layer, rows):
    z = f_ref[...]
    ls = _log_sigmoid(z)
    if layer == 0:
        g = ls
        logk = ls - z
    else:
        lbp = lb_ref[...]
        e = jnp.exp(lbp - jnp.max(lbp, axis=0, keepdims=True))
        p = e / jnp.sum(e, axis=0, keepdims=True)
        lb = jnp.sum(p[1:layer + 1, :], axis=0, keepdims=True)
        a = jnp.log(lb)
        l1m = jnp.log1p(-lb)
        b = l1m + ls
        g = jnp.maximum(a, b) + jnp.log1p(jnp.exp(-jnp.abs(a - b)))
        logk = l1m + (ls - z)

    cums, tots = [], []
    for r0 in range(0, rows, LANES):
        ct = _dot01(tb_ref[...], g[r0:r0 + LANES, :])
        cums.append(ct[0:LANES, :])
        tots.append(ct[LANES:2 * LANES, :])
    cum = jnp.concatenate(cums, axis=0) * LOG2E
    tot = jnp.concatenate(tots, axis=0) * LOG2E
    w = cum - logk * LOG2E
    qt_ref[...] = (q_ref[...] * jnp.exp2(cum)).astype(BF16)
    kt_ref[...] = jnp.exp2(tot - w).astype(BF16)
    w_ref[...] = w
    cum_ref[...] = cum
    dd_ref[...] = jnp.exp2(tot)

    half = HG_SUB // 2
    n_groups = rows // HG_SUB

    def bcast_row(ref, r):
        return jnp.broadcast_to(ref[r:r + 1, :], (half, HEAD_DIM))

    for g_i in range(n_groups):
        r0 = g_i * HG_SUB
        p0 = g_i * HG_PROWS
        c_lo, c_hi = cum_ref[r0:r0 + half, :], cum_ref[r0 + half:r0 + HG_SUB, :]
        q_lo, q_hi = q_ref[r0:r0 + half, :], q_ref[r0 + half:r0 + HG_SUB, :]
        for s in range(half):
            w_s = bcast_row(w_ref, r0 + s)
            p_lo = q_lo * jnp.exp2(c_lo - w_s + mask_ref[s * half:(s + 1) * half, :])
            p_hi = q_hi * jnp.exp2(c_hi - w_s)
            p_ref[p0 + s * HG_SUB:p0 + (s + 1) * HG_SUB, :] = (
                jnp.concatenate([p_lo, p_hi], axis=0).astype(BF16))
        for s in range(0, half, 2):
            pa = q_hi * jnp.exp2(c_hi - bcast_row(w_ref, r0 + half + s)
                                + mask_ref[s * half:(s + 1) * half, :])
            pb = q_hi * jnp.exp2(c_hi - bcast_row(w_ref, r0 + half + s + 1)
                                + mask_ref[(s + 1) * half:(s + 2) * half, :])
            base = p0 + half * HG_SUB + s * half
            p_ref[base:base + HG_SUB, :] = jnp.concatenate([pa, pb], axis=0).astype(BF16)

    sc_ref[...] = _dot(p_ref[...], ones_ref[...])

    upds = [_dot_tn(i_ref[g_i * HG_SUB:(g_i + 1) * HG_SUB, :].astype(BF16),
                    kt_ref[g_i * HG_SUB:(g_i + 1) * HG_SUB, :]) for g_i in range(n_groups)]
    st = st_ref[...]
    for g_i in range(n_groups):
        r0 = g_i * HG_SUB
        p0 = g_i * HG_PROWS
        o_lo = jnp.zeros((half, HEAD_DIM), F32)
        o_hi = jnp.zeros((half, HEAD_DIM), F32)
        for s in range(half):
            v_s = bcast_row(i_ref, r0 + s)
            o_lo = o_lo + sc_ref[p0 + s * HG_SUB:p0 + s * HG_SUB + half, :] * v_s
            o_hi = o_hi + sc_ref[p0 + s * HG_SUB + half:p0 + (s + 1) * HG_SUB, :] * v_s
        for s in range(half):
            base = p0 + half * HG_SUB + s * half
            o_hi = o_hi + sc_ref[base:base + half, :] * bcast_row(i_ref, r0 + half + s)
        o_inter = _dot_nt(qt_ref[r0:r0 + HG_SUB, :], st.astype(BF16))
        acc_ref[r0:r0 + HG_SUB, :] = o_inter + jnp.concatenate([o_lo, o_hi], axis=0)
        st = st * dd_ref[r0:r0 + 1, :] + upds[g_i]
    st_ref[...] = st

    o = acc_ref[...]
    ms = jnp.mean(o * o, axis=-1, keepdims=True)
    o = o * lax.rsqrt(ms + EPS) * ng_ref[...]
    o_ref[...] = (o * _silu(gate_ref[...])).astype(o_ref.dtype)


def _hgrn(z, hg_lb, norm_g, layer, q_off, f_off, i_off, g_off, n_heads, rows=512):
    s_len = z.shape[0]
    depth = hg_lb.shape[0]
    hps = HG_HEADS_PER_STEP
    width = hps * HEAD_DIM
    assert n_heads % hps == 0 and all(o % width == 0 for o in (q_off, f_off, i_off, g_off))
    qo, fo, io, go = (o // width for o in (q_off, f_off, i_off, g_off))
    r = np.arange(LANES)
    same = (r[:, None] // HG_SUB) == (r[None, :] // HG_SUB)
    tb = jnp.asarray(np.concatenate([same & (r[None, :] <= r[:, None]), same]).astype(np.float32), BF16)
    ones = jnp.ones((HEAD_DIM, HEAD_DIM), BF16)
    half = HG_SUB // 2
    t_idx = np.arange(half)
    mask_np = np.where(t_idx[None, :, None] >= t_idx[:, None, None], 0.0, NEG_BIG)
    mask = jnp.asarray(np.broadcast_to(mask_np, (half, half, HEAD_DIM)).reshape(half * half, HEAD_DIM), F32)
    n_prows = (rows // HG_SUB) * HG_PROWS
    kern = functools.partial(_hgrn_kernel, layer=layer, rows=rows)
    blk_spec = lambda off: pl.BlockSpec((rows, width), lambda h, i: (i, off + h))
    const = lambda shape: pl.BlockSpec(shape, lambda h, i: (0, 0))
    per_head = lambda shape, dtype: pltpu.VMEM((hps,) + shape, dtype)
    return pl.pallas_call(
        kern,
        grid=(n_heads // hps, s_len // rows),
        in_specs=[blk_spec(qo), blk_spec(fo), blk_spec(io), blk_spec(go),
                  pl.BlockSpec((depth, width), lambda h, i: (0, h)),
                  pl.BlockSpec((1, width), lambda h, i: (0, h)),
                  const((2 * LANES, LANES)), const((HEAD_DIM, HEAD_DIM)),
                  const((half * half, HEAD_DIM))],
        out_specs=pl.BlockSpec((rows, width), lambda h, i: (i, h)),
        out_shape=jax.ShapeDtypeStruct((s_len, n_heads * HEAD_DIM), BF16),
        scratch_shapes=[per_head((HEAD_DIM, HEAD_DIM), F32),
                        per_head((rows, HEAD_DIM), BF16),
                        per_head((rows, HEAD_DIM), BF16),
                        per_head((rows, HEAD_DIM), F32),
                        per_head((rows, HEAD_DIM), F32),
                        per_head((rows, HEAD_DIM), F32),
                        per_head((n_prows, HEAD_DIM), BF16),
                        per_head((n_prows, HEAD_DIM), F32),
                        per_head((rows, HEAD_DIM), F32)],
        compiler_params=_cparams(("parallel", "arbitrary")),
        name="hgrn2",
    )(z, z, z, z, hg_lb, norm_g.reshape(1, -1), tb, ones, mask)


def _gelu_tanh(x):
    c = math.sqrt(2.0 / math.pi)
    return 0.5 * x * (1.0 + jnp.tanh(c * (x + 0.044715 * (x * x * x))))


def _s5_kernel(u_ref, perm_ref, permt_ref, bm_ref, cm_ref, are_ref, aim_ref, pw_ref,
               d_ref, o_ref, state_ref, x_ref, yp_ref, *, rows, half):
    @pl.when(pl.program_id(0) == 0)
    def _():
        state_ref[...] = jnp.zeros_like(state_ref)

    nblk = bm_ref.shape[0]
    nt = rows // 8
    u = u_ref[...]
    up = _dot(perm_ref[...], u.astype(BF16)).astype(BF16)
    sub = lax.broadcasted_iota(jnp.int32, (8, half), 0)

    re, im = slice(0, half), slice(half, 2 * half)
    for b in range(nblk):
        x_ref[b] = _dot(up[:, b * LANES:(b + 1) * LANES], bm_ref[b])
        are = are_ref[b]
        aim = aim_ref[b]
        xr = jnp.zeros((8, half), F32)
        xi = jnp.zeros((8, half), F32)
        for t in range(nt):
            rs = slice(t * 8, (t + 1) * 8)
            xr, xi = (are * xr - aim * xi + x_ref[b, rs, re],
                      are * xi + aim * xr + x_ref[b, rs, im])
            x_ref[b, rs, re] = xr
            x_ref[b, rs, im] = xi

        er, ei = xr, xi
        alre = pw_ref[b, rows - 1:rows, re]
        alim = pw_ref[b, rows - 1:rows, im]
        cr = state_ref[b, 0:1, re]
        ci = state_ref[b, 0:1, im]
        ctr = jnp.zeros((8, half), F32)
        cti = jnp.zeros((8, half), F32)
        for s in range(8):
            ctr = jnp.where(sub == s, cr, ctr)
            cti = jnp.where(sub == s, ci, cti)
            cr, ci = (alre * cr - alim * ci + er[s:s + 1, :],
                      alre * ci + alim * cr + ei[s:s + 1, :])
        state_ref[b, :, re] = jnp.broadcast_to(cr, (8, half))
        state_ref[b, :, im] = jnp.broadcast_to(ci, (8, half))

        for t in range(nt):
            rs = slice(t * 8, (t + 1) * 8)
            pr = pw_ref[b, rs, re]
            pi = pw_ref[b, rs, im]
            x_ref[b, rs, re] = x_ref[b, rs, re] + (pr * ctr - pi * cti)
            x_ref[b, rs, im] = x_ref[b, rs, im] + (pr * cti + pi * ctr)
        yp_ref[:, b * LANES:(b + 1) * LANES] = _dot(x_ref[b].astype(BF16), cm_ref[b])

    y = _dot01(permt_ref[...], yp_ref[...]) + d_ref[...] * u
    o_ref[...] = _gelu_tanh(y).astype(o_ref.dtype)


def _s5(z, u_off, bm, cm, are, aim, pw, d_skip, rows=256):
    s_len = z.shape[0]
    width = d_skip.shape[0]
    nblk, _, two_half = bm.shape
    half = two_half // 2
    seg = rows // 8
    rho = np.arange(rows)
    t_of = (rho % 8) * seg + rho // 8
    perm_np = np.zeros((rows, rows), np.float32)
    perm_np[rho, t_of] = 1.0
    perm = jnp.asarray(perm_np, BF16)
    permt = jnp.asarray(perm_np.T, BF16)
    uo = u_off // width
    kern = functools.partial(_s5_kernel, rows=rows, half=half)
    c2 = lambda shape: pl.BlockSpec(shape, lambda i: (0, 0))
    c3 = lambda shape: pl.BlockSpec(shape, lambda i: (0, 0, 0))
    return pl.pallas_call(
        kern,
        grid=(s_len // rows,),
        in_specs=[pl.BlockSpec((rows, width), lambda i: (i, uo)),
                  c2((rows, rows)), c2((rows, rows)),
                  c3(bm.shape), c3(cm.shape),
                  c3(are.shape), c3(aim.shape),
                  pl.BlockSpec(pw.shape, lambda i: (0, 0, 0), pipeline_mode=pl.Buffered(1)),
                  c2((1, width))],
        out_specs=pl.BlockSpec((rows, width), lambda i: (i, 0)),
        out_shape=jax.ShapeDtypeStruct((s_len, width), BF16),
        scratch_shapes=[pltpu.VMEM((nblk, 8, two_half), F32),
                        pltpu.VMEM((nblk, rows, two_half), F32),
                        pltpu.VMEM((rows, width), F32)],
        compiler_params=_cparams(("arbitrary",)),
        name="s5",
    )(z, perm, permt, bm, cm, are, aim, pw, d_skip.reshape(1, width))


def _s5_params(a_re, a_im, log_dt, b_re, b_im, c_re, c_im, seg):
    g_n, p_n = a_re.shape
    nblk = g_n // S5_GPB
    dt = jnp.exp(log_dt)[:, None]
    mag = jnp.exp(a_re * dt)
    ang = a_im * dt
    abar_re = mag * jnp.cos(ang)
    abar_im = mag * jnp.sin(ang)
    nr = abar_re - 1.0
    den = a_re * a_re + a_im * a_im
    zr = (nr * a_re + abar_im * a_im) / den
    zi = (abar_im * a_re - nr * a_im) / den
    bbar_re = zr[:, :, None] * b_re - zi[:, :, None] * b_im
    bbar_im = zr[:, :, None] * b_im + zi[:, :, None] * b_re
    same = jnp.asarray(np.eye(S5_GPB, dtype=bool))

    def embed_b(bb):
        bb = bb.reshape(nblk, S5_GPB, 1, p_n, S5_GROUP).transpose(0, 1, 4, 2, 3)
        m = jnp.where(same[None, :, None, :, None], bb, 0.0)
        return m.reshape(nblk, S5_GPB * S5_GROUP, S5_GPB * p_n)

    bm = jnp.concatenate([embed_b(bbar_re), embed_b(bbar_im)], axis=2).astype(BF16)

    def embed_c(cc):
        cc = cc.reshape(nblk, S5_GPB, S5_GROUP, 1, p_n).transpose(0, 1, 4, 3, 2)
        m = jnp.where(same[None, :, None, :, None], cc, 0.0)
        return m.reshape(nblk, S5_GPB * p_n, S5_GPB * S5_GROUP)

    cm = jnp.concatenate([embed_c(c_re), embed_c(-c_im)], axis=1).astype(BF16)

    def tile8(v):
        v = v.reshape(nblk, 1, S5_GPB * p_n)
        return jnp.broadcast_to(v, (nblk, 8, S5_GPB * p_n))

    k = jnp.arange(1, seg + 1, dtype=F32)[:, None, None]
    mag_k = jnp.exp(k * (a_re * dt))
    ang_k = k * ang

    def table(t):
        t = t.reshape(seg, 1, nblk, S5_GPB * p_n)
        return jnp.broadcast_to(t, (seg, 8, nblk, S5_GPB * p_n)).transpose(2, 0, 1, 3).reshape(
            nblk, 8 * seg, S5_GPB * p_n)

    pw = jnp.concatenate([table(mag_k * jnp.cos(ang_k)), table(mag_k * jnp.sin(ang_k))], axis=2)
    return bm, cm, tile8(abar_re), tile8(abar_im), pw


def _glu_kernel(y_ref, w_ref, gate_ref, o_ref, *, width):
    zg = _dot(y_ref[...], w_ref[...])
    o = zg[:, :width] * _sigmoid(zg[:, width:]) * _silu(gate_ref[...])
    o_ref[...] = o.astype(o_ref.dtype)


def _glu(y, w_glu, layer, z, g_off, tm=512):
    m, width = y.shape
    go = g_off // width
    return pl.pallas_call(
        functools.partial(_glu_kernel, width=width),
        grid=(m // tm,),
        in_specs=[pl.BlockSpec((tm, width), lambda i: (i, 0)),
                  pl.BlockSpec((None, width, 2 * width), lambda i: (layer, 0, 0)),
                  pl.BlockSpec((tm, width), lambda i: (i, go))],
        out_specs=pl.BlockSpec((tm, width), lambda i: (i, 0)),
        out_shape=jax.ShapeDtypeStruct((m, width), BF16),
        compiler_params=_cparams(("parallel",)),
        name="s5_glu",
    )(y, w_glu, z)


def _merge_kernel(oa_ref, ob_ref, oc_ref, ga_ref, gb_ref, gc_ref, bg_ref, wa_ref, wb_ref, wc_ref,
                  wo_ref, x_ref, ng_ref, xo_ref, ho_ref, *, d):
    def gate(g_ref, i):
        return _sigmoid(g_ref[...] + bg_ref[:, i * d:(i + 1) * d])

    merged = gate(ga_ref, 0) * _dot(oa_ref[...], wa_ref[...])
    merged = merged + gate(gb_ref, 1) * _dot(ob_ref[...], wb_ref[...])
    merged = merged + gate(gc_ref, 2) * _dot(oc_ref[...], wc_ref[...])
    xn = x_ref[...] + _dot(merged.astype(BF16), wo_ref[...])
    xo_ref[...] = xn
    ms = jnp.mean(xn * xn, axis=-1, keepdims=True)
    ho_ref[...] = (xn * lax.rsqrt(ms + EPS) * ng_ref[...]).astype(ho_ref.dtype)


def _merge(oa, ob, oc, z, mg_off, b_gate, wa, wb, wc, wo, layer, x, next_g, h_dtype, tm=256):
    m, d = x.shape
    w = oa.shape[1]
    assert mg_off % d == 0
    mo = mg_off // d
    row = lambda width: pl.BlockSpec((tm, width), lambda i: (i, 0))
    gate = lambda k: pl.BlockSpec((tm, d), lambda i: (i, mo + k))
    resident = lambda shape: pl.BlockSpec(shape, lambda i: (0, 0), pipeline_mode=pl.Buffered(1))
    weight = lambda rows: pl.BlockSpec((None, rows, d), lambda i: (layer, 0, 0),
                                       pipeline_mode=pl.Buffered(1))
    return pl.pallas_call(
        functools.partial(_merge_kernel, d=d),
        grid=(m // tm,),
        in_specs=[row(w), row(w), row(w), gate(0), gate(1), gate(2),
                  resident((1, N_BRANCH * d)),
                  weight(w), weight(w), weight(w), weight(d),
                  row(d), resident((1, d))],
        out_specs=[row(d), row(d)],
        out_shape=[jax.ShapeDtypeStruct((m, d), F32), jax.ShapeDtypeStruct((m, d), h_dtype)],
        compiler_params=_cparams(("parallel",)),
        name="merge_out",
    )(oa, ob, oc, z, z, z, b_gate.reshape(1, -1), wa, wb, wc, wo, x, next_g.reshape(1, -1))


def kernel(x, norm_g, w_in, b_gate, fox_bf, hg_lb, hg_norm_g, s5_a_re, s5_a_im, s5_log_dt,
           s5_b_re, s5_b_im, s5_c_re, s5_c_im, s5_d, s5_w_glu, w_br_a, w_br_b, w_br_c, w_out,
           final_g):
    bsz, s_len, d = x.shape
    depth = w_in.shape[0]
    hg_w = hg_lb.shape[1]
    s5_w = s5_d.shape[1]
    n_fox = fox_bf.shape[1]
    fox_w = n_fox * HEAD_DIM
    n_hg = hg_w // HEAD_DIM
    sizes = (hg_w, hg_w, hg_w, hg_w, s5_w, s5_w, fox_w, fox_w, fox_w, n_fox, fox_w, N_BRANCH * d)
    offs = np.concatenate([[0], np.cumsum(sizes)])
    (o_hq, o_hf, o_hi, o_hg, o_su, o_sg, o_fq, o_fk, o_fv, o_ff, o_fg, o_mg, o_end) = (int(v) for v in offs)
    wt = jnp.transpose(w_in, (0, 2, 1))
    tn = fox_w
    tiles = lambda start, stop: list(range(start, stop, tn))
    o_mg2, o_fg2 = 0, o_end - o_mg
    w_glu, w_a, w_b, w_c, w_o = (w.astype(BF16) for w in (s5_w_glu, w_br_a, w_br_b, w_br_c, w_out))
    s5_rows = 256
    outs = []
    for b in range(bsz):
        xb = x[b]
        h = _rmsnorm(xb, norm_g[0], BF16)
        for l in range(depth):
            z = _inproj(h, wt, l, tiles(0, o_fq), F32, tn=tn)
            zqkv = _inproj(h, wt, l, tiles(o_fq, o_ff), BF16,
                           first_tile_scale=HEAD_DIM ** -0.5 * LOG2E, tn=tn)
            zg = _inproj(h, wt, l, tiles(o_mg, o_end) + tiles(o_fg, o_mg), F32, tn=tn)

            o_a = _hgrn(z, hg_lb, hg_norm_g[l], l, o_hq, o_hf, o_hi, o_hg, n_hg)

            s5p = _s5_params(s5_a_re[l], s5_a_im[l], s5_log_dt[l], s5_b_re[l], s5_b_im[l],
                             s5_c_re[l], s5_c_im[l], s5_rows // 8)
            y_b = _s5(z, o_su, *s5p, s5_d[l], rows=s5_rows)
            o_b = _glu(y_b, w_glu, l, z, o_sg)

            ct = _fox_cumlog(h, wt, l, o_ff, fox_bf[l])
            o_c = _fox_attn(zqkv, zg, ct[:, :, None], ct[:, None, :], o_fg2, n_fox)

            last = l == depth - 1
            next_g = final_g if last else norm_g[l + 1]
            xb, h = _merge(o_a, o_b, o_c, zg, o_mg2, b_gate[l], w_a, w_b, w_c, w_o, l,
                           xb, next_g, F32 if last else BF16)
        outs.append(h)
    return outs[0][None] if bsz == 1 else jnp.stack(outs, axis=0)
```

```python
import functools
import math

import numpy as np
import jax
import jax.numpy as jnp
from jax import lax
from jax.experimental import pallas as pl
from jax.experimental.pallas import tpu as pltpu

F32 = jnp.float32
BF16 = jnp.bfloat16

EPS = 1e-6
LANES = 128
VMEM_LIMIT = 56 * 1024 * 1024

HEAD_DIM = 128
S5_GROUP = 16
S5_STATE = 64
S5_GPB = 8
N_BRANCH = 3

HG_SUB = 16


def _cparams(sem, vmem=VMEM_LIMIT):
    return pltpu.CompilerParams(dimension_semantics=sem, vmem_limit_bytes=vmem)


def _dot(a, b):
    return jnp.dot(a, b, preferred_element_type=F32)


def _dot_nt(a, b):
    return lax.dot_general(a, b, (((1,), (1,)), ((), ())), preferred_element_type=F32)


def _dot_tn(a, b):
    return lax.dot_general(a, b, (((0,), (0,)), ((), ())), preferred_element_type=F32)


def _split3(x):
    hi = x.astype(BF16)
    r1 = x - hi.astype(F32)
    mid = r1.astype(BF16)
    lo = (r1 - mid.astype(F32)).astype(BF16)
    return hi, mid, lo


def _dot01(m, x, terms=3):
    parts = _split3(x)[:terms]
    out = _dot(m, parts[0])
    for part in parts[1:]:
        out = out + _dot(m, part)
    return out


def _log_sigmoid(z):
    return jnp.minimum(z, 0.0) - jnp.log1p(jnp.exp(-jnp.abs(z)))


def _sigmoid(z):
    return 1.0 / (1.0 + jnp.exp(-z))


def _silu(z):
    return z * _sigmoid(z)


def _rmsnorm_kernel(x_ref, g_ref, o_ref):
    x = x_ref[...]
    ms = jnp.mean(x * x, axis=-1, keepdims=True)
    o_ref[...] = (x * lax.rsqrt(ms + EPS) * g_ref[...]).astype(o_ref.dtype)


def _rmsnorm(x, g, out_dtype, tm=512):
    m, d = x.shape
    return pl.pallas_call(
        _rmsnorm_kernel,
        grid=(m // tm,),
        in_specs=[pl.BlockSpec((tm, d), lambda i: (i, 0)),
                  pl.BlockSpec((1, d), lambda i: (0, 0))],
        out_specs=pl.BlockSpec((tm, d), lambda i: (i, 0)),
        out_shape=jax.ShapeDtypeStruct((m, d), out_dtype),
        compiler_params=_cparams(("parallel",)),
        name="rmsnorm",
    )(x, g.reshape(1, d))


def _inproj_kernel(starts_ref, h_ref, w_ref, o_ref, wb_ref, *, first_tile_scale):
    del starts_ref
    @pl.when(pl.program_id(1) == 0)
    def _():
        wb_ref[...] = w_ref[0].T.astype(BF16)

    acc = _dot(h_ref[...], wb_ref[...])
    if first_tile_scale is not None:
        acc = acc * jnp.where(pl.program_id(0) == 0, first_tile_scale, 1.0)
    o_ref[...] = acc.astype(o_ref.dtype)


def _inproj(h, wt, layer, row_starts, out_dtype, first_tile_scale=None, tm=512, tn=1024):
    m, k = h.shape
    n_tiles = len(row_starts)
    assert all(r % 8 == 0 for r in row_starts)
    starts = jnp.asarray(np.asarray(row_starts, np.int32) // 8)
    grid_spec = pltpu.PrefetchScalarGridSpec(
        num_scalar_prefetch=1,
        grid=(n_tiles, m // tm),
        in_specs=[pl.BlockSpec((tm, k), lambda j, i, st: (i, 0)),
                  pl.BlockSpec((pl.Element(1), pl.Element(tn), pl.Element(k)),
                               lambda j, i, st: (layer, st[j] * 8, 0))],
        out_specs=pl.BlockSpec((tm, tn), lambda j, i, st: (i, j)),
        scratch_shapes=[pltpu.VMEM((k, tn), BF16)])
    return pl.pallas_call(
        functools.partial(_inproj_kernel, first_tile_scale=first_tile_scale),
        grid_spec=grid_spec,
        out_shape=jax.ShapeDtypeStruct((m, n_tiles * tn), out_dtype),
        compiler_params=_cparams(("parallel", "arbitrary")),
        name="inproj",
    )(starts, h, wt)


def _foxc_kernel(h_ref, w_ref, b_ref, triu_ref, c_ref, carry_ref):
    @pl.when(pl.program_id(0) == 0)
    def _():
        carry_ref[...] = jnp.zeros_like(carry_ref)

    logits = _dot_nt(w_ref[...].astype(BF16), h_ref[...]) + b_ref[...]
    hi, mid, lo = _split3(_log_sigmoid(logits))
    tri = triu_ref[...]
    cum = _dot(hi, tri) + _dot(mid, tri) + _dot(lo, tri) + carry_ref[:, 0:1]
    c_ref[...] = cum
    tm = cum.shape[1]
    carry_ref[...] = jnp.broadcast_to(cum[:, tm - 1:tm], carry_ref.shape)


def _fox_cumlog(h, wt, layer, row0, b_ff, tm=512):
    m, k = h.shape
    n_heads = b_ff.shape[0]
    assert row0 % n_heads == 0
    triu = jnp.asarray(np.triu(np.ones((tm, tm), np.float32)), BF16)
    return pl.pallas_call(
        _foxc_kernel,
        grid=(m // tm,),
        in_specs=[pl.BlockSpec((tm, k), lambda i: (i, 0)),
                  pl.BlockSpec((None, n_heads, k), lambda i: (layer, row0 // n_heads, 0)),
                  pl.BlockSpec((n_heads, 1), lambda i: (0, 0)),
                  pl.BlockSpec((tm, tm), lambda i: (0, 0))],
        out_specs=pl.BlockSpec((n_heads, tm), lambda i: (0, i)),
        out_shape=jax.ShapeDtypeStruct((n_heads, m), F32),
        scratch_shapes=[pltpu.VMEM((n_heads, LANES), F32)],
        compiler_params=_cparams(("arbitrary",)),
        name="fox_cumlog",
    )(h, wt, b_ff.reshape(n_heads, 1), triu)


NEG_BIG = -1e30


LOG2E = 1.0 / math.log(2.0)
ATT_ROWS = 32
ATT_UNROLL = 2


def _fox_attn_kernel(tab_ref, q_ref, k_ref, v_ref, ccol_ref, crow_ref, gate_ref, o_ref,
                     va_ref, sa_ref, sb_ref, pa_ref, pb_ref, ala_ref, alb_ref, acc_ref, m_ref,
                     *, tq, n_off, n_diag):
    @pl.when(pl.program_id(0) == 0)
    def _():
        va_ref[:, HEAD_DIM:2 * HEAD_DIM] = jnp.ones((va_ref.shape[0], HEAD_DIM), BF16)

    va_ref[:, 0:HEAD_DIM] = v_ref[...]
    m_ref[...] = jnp.full(m_ref.shape, NEG_BIG, F32)
    acc_ref[...] = jnp.zeros(acc_ref.shape, F32)

    def tile(n):
        q0 = pl.multiple_of(tab_ref[0, n] * tq, tq)
        k0 = pl.multiple_of(tab_ref[1, n] * tq, tq)
        return q0, k0

    def logits(n, s_ref):
        q0, k0 = tile(n)
        s_ref[...] = _dot_nt(q_ref[pl.ds(q0, tq), :], k_ref[pl.ds(k0, tq), :])

    def softmax(n, s_ref, p_ref, al_ref, masked):
        q0, k0 = tile(n)
        crow = crow_ref[0, :, pl.ds(k0, tq)] * LOG2E
        for r in range(tq // ATT_ROWS):
            rs = slice(r * ATT_ROWS, (r + 1) * ATT_ROWS)
            qs = pl.ds(q0 + r * ATT_ROWS, ATT_ROWS)
            s = s_ref[rs, :] - crow
            if masked:
                row = lax.broadcasted_iota(jnp.int32, (ATT_ROWS, tq), 0) + r * ATT_ROWS
                col = lax.broadcasted_iota(jnp.int32, (ATT_ROWS, tq), 1)
                s = jnp.where(col <= row, s, NEG_BIG)
            ct = ccol_ref[0, qs, :] * LOG2E
            m_old = m_ref[qs, :]
            m_new = jnp.maximum(m_old, jnp.max(s, axis=1, keepdims=True) + ct)
            p_ref[rs, :] = jnp.exp2(s - (m_new - ct)).astype(BF16)
            m_ref[qs, :] = m_new
            al_ref[rs, :] = jnp.exp2(m_old - m_new)

    def accumulate(n, p_ref, al_ref):
        q0, k0 = tile(n)
        qs = pl.ds(q0, tq)
        acc_ref[qs, :] = al_ref[...] * acc_ref[qs, :] + _dot(p_ref[...], va_ref[pl.ds(k0, tq), :])

    def run(first, count, masked):
        if count == 0:
            return
        last = first + count - 1
        nxt = lambda n: jnp.minimum(n, last)
        s_buf = (sa_ref, sb_ref)
        p_buf = ((pa_ref, ala_ref), (pb_ref, alb_ref))
        logits(first, s_buf[0])
        softmax(first, s_buf[0], *p_buf[0], masked)
        logits(nxt(first + 1), s_buf[1])
        n_loop = (count - 1) // ATT_UNROLL

        def body(j, carry):
            n = first + ATT_UNROLL * j
            for u in range(ATT_UNROLL):
                accumulate(n + u, *p_buf[u % 2])
                softmax(n + u + 1, s_buf[(u + 1) % 2], *p_buf[(u + 1) % 2], masked)
                logits(nxt(n + u + 2), s_buf[u % 2])
            return carry

        lax.fori_loop(0, n_loop, body, 0)
        n = first + ATT_UNROLL * n_loop
        rest = count - 1 - ATT_UNROLL * n_loop
        for u in range(rest + 1):
            accumulate(n + u, *p_buf[u % 2])
            if u + 1 <= rest:
                softmax(n + u + 1, s_buf[(u + 1) % 2], *p_buf[(u + 1) % 2], masked)
            if u + 2 <= rest:
                logits(n + u + 2, s_buf[u % 2])

    run(0, n_off, False)
    run(n_off, n_diag, True)

    acc = acc_ref[...]
    out = acc[:, 0:HEAD_DIM] / acc[:, HEAD_DIM:2 * HEAD_DIM]
    o_ref[...] = (out * _silu(gate_ref[...])).astype(o_ref.dtype)


def _fox_attn(zqkv, z, ccol, crow, g_off, n_heads, tq=512):
    s_len = zqkv.shape[0]
    nq = s_len // tq
    go = g_off // HEAD_DIM
    off = [(qi, kb) for kb in range(nq) for qi in range(kb + 1, nq)]
    diag = [(i, i) for i in range(nq)]
    tab = jnp.asarray(np.array(off + diag, np.int32).T)
    kern = functools.partial(_fox_attn_kernel, tq=tq, n_off=len(off), n_diag=len(diag))
    once = pl.Buffered(1)
    head_col = lambda base: pl.BlockSpec((s_len, HEAD_DIM), lambda h, t: (0, base + h),
                                         pipeline_mode=once)
    grid_spec = pltpu.PrefetchScalarGridSpec(
        num_scalar_prefetch=1,
        grid=(n_heads,),
        in_specs=[head_col(0), head_col(n_heads), head_col(2 * n_heads),
                  pl.BlockSpec((1, s_len, 1), lambda h, t: (h, 0, 0), pipeline_mode=once),
                  pl.BlockSpec((1, 1, s_len), lambda h, t: (h, 0, 0)),
                  head_col(go)],
        out_specs=pl.BlockSpec((s_len, HEAD_DIM), lambda h, t: (0, h)),
        scratch_shapes=[pltpu.VMEM((s_len, 2 * HEAD_DIM), BF16),
                        pltpu.VMEM((tq, tq), F32),
                        pltpu.VMEM((tq, tq), F32),
                        pltpu.VMEM((tq, tq), BF16),
                        pltpu.VMEM((tq, tq), BF16),
                        pltpu.VMEM((tq, 1), F32),
                        pltpu.VMEM((tq, 1), F32),
                        pltpu.VMEM((s_len, 2 * HEAD_DIM), F32),
                        pltpu.VMEM((s_len, 1), F32)])
    return pl.pallas_call(
        kern,
        grid_spec=grid_spec,
        out_shape=jax.ShapeDtypeStruct((s_len, n_heads * HEAD_DIM), BF16),
        compiler_params=_cparams(("arbitrary",)),
        name="fox_attn",
    )(tab, zqkv, zqkv, zqkv, ccol, crow, z)


HG_PROWS = 8 * HG_SUB + 8 * (HG_SUB // 2)


HG_HEADS_PER_STEP = 2


def _hgrn_kernel(q_ref, f_ref, i_ref, gate_ref, lb_ref, ng_ref, tb_ref, ones_ref, mask_ref, o_ref,
                 *scratch, layer, rows):
    @pl.when(pl.program_id(1) == 0)
    def _():
        scratch[0][...] = jnp.zeros_like(scratch[0])

    for hh in range(HG_HEADS_PER_STEP):
        lanes = pl.ds(hh * HEAD_DIM, HEAD_DIM)
        _hgrn_head(*(r.at[:, lanes] for r in (q_ref, f_ref, i_ref, gate_ref, lb_ref, ng_ref)),
                   tb_ref, ones_ref, mask_ref, o_ref.at[:, lanes], *(s.at[hh] for s in scratch),
                   layer=layer, rows=rows)


def _hgrn_head(q_ref, f_ref, i_ref, gate_ref, lb_ref, ng_ref, tb_ref, ones_ref,
               mask_ref, o_ref, st_ref, qt_ref, kt_ref, w_ref, cum_ref, dd_ref, p_ref, sc_ref,
               acc_ref, *, layer, rows):
    z = f_ref[...]
    ls = _log_sigmoid(z)
    if layer == 0:
        g = ls
        logk = ls - z
    else:
        lbp = lb_ref[...]
        e = jnp.exp(lbp - jnp.max(lbp, axis=0, keepdims=True))
        p = e / jnp.sum(e, axis=0, keepdims=True)
        lb = jnp.sum(p[1:layer + 1, :], axis=0, keepdims=True)
        a = jnp.log(lb)
        l1m = jnp.log1p(-lb)
        b = l1m + ls
        g = jnp.maximum(a, b) + jnp.log1p(jnp.exp(-jnp.abs(a - b)))
        logk = l1m + (ls - z)

    cums, tots = [], []
    for r0 in range(0, rows, LANES):
        ct = _dot01(tb_ref[...], g[r0:r0 + LANES, :])
        cums.append(ct[0:LANES, :])
        tots.append(ct[LANES:2 * LANES, :])
    cum = jnp.concatenate(cums, axis=0) * LOG2E
    tot = jnp.concatenate(tots, axis=0) * LOG2E
    w = cum - logk * LOG2E
    qt_ref[...] = (q_ref[...] * jnp.exp2(cum)).astype(BF16)
    kt_ref[...] = jnp.exp2(tot - w).astype(BF16)
    w_ref[...] = w
    cum_ref[...] = cum
    dd_ref[...] = jnp.exp2(tot)

    half = HG_SUB // 2
    n_groups = rows // HG_SUB

    def bcast_row(ref, r):
        return jnp.broadcast_to(ref[r:r + 1, :], (half, HEAD_DIM))

    for g_i in range(n_groups):
        r0 = g_i * HG_SUB
        p0 = g_i * HG_PROWS
        c_lo, c_hi = cum_ref[r0:r0 + half, :], cum_ref[r0 + half:r0 + HG_SUB, :]
        q_lo, q_hi = q_ref[r0:r0 + half, :], q_ref[r0 + half:r0 + HG_SUB, :]
        for s in range(half):
            w_s = bcast_row(w_ref, r0 + s)
            p_lo = q_lo * jnp.exp2(c_lo - w_s + mask_ref[s * half:(s + 1) * half, :])
            p_hi = q_hi * jnp.exp2(c_hi - w_s)
            p_ref[p0 + s * HG_SUB:p0 + (s + 1) * HG_SUB, :] = (
                jnp.concatenate([p_lo, p_hi], axis=0).astype(BF16))
        for s in range(0, half, 2):
            pa = q_hi * jnp.exp2(c_hi - bcast_row(w_ref, r0 + half + s)
                                + mask_ref[s * half:(s + 1) * half, :])
            pb = q_hi * jnp.exp2(c_hi - bcast_row(w_ref, r0 + half + s + 1)
                                + mask_ref[(s + 1) * half:(s + 2) * half, :])
            base = p0 + half * HG_SUB + s * half
            p_ref[base:base + HG_SUB, :] = jnp.concatenate([pa, pb], axis=0).astype(BF16)

    sc_ref[...] = _dot(p_ref[...], ones_ref[...])

    upds = [_dot_tn(i_ref[g_i * HG_SUB:(g_i + 1) * HG_SUB, :].astype(BF16),
                    kt_ref[g_i * HG_SUB:(g_i + 1) * HG_SUB, :]) for g_i in range(n_groups)]
    st = st_ref[...]
    for g_i in range(n_groups):
        r0 = g_i * HG_SUB
        p0 = g_i * HG_PROWS
        o_lo = jnp.zeros((half, HEAD_DIM), F32)
        o_hi = jnp.zeros((half, HEAD_DIM), F32)
        for s in range(half):
            v_s = bcast_row(i_ref, r0 + s)
            o_lo = o_lo + sc_ref[p0 + s * HG_SUB:p0 + s * HG_SUB + half, :] * v_s
            o_hi = o_hi + sc_ref[p0 + s * HG_SUB + half:p0 + (s + 1) * HG_SUB, :] * v_s
        for s in range(half):
            base = p0 + half * HG_SUB + s * half
            o_hi = o_hi + sc_ref[base:base + half, :] * bcast_row(i_ref, r0 + half + s)
        o_inter = _dot_nt(qt_ref[r0:r0 + HG_SUB, :], st.astype(BF16))
        acc_ref[r0:r0 + HG_SUB, :] = o_inter + jnp.concatenate([o_lo, o_hi], axis=0)
        st = st * dd_ref[r0:r0 + 1, :] + upds[g_i]
    st_ref[...] = st

    o = acc_ref[...]
    ms = jnp.mean(o * o, axis=-1, keepdims=True)
    o = o * lax.rsqrt(ms + EPS) * ng_ref[...]
    o_ref[...] = (o * _silu(gate_ref[...])).astype(o_ref.dtype)


def _hgrn(z, hg_lb, norm_g, layer, q_off, f_off, i_off, g_off, n_heads, rows=512):
    s_len = z.shape[0]
    depth = hg_lb.shape[0]
    hps = HG_HEADS_PER_STEP
    width = hps * HEAD_DIM
    assert n_heads % hps == 0 and all(o % width == 0 for o in (q_off, f_off, i_off, g_off))
    qo, fo, io, go = (o // width for o in (q_off, f_off, i_off, g_off))
    r = np.arange(LANES)
    same = (r[:, None] // HG_SUB) == (r[None, :] // HG_SUB)
    tb = jnp.asarray(np.concatenate([same & (r[None, :] <= r[:, None]), same]).astype(np.float32), BF16)
    ones = jnp.ones((HEAD_DIM, HEAD_DIM), BF16)
    half = HG_SUB // 2
    t_idx = np.arange(half)
    mask_np = np.where(t_idx[None, :, None] >= t_idx[:, None, None], 0.0, NEG_BIG)
    mask = jnp.asarray(np.broadcast_to(mask_np, (half, half, HEAD_DIM)).reshape(half * half, HEAD_DIM), F32)
    n_prows = (rows // HG_SUB) * HG_PROWS
    kern = functools.partial(_hgrn_kernel, layer=layer, rows=rows)
    blk_spec = lambda off: pl.BlockSpec((rows, width), lambda h, i: (i, off + h))
    const = lambda shape: pl.BlockSpec(shape, lambda h, i: (0, 0))
    per_head = lambda shape, dtype: pltpu.VMEM((hps,) + shape, dtype)
    return pl.pallas_call(
        kern,
        grid=(n_heads // hps, s_len // rows),
        in_specs=[blk_spec(qo), blk_spec(fo), blk_spec(io), blk_spec(go),
                  pl.BlockSpec((depth, width), lambda h, i: (0, h)),
                  pl.BlockSpec((1, width), lambda h, i: (0, h)),
                  const((2 * LANES, LANES)), const((HEAD_DIM, HEAD_DIM)),
                  const((half * half, HEAD_DIM))],
        out_specs=pl.BlockSpec((rows, width), lambda h, i: (i, h)),
        out_shape=jax.ShapeDtypeStruct((s_len, n_heads * HEAD_DIM), BF16),
        scratch_shapes=[per_head((HEAD_DIM, HEAD_DIM), F32),
                        per_head((rows, HEAD_DIM), BF16),
                        per_head((rows, HEAD_DIM), BF16),
                        per_head((rows, HEAD_DIM), F32),
                        per_head((rows, HEAD_DIM), F32),
                        per_head((rows, HEAD_DIM), F32),
                        per_head((n_prows, HEAD_DIM), BF16),
                        per_head((n_prows, HEAD_DIM), F32),
                        per_head((rows, HEAD_DIM), F32)],
        compiler_params=_cparams(("parallel", "arbitrary")),
        name="hgrn2",
    )(z, z, z, z, hg_lb, norm_g.reshape(1, -1), tb, ones, mask)


def _gelu_tanh(x):
    c = math.sqrt(2.0 / math.pi)
    return 0.5 * x * (1.0 + jnp.tanh(c * (x + 0.044715 * (x * x * x))))


def _s5_kernel(u_ref, perm_ref, permt_ref, bm_ref, cm_ref, are_ref, aim_ref, pw_ref,
               d_ref, o_ref, state_ref, x_ref, yp_ref, *, rows, half):
    @pl.when(pl.program_id(0) == 0)
    def _():
        state_ref[...] = jnp.zeros_like(state_ref)

    nblk = bm_ref.shape[0]
    nt = rows // 8
    u = u_ref[...]
    up = _dot(perm_ref[...], u.astype(BF16)).astype(BF16)
    sub = lax.broadcasted_iota(jnp.int32, (8, half), 0)

    re, im = slice(0, half), slice(half, 2 * half)
    for b in range(nblk):
        x_ref[b] = _dot(up[:, b * LANES:(b + 1) * LANES], bm_ref[b])
    for b in range(nblk):
        are = are_ref[b]
        aim = aim_ref[b]
        xr = jnp.zeros((8, half), F32)
        xi = jnp.zeros((8, half), F32)
        for t in range(nt):
            rs = slice(t * 8, (t + 1) * 8)
            xr, xi = (are * xr - aim * xi + x_ref[b, rs, re],
                      are * xi + aim * xr + x_ref[b, rs, im])
            x_ref[b, rs, re] = xr
            x_ref[b, rs, im] = xi

        er, ei = xr, xi
        alre = pw_ref[b, rows - 1:rows, re]
        alim = pw_ref[b, rows - 1:rows, im]
        cr = state_ref[b, 0:1, re]
        ci = state_ref[b, 0:1, im]
        ctr = jnp.zeros((8, half), F32)
        cti = jnp.zeros((8, half), F32)
        for s in range(8):
            ctr = jnp.where(sub == s, cr, ctr)
            cti = jnp.where(sub == s, ci, cti)
            cr, ci = (alre * cr - alim * ci + er[s:s + 1, :],
                      alre * ci + alim * cr + ei[s:s + 1, :])
        state_ref[b, :, re] = jnp.broadcast_to(cr, (8, half))
        state_ref[b, :, im] = jnp.broadcast_to(ci, (8, half))

        for t in range(nt):
            rs = slice(t * 8, (t + 1) * 8)
            pr = pw_ref[b, rs, re]
            pi = pw_ref[b, rs, im]
            x_ref[b, rs, re] = x_ref[b, rs, re] + (pr * ctr - pi * cti)
            x_ref[b, rs, im] = x_ref[b, rs, im] + (pr * cti + pi * ctr)
        yp_ref[:, b * LANES:(b + 1) * LANES] = _dot(x_ref[b].astype(BF16), cm_ref[b])

    y = _dot01(permt_ref[...], yp_ref[...], terms=2) + d_ref[...] * u
    o_ref[...] = _gelu_tanh(y).astype(o_ref.dtype)


def _s5(z, u_off, bm, cm, are, aim, pw, d_skip, rows=256):
    s_len = z.shape[0]
    width = d_skip.shape[0]
    nblk, _, two_half = bm.shape
    half = two_half // 2
    seg = rows // 8
    rho = np.arange(rows)
    t_of = (rho % 8) * seg + rho // 8
    perm_np = np.zeros((rows, rows), np.float32)
    perm_np[rho, t_of] = 1.0
    perm = jnp.asarray(perm_np, BF16)
    permt = jnp.asarray(perm_np.T, BF16)
    uo = u_off // width
    kern = functools.partial(_s5_kernel, rows=rows, half=half)
    c2 = lambda shape: pl.BlockSpec(shape, lambda i: (0, 0))
    c3 = lambda shape: pl.BlockSpec(shape, lambda i: (0, 0, 0))
    return pl.pallas_call(
        kern,
        grid=(s_len // rows,),
        in_specs=[pl.BlockSpec((rows, width), lambda i: (i, uo)),
                  c2((rows, rows)), c2((rows, rows)),
                  c3(bm.shape), c3(cm.shape),
                  c3(are.shape), c3(aim.shape),
                  pl.BlockSpec(pw.shape, lambda i: (0, 0, 0), pipeline_mode=pl.Buffered(1)),
                  c2((1, width))],
        out_specs=pl.BlockSpec((rows, width), lambda i: (i, 0)),
        out_shape=jax.ShapeDtypeStruct((s_len, width), BF16),
        scratch_shapes=[pltpu.VMEM((nblk, 8, two_half), F32),
                        pltpu.VMEM((nblk, rows, two_half), F32),
                        pltpu.VMEM((rows, width), F32)],
        compiler_params=_cparams(("arbitrary",)),
        name="s5",
    )(z, perm, permt, bm, cm, are, aim, pw, d_skip.reshape(1, width))


def _s5_params(a_re, a_im, log_dt, b_re, b_im, c_re, c_im, seg):
    g_n, p_n = a_re.shape
    nblk = g_n // S5_GPB
    dt = jnp.exp(log_dt)[:, None]
    mag = jnp.exp(a_re * dt)
    ang = a_im * dt
    abar_re = mag * jnp.cos(ang)
    abar_im = mag * jnp.sin(ang)
    nr = abar_re - 1.0
    den = a_re * a_re + a_im * a_im
    zr = (nr * a_re + abar_im * a_im) / den
    zi = (abar_im * a_re - nr * a_im) / den
    bbar_re = zr[:, :, None] * b_re - zi[:, :, None] * b_im
    bbar_im = zr[:, :, None] * b_im + zi[:, :, None] * b_re
    same = jnp.asarray(np.eye(S5_GPB, dtype=bool))

    def embed_b(bb):
        bb = bb.reshape(nblk, S5_GPB, 1, p_n, S5_GROUP).transpose(0, 1, 4, 2, 3)
        m = jnp.where(same[None, :, None, :, None], bb, 0.0)
        return m.reshape(nblk, S5_GPB * S5_GROUP, S5_GPB * p_n)

    bm = jnp.concatenate([embed_b(bbar_re), embed_b(bbar_im)], axis=2).astype(BF16)

    def embed_c(cc):
        cc = cc.reshape(nblk, S5_GPB, S5_GROUP, 1, p_n).transpose(0, 1, 4, 3, 2)
        m = jnp.where(same[None, :, None, :, None], cc, 0.0)
        return m.reshape(nblk, S5_GPB * p_n, S5_GPB * S5_GROUP)

    cm = jnp.concatenate([embed_c(c_re), embed_c(-c_im)], axis=1).astype(BF16)

    def tile8(v):
        v = v.reshape(nblk, 1, S5_GPB * p_n)
        return jnp.broadcast_to(v, (nblk, 8, S5_GPB * p_n))

    k = jnp.arange(1, seg + 1, dtype=F32)[:, None, None]
    mag_k = jnp.exp(k * (a_re * dt))
    ang_k = k * ang

    def table(t):
        t = t.reshape(seg, 1, nblk, S5_GPB * p_n)
        return jnp.broadcast_to(t, (seg, 8, nblk, S5_GPB * p_n)).transpose(2, 0, 1, 3).reshape(
            nblk, 8 * seg, S5_GPB * p_n)

    pw = jnp.concatenate([table(mag_k * jnp.cos(ang_k)), table(mag_k * jnp.sin(ang_k))], axis=2)
    return bm, cm, tile8(abar_re), tile8(abar_im), pw


def _glu_kernel(y_ref, w_ref, gate_ref, o_ref, *, width):
    zg = _dot(y_ref[...], w_ref[...])
    o = zg[:, :width] * _sigmoid(zg[:, width:]) * _silu(gate_ref[...])
    o_ref[...] = o.astype(o_ref.dtype)


def _glu(y, w_glu, layer, z, g_off, tm=512):
    m, width = y.shape
    go = g_off // width
    return pl.pallas_call(
        functools.partial(_glu_kernel, width=width),
        grid=(m // tm,),
        in_specs=[pl.BlockSpec((tm, width), lambda i: (i, 0)),
                  pl.BlockSpec((None, width, 2 * width), lambda i: (layer, 0, 0)),
                  pl.BlockSpec((tm, width), lambda i: (i, go))],
        out_specs=pl.BlockSpec((tm, width), lambda i: (i, 0)),
        out_shape=jax.ShapeDtypeStruct((m, width), BF16),
        compiler_params=_cparams(("parallel",)),
        name="s5_glu",
    )(y, w_glu, z)


def _merge_kernel(oa_ref, ob_ref, oc_ref, ga_ref, gb_ref, gc_ref, bg_ref, wa_ref, wb_ref, wc_ref,
                  wo_ref, x_ref, ng_ref, xo_ref, ho_ref, *, d):
    def gate(g_ref, i):
        return _sigmoid(g_ref[...] + bg_ref[:, i * d:(i + 1) * d])

    merged = gate(ga_ref, 0) * _dot(oa_ref[...], wa_ref[...])
    merged = merged + gate(gb_ref, 1) * _dot(ob_ref[...], wb_ref[...])
    merged = merged + gate(gc_ref, 2) * _dot(oc_ref[...], wc_ref[...])
    xn = x_ref[...] + _dot(merged.astype(BF16), wo_ref[...])
    xo_ref[...] = xn
    ms = jnp.mean(xn * xn, axis=-1, keepdims=True)
    ho_ref[...] = (xn * lax.rsqrt(ms + EPS) * ng_ref[...]).astype(ho_ref.dtype)


def _merge(oa, ob, oc, z, mg_off, b_gate, wa, wb, wc, wo, layer, x, next_g, h_dtype, tm=256):
    m, d = x.shape
    w = oa.shape[1]
    assert mg_off % d == 0
    mo = mg_off // d
    row = lambda width: pl.BlockSpec((tm, width), lambda i: (i, 0))
    gate = lambda k: pl.BlockSpec((tm, d), lambda i: (i, mo + k))
    resident = lambda shape: pl.BlockSpec(shape, lambda i: (0, 0), pipeline_mode=pl.Buffered(1))
    weight = lambda rows: pl.BlockSpec((None, rows, d), lambda i: (layer, 0, 0),
                                       pipeline_mode=pl.Buffered(1))
    return pl.pallas_call(
        functools.partial(_merge_kernel, d=d),
        grid=(m // tm,),
        in_specs=[row(w), row(w), row(w), gate(0), gate(1), gate(2),
                  resident((1, N_BRANCH * d)),
                  weight(w), weight(w), weight(w), weight(d),
                  row(d), resident((1, d))],
        out_specs=[row(d), row(d)],
        out_shape=[jax.ShapeDtypeStruct((m, d), F32), jax.ShapeDtypeStruct((m, d), h_dtype)],
        compiler_params=_cparams(("parallel",)),
        name="merge_out",
    )(oa, ob, oc, z, z, z, b_gate.reshape(1, -1), wa, wb, wc, wo, x, next_g.reshape(1, -1))


def kernel(x, norm_g, w_in, b_gate, fox_bf, hg_lb, hg_norm_g, s5_a_re, s5_a_im, s5_log_dt,
           s5_b_re, s5_b_im, s5_c_re, s5_c_im, s5_d, s5_w_glu, w_br_a, w_br_b, w_br_c, w_out,
           final_g):
    bsz, s_len, d = x.shape
    depth = w_in.shape[0]
    hg_w = hg_lb.shape[1]
    s5_w = s5_d.shape[1]
    n_fox = fox_bf.shape[1]
    fox_w = n_fox * HEAD_DIM
    n_hg = hg_w // HEAD_DIM
    sizes = (hg_w, hg_w, hg_w, hg_w, s5_w, s5_w, fox_w, fox_w, fox_w, n_fox, fox_w, N_BRANCH * d)
    offs = np.concatenate([[0], np.cumsum(sizes)])
    (o_hq, o_hf, o_hi, o_hg, o_su, o_sg, o_fq, o_fk, o_fv, o_ff, o_fg, o_mg, o_end) = (int(v) for v in offs)
    wt = jnp.transpose(w_in, (0, 2, 1))
    tn = fox_w
    tiles = lambda start, stop: list(range(start, stop, tn))
    o_mg2, o_fg2 = 0, o_end - o_mg
    w_glu, w_a, w_b, w_c, w_o = (w.astype(BF16) for w in (s5_w_glu, w_br_a, w_br_b, w_br_c, w_out))
    s5_rows = 256
    outs = []
    for b in range(bsz):
        xb = x[b]
        h = _rmsnorm(xb, norm_g[0], BF16)
        for l in range(depth):
            z = _inproj(h, wt, l, tiles(0, o_fq), F32, tn=tn)
            zqkv = _inproj(h, wt, l, tiles(o_fq, o_ff), BF16,
                           first_tile_scale=HEAD_DIM ** -0.5 * LOG2E, tn=tn)
            zg = _inproj(h, wt, l, tiles(o_mg, o_end) + tiles(o_fg, o_mg), F32, tn=tn)

            o_a = _hgrn(z, hg_lb, hg_norm_g[l], l, o_hq, o_hf, o_hi, o_hg, n_hg)

            s5p = _s5_params(s5_a_re[l], s5_a_im[l], s5_log_dt[l], s5_b_re[l], s5_b_im[l],
                             s5_c_re[l], s5_c_im[l], s5_rows // 8)
            y_b = _s5(z, o_su, *s5p, s5_d[l], rows=s5_rows)
            o_b = _glu(y_b, w_glu, l, z, o_sg)

            ct = _fox_cumlog(h, wt, l, o_ff, fox_bf[l])
            o_c = _fox_attn(zqkv, zg, ct[:, :, None], ct[:, None, :], o_fg2, n_fox)

            last = l == depth - 1
            next_g = final_g if last else norm_g[l + 1]
            xb, h = _merge(o_a, o_b, o_c, zg, o_mg2, b_gate[l], w_a, w_b, w_c, w_o, l,
                           xb, next_g, F32 if last else BF16)
        outs.append(h)
    return outs[0][None] if bsz == 1 else jnp.stack(outs, axis=0)
```

```python
import functools
import math

import numpy as np
import jax
import jax.numpy as jnp
from jax import lax
from jax.experimental import pallas as pl
from jax.experimental.pallas import tpu as pltpu

F32 = jnp.float32
BF16 = jnp.bfloat16

EPS = 1e-6
LANES = 128
VMEM_LIMIT = 56 * 1024 * 1024

HEAD_DIM = 128
S5_GROUP = 16
S5_STATE = 64
S5_GPB = 8
N_BRANCH = 3

HG_SUB = 16


def _cparams(sem, vmem=VMEM_LIMIT):
    return pltpu.CompilerParams(dimension_semantics=sem, vmem_limit_bytes=vmem)


def _dot(a, b):
    return jnp.dot(a, b, preferred_element_type=F32)


def _dot_nt(a, b):
    return lax.dot_general(a, b, (((1,), (1,)), ((), ())), preferred_element_type=F32)


def _dot_tn(a, b):
    return lax.dot_general(a, b, (((0,), (0,)), ((), ())), preferred_element_type=F32)


def _split3(x):
    hi = x.astype(BF16)
    r1 = x - hi.astype(F32)
    mid = r1.astype(BF16)
    lo = (r1 - mid.astype(F32)).astype(BF16)
    return hi, mid, lo


def _dot01(m, x, terms=3):
    parts = _split3(x)[:terms]
    out = _dot(m, parts[0])
    for part in parts[1:]:
        out = out + _dot(m, part)
    return out


def _log_sigmoid(z):
    return jnp.minimum(z, 0.0) - jnp.log1p(jnp.exp(-jnp.abs(z)))


def _sigmoid(z):
    return 1.0 / (1.0 + jnp.exp(-z))


def _silu(z):
    return z * _sigmoid(z)


def _rmsnorm_kernel(x_ref, g_ref, o_ref):
    x = x_ref[...]
    ms = jnp.mean(x * x, axis=-1, keepdims=True)
    o_ref[...] = (x * lax.rsqrt(ms + EPS) * g_ref[...]).astype(o_ref.dtype)


def _rmsnorm(x, g, out_dtype, tm=512):
    m, d = x.shape
    return pl.pallas_call(
        _rmsnorm_kernel,
        grid=(m // tm,),
        in_specs=[pl.BlockSpec((tm, d), lambda i: (i, 0)),
                  pl.BlockSpec((1, d), lambda i: (0, 0))],
        out_specs=pl.BlockSpec((tm, d), lambda i: (i, 0)),
        out_shape=jax.ShapeDtypeStruct((m, d), out_dtype),
        compiler_params=_cparams(("parallel",)),
        name="rmsnorm",
    )(x, g.reshape(1, d))


def _inproj_kernel(starts_ref, h_ref, w_ref, o_ref, wb_ref, *, first_tile_scale):
    del starts_ref
    @pl.when(pl.program_id(1) == 0)
    def _():
        wb_ref[...] = w_ref[0].T.astype(BF16)

    acc = _dot(h_ref[...], wb_ref[...])
    if first_tile_scale is not None:
        acc = acc * jnp.where(pl.program_id(0) == 0, first_tile_scale, 1.0)
    o_ref[...] = acc.astype(o_ref.dtype)


def _inproj(h, wt, layer, row_starts, out_dtype, first_tile_scale=None, tm=1024, tn=1024):
    m, k = h.shape
    n_tiles = len(row_starts)
    assert all(r % 8 == 0 for r in row_starts)
    starts = jnp.asarray(np.asarray(row_starts, np.int32) // 8)
    grid_spec = pltpu.PrefetchScalarGridSpec(
        num_scalar_prefetch=1,
        grid=(n_tiles, m // tm),
        in_specs=[pl.BlockSpec((tm, k), lambda j, i, st: (i, 0)),
                  pl.BlockSpec((pl.Element(1), pl.Element(tn), pl.Element(k)),
                               lambda j, i, st: (layer, st[j] * 8, 0))],
        out_specs=pl.BlockSpec((tm, tn), lambda j, i, st: (i, j)),
        scratch_shapes=[pltpu.VMEM((k, tn), BF16)])
    return pl.pallas_call(
        functools.partial(_inproj_kernel, first_tile_scale=first_tile_scale),
        grid_spec=grid_spec,
        out_shape=jax.ShapeDtypeStruct((m, n_tiles * tn), out_dtype),
        compiler_params=_cparams(("parallel", "arbitrary")),
        name="inproj",
    )(starts, h, wt)


def _foxc_kernel(h_ref, w_ref, b_ref, triu_ref, c_ref, carry_ref):
    @pl.when(pl.program_id(0) == 0)
    def _():
        carry_ref[...] = jnp.zeros_like(carry_ref)

    logits = _dot_nt(w_ref[...].astype(BF16), h_ref[...]) + b_ref[...]
    hi, mid, lo = _split3(_log_sigmoid(logits))
    tri = triu_ref[...]
    cum = _dot(hi, tri) + _dot(mid, tri) + _dot(lo, tri) + carry_ref[:, 0:1]
    c_ref[...] = cum
    tm = cum.shape[1]
    carry_ref[...] = jnp.broadcast_to(cum[:, tm - 1:tm], carry_ref.shape)


def _fox_cumlog(h, wt, layer, row0, b_ff, tm=512):
    m, k = h.shape
    n_heads = b_ff.shape[0]
    assert row0 % n_heads == 0
    triu = jnp.asarray(np.triu(np.ones((tm, tm), np.float32)), BF16)
    return pl.pallas_call(
        _foxc_kernel,
        grid=(m // tm,),
        in_specs=[pl.BlockSpec((tm, k), lambda i: (i, 0)),
                  pl.BlockSpec((None, n_heads, k), lambda i: (layer, row0 // n_heads, 0)),
                  pl.BlockSpec((n_heads, 1), lambda i: (0, 0)),
                  pl.BlockSpec((tm, tm), lambda i: (0, 0))],
        out_specs=pl.BlockSpec((n_heads, tm), lambda i: (0, i)),
        out_shape=jax.ShapeDtypeStruct((n_heads, m), F32),
        scratch_shapes=[pltpu.VMEM((n_heads, LANES), F32)],
        compiler_params=_cparams(("arbitrary",)),
        name="fox_cumlog",
    )(h, wt, b_ff.reshape(n_heads, 1), triu)


NEG_BIG = -1e30


LOG2E = 1.0 / math.log(2.0)
ATT_ROWS = 32
ATT_UNROLL = 2


def _fox_attn_kernel(tab_ref, q_ref, k_ref, v_ref, ccol_ref, crow_ref, gate_ref, o_ref,
                     va_ref, sa_ref, sb_ref, pa_ref, pb_ref, ala_ref, alb_ref, acc_ref, m_ref,
                     *, tq, runs):
    @pl.when(pl.program_id(0) == 0)
    def _():
        va_ref[:, HEAD_DIM:2 * HEAD_DIM] = jnp.ones((va_ref.shape[0], HEAD_DIM), BF16)

    va_ref[:, 0:HEAD_DIM] = v_ref[...]
    m_ref[...] = jnp.full(m_ref.shape, NEG_BIG, F32)
    acc_ref[...] = jnp.zeros(acc_ref.shape, F32)

    def tile(n):
        q0 = pl.multiple_of(tab_ref[0, n] * tq, tq)
        k0 = pl.multiple_of(tab_ref[1, n] * tq, tq)
        return q0, k0

    def logits(n, s_ref, tk):
        q0, k0 = tile(n)
        s_ref[:, 0:tk] = _dot_nt(q_ref[pl.ds(q0, tq), :], k_ref[pl.ds(k0, tk), :])

    def softmax(n, s_ref, p_ref, al_ref, masked, tk):
        q0, k0 = tile(n)
        crow = crow_ref[0, :, pl.ds(k0, tk)] * LOG2E
        for r in range(tq // ATT_ROWS):
            rs = slice(r * ATT_ROWS, (r + 1) * ATT_ROWS)
            qs = pl.ds(q0 + r * ATT_ROWS, ATT_ROWS)
            s = s_ref[rs, 0:tk] - crow
            if masked:
                row = lax.broadcasted_iota(jnp.int32, (ATT_ROWS, tk), 0) + r * ATT_ROWS
                col = lax.broadcasted_iota(jnp.int32, (ATT_ROWS, tk), 1)
                s = jnp.where(col <= row, s, NEG_BIG)
            ct = ccol_ref[0, qs, :] * LOG2E
            m_old = m_ref[qs, :]
            m_new = jnp.maximum(m_old, jnp.max(s, axis=1, keepdims=True) + ct)
            p_ref[rs, 0:tk] = jnp.exp2(s - (m_new - ct)).astype(BF16)
            m_ref[qs, :] = m_new
            al_ref[rs, :] = jnp.exp2(m_old - m_new)

    def accumulate(n, p_ref, al_ref, tk):
        q0, k0 = tile(n)
        qs = pl.ds(q0, tq)
        acc_ref[qs, :] = (al_ref[...] * acc_ref[qs, :]
                          + _dot(p_ref[:, 0:tk], va_ref[pl.ds(k0, tk), :]))

    def run(first, count, masked, tk):
        if count == 0:
            return
        last = first + count - 1
        nxt = lambda n: jnp.minimum(n, last)
        s_buf = (sa_ref, sb_ref)
        p_buf = ((pa_ref, ala_ref), (pb_ref, alb_ref))
        logits(first, s_buf[0], tk)
        softmax(first, s_buf[0], *p_buf[0], masked, tk)
        logits(nxt(first + 1), s_buf[1], tk)
        n_loop = (count - 1) // ATT_UNROLL

        def body(j, carry):
            n = first + ATT_UNROLL * j
            for u in range(ATT_UNROLL):
                accumulate(n + u, *p_buf[u % 2], tk)
                softmax(n + u + 1, s_buf[(u + 1) % 2], *p_buf[(u + 1) % 2], masked, tk)
                logits(nxt(n + u + 2), s_buf[u % 2], tk)
            return carry

        lax.fori_loop(0, n_loop, body, 0)
        n = first + ATT_UNROLL * n_loop
        rest = count - 1 - ATT_UNROLL * n_loop
        for u in range(rest + 1):
            accumulate(n + u, *p_buf[u % 2], tk)
            if u + 1 <= rest:
                softmax(n + u + 1, s_buf[(u + 1) % 2], *p_buf[(u + 1) % 2], masked, tk)
            if u + 2 <= rest:
                logits(n + u + 2, s_buf[u % 2], tk)

    for first, count, masked, tk in runs:
        run(first, count, masked, tk)

    acc = acc_ref[...]
    out = acc[:, 0:HEAD_DIM] / acc[:, HEAD_DIM:2 * HEAD_DIM]
    o_ref[...] = (out * _silu(gate_ref[...])).astype(o_ref.dtype)


def _fox_attn(zqkv, z, ccol, crow, g_off, n_heads, tq=512):
    s_len = zqkv.shape[0]
    nq = s_len // tq
    go = g_off // HEAD_DIM
    wide = [(qi, 2 * j) for j in range(nq // 2) for qi in range(2 * j + 2, nq)]
    narrow = [(qi, qi - 1) for qi in range(1, nq, 2)]
    diag = [(i, i) for i in range(nq)]
    tab = jnp.asarray(np.array(wide + narrow + diag, np.int32).reshape(-1, 2).T)
    runs = ((0, len(wide), False, 2 * tq),
            (len(wide), len(narrow), False, tq),
            (len(wide) + len(narrow), len(diag), True, tq))
    kern = functools.partial(_fox_attn_kernel, tq=tq, runs=runs)
    once = pl.Buffered(1)
    head_col = lambda base: pl.BlockSpec((s_len, HEAD_DIM), lambda h, t: (0, base + h),
                                         pipeline_mode=once)
    grid_spec = pltpu.PrefetchScalarGridSpec(
        num_scalar_prefetch=1,
        grid=(n_heads,),
        in_specs=[head_col(0), head_col(n_heads), head_col(2 * n_heads),
                  pl.BlockSpec((1, s_len, 1), lambda h, t: (h, 0, 0), pipeline_mode=once),
                  pl.BlockSpec((1, 1, s_len), lambda h, t: (h, 0, 0)),
                  head_col(go)],
        out_specs=pl.BlockSpec((s_len, HEAD_DIM), lambda h, t: (0, h)),
        scratch_shapes=[pltpu.VMEM((s_len, 2 * HEAD_DIM), BF16),
                        pltpu.VMEM((tq, 2 * tq), F32),
                        pltpu.VMEM((tq, 2 * tq), F32),
                        pltpu.VMEM((tq, 2 * tq), BF16),
                        pltpu.VMEM((tq, 2 * tq), BF16),
                        pltpu.VMEM((tq, 1), F32),
                        pltpu.VMEM((tq, 1), F32),
                        pltpu.VMEM((s_len, 2 * HEAD_DIM), F32),
                        pltpu.VMEM((s_len, 1), F32)])
    return pl.pallas_call(
        kern,
        grid_spec=grid_spec,
        out_shape=jax.ShapeDtypeStruct((s_len, n_heads * HEAD_DIM), BF16),
        compiler_params=_cparams(("arbitrary",)),
        name="fox_attn",
    )(tab, zqkv, zqkv, zqkv, ccol, crow, z)


HG_PROWS = 8 * HG_SUB + 8 * (HG_SUB // 2)


HG_HEADS_PER_STEP = 4


def _hgrn_kernel(q_ref, f_ref, i_ref, gate_ref, lb_ref, ng_ref, tb_ref, ones_ref, mask_ref, o_ref,
                 *scratch, layer, rows):
    @pl.when(pl.program_id(1) == 0)
    def _():
        scratch[0][...] = jnp.zeros_like(scratch[0])

    for hh in range(HG_HEADS_PER_STEP):
        lanes = pl.ds(hh * HEAD_DIM, HEAD_DIM)
        _hgrn_head(*(r.at[:, lanes] for r in (q_ref, f_ref, i_ref, gate_ref, lb_ref, ng_ref)),
                   tb_ref, ones_ref, mask_ref, o_ref.at[:, lanes], *(s.at[hh] for s in scratch),
                   layer=layer, rows=rows)


def _hgrn_head(q_ref, f_ref, i_ref, gate_ref, lb_ref, ng_ref, tb_ref, ones_ref,
               mask_ref, o_ref, st_ref, qt_ref, kt_ref, w_ref, cum_ref, dd_ref, p_ref, sc_ref,
               acc_ref, *, layer, rows):
    z = f_ref[...]
    ls = _log_sigmoid(z)
    if layer == 0:
        g = ls
        logk = ls - z
    else:
        lbp = lb_ref[...]
        e = jnp.exp(lbp - jnp.max(lbp, axis=0, keepdims=True))
        p = e / jnp.sum(e, axis=0, keepdims=True)
        lb = jnp.sum(p[1:layer + 1, :], axis=0, keepdims=True)
        a = jnp.log(lb)
        l1m = jnp.log1p(-lb)
        b = l1m + ls
        g = jnp.maximum(a, b) + jnp.log1p(jnp.exp(-jnp.abs(a - b)))
        logk = l1m + (ls - z)

    cums, tots = [], []
    for r0 in range(0, rows, LANES):
        ct = _dot01(tb_ref[...], g[r0:r0 + LANES, :])
        cums.append(ct[0:LANES, :])
        tots.append(ct[LANES:2 * LANES, :])
    cum = jnp.concatenate(cums, axis=0) * LOG2E
    tot = jnp.concatenate(tots, axis=0) * LOG2E
    w = cum - logk * LOG2E
    qt_ref[...] = (q_ref[...] * jnp.exp2(cum)).astype(BF16)
    kt_ref[...] = jnp.exp2(tot - w).astype(BF16)
    w_ref[...] = w
    cum_ref[...] = cum
    dd_ref[...] = jnp.exp2(tot)

    half = HG_SUB // 2
    n_groups = rows // HG_SUB

    def bcast_row(ref, r):
        return jnp.broadcast_to(ref[r:r + 1, :], (half, HEAD_DIM))

    for g_i in range(n_groups):
        r0 = g_i * HG_SUB
        p0 = g_i * HG_PROWS
        c_lo, c_hi = cum_ref[r0:r0 + half, :], cum_ref[r0 + half:r0 + HG_SUB, :]
        q_lo, q_hi = q_ref[r0:r0 + half, :], q_ref[r0 + half:r0 + HG_SUB, :]
        for s in range(half):
            w_s = bcast_row(w_ref, r0 + s)
            p_lo = q_lo * jnp.exp2(c_lo - w_s + mask_ref[s * half:(s + 1) * half, :])
            p_hi = q_hi * jnp.exp2(c_hi - w_s)
            p_ref[p0 + s * HG_SUB:p0 + (s + 1) * HG_SUB, :] = (
                jnp.concatenate([p_lo, p_hi], axis=0).astype(BF16))
        for s in range(0, half, 2):
            pa = q_hi * jnp.exp2(c_hi - bcast_row(w_ref, r0 + half + s)
                                + mask_ref[s * half:(s + 1) * half, :])
            pb = q_hi * jnp.exp2(c_hi - bcast_row(w_ref, r0 + half + s + 1)
                                + mask_ref[(s + 1) * half:(s + 2) * half, :])
            base = p0 + half * HG_SUB + s * half
            p_ref[base:base + HG_SUB, :] = jnp.concatenate([pa, pb], axis=0).astype(BF16)

    sc_ref[...] = _dot(p_ref[...], ones_ref[...])

    upds = [_dot_tn(i_ref[g_i * HG_SUB:(g_i + 1) * HG_SUB, :].astype(BF16),
                    kt_ref[g_i * HG_SUB:(g_i + 1) * HG_SUB, :]) for g_i in range(n_groups)]
    st = st_ref[...]
    for g_i in range(n_groups):
        r0 = g_i * HG_SUB
        p0 = g_i * HG_PROWS
        o_lo = jnp.zeros((half, HEAD_DIM), F32)
        o_hi = jnp.zeros((half, HEAD_DIM), F32)
        for s in range(half):
            v_s = bcast_row(i_ref, r0 + s)
            o_lo = o_lo + sc_ref[p0 + s * HG_SUB:p0 + s * HG_SUB + half, :] * v_s
            o_hi = o_hi + sc_ref[p0 + s * HG_SUB + half:p0 + (s + 1) * HG_SUB, :] * v_s
        for s in range(half):
            base = p0 + half * HG_SUB + s * half
            o_hi = o_hi + sc_ref[base:base + half, :] * bcast_row(i_ref, r0 + half + s)
        o_inter = _dot_nt(qt_ref[r0:r0 + HG_SUB, :], st.astype(BF16))
        acc_ref[r0:r0 + HG_SUB, :] = o_inter + jnp.concatenate([o_lo, o_hi], axis=0)
        st = st * dd_ref[r0:r0 + 1, :] + upds[g_i]
    st_ref[...] = st

    o = acc_ref[...]
    ms = jnp.mean(o * o, axis=-1, keepdims=True)
    o = o * lax.rsqrt(ms + EPS) * ng_ref[...]
    o_ref[...] = (o * _silu(gate_ref[...])).astype(o_ref.dtype)


def _hgrn(z, hg_lb, norm_g, layer, q_off, f_off, i_off, g_off, n_heads, rows=512):
    s_len = z.shape[0]
    depth = hg_lb.shape[0]
    hps = HG_HEADS_PER_STEP
    width = hps * HEAD_DIM
    assert n_heads % hps == 0 and all(o % width == 0 for o in (q_off, f_off, i_off, g_off))
    qo, fo, io, go = (o // width for o in (q_off, f_off, i_off, g_off))
    r = np.arange(LANES)
    same = (r[:, None] // HG_SUB) == (r[None, :] // HG_SUB)
    tb = jnp.asarray(np.concatenate([same & (r[None, :] <= r[:, None]), same]).astype(np.float32), BF16)
    ones = jnp.ones((HEAD_DIM, HEAD_DIM), BF16)
    half = HG_SUB // 2
    t_idx = np.arange(half)
    mask_np = np.where(t_idx[None, :, None] >= t_idx[:, None, None], 0.0, NEG_BIG)
    mask = jnp.asarray(np.broadcast_to(mask_np, (half, half, HEAD_DIM)).reshape(half * half, HEAD_DIM), F32)
    n_prows = (rows // HG_SUB) * HG_PROWS
    kern = functools.partial(_hgrn_kernel, layer=layer, rows=rows)
    blk_spec = lambda off: pl.BlockSpec((rows, width), lambda h, i: (i, off + h))
    const = lambda shape: pl.BlockSpec(shape, lambda h, i: (0, 0))
    per_head = lambda shape, dtype: pltpu.VMEM((hps,) + shape, dtype)
    return pl.pallas_call(
        kern,
        grid=(n_heads // hps, s_len // rows),
        in_specs=[blk_spec(qo), blk_spec(fo), blk_spec(io), blk_spec(go),
                  pl.BlockSpec((depth, width), lambda h, i: (0, h)),
                  pl.BlockSpec((1, width), lambda h, i: (0, h)),
                  const((2 * LANES, LANES)), const((HEAD_DIM, HEAD_DIM)),
                  const((half * half, HEAD_DIM))],
        out_specs=pl.BlockSpec((rows, width), lambda h, i: (i, h)),
        out_shape=jax.ShapeDtypeStruct((s_len, n_heads * HEAD_DIM), BF16),
        scratch_shapes=[per_head((HEAD_DIM, HEAD_DIM), F32),
                        per_head((rows, HEAD_DIM), BF16),
                        per_head((rows, HEAD_DIM), BF16),
                        per_head((rows, HEAD_DIM), F32),
                        per_head((rows, HEAD_DIM), F32),
                        per_head((rows, HEAD_DIM), F32),
                        per_head((n_prows, HEAD_DIM), BF16),
                        per_head((n_prows, HEAD_DIM), F32),
                        per_head((rows, HEAD_DIM), F32)],
        compiler_params=_cparams(("parallel", "arbitrary")),
        name="hgrn2",
    )(z, z, z, z, hg_lb, norm_g.reshape(1, -1), tb, ones, mask)


def _gelu_tanh(x):
    c = math.sqrt(2.0 / math.pi)
    return 0.5 * x * (1.0 + jnp.tanh(c * (x + 0.044715 * (x * x * x))))


def _s5_kernel(u_ref, perm_ref, permt_ref, bm_ref, cm_ref, are_ref, aim_ref, pw_ref,
               d_ref, o_ref, state_ref, x_ref, yp_ref, *, rows, half):
    @pl.when(pl.program_id(0) == 0)
    def _():
        state_ref[...] = jnp.zeros_like(state_ref)

    nblk = bm_ref.shape[0]
    nt = rows // 8
    u = u_ref[...]
    up = _dot(perm_ref[...], u.astype(BF16)).astype(BF16)
    sub = lax.broadcasted_iota(jnp.int32, (8, half), 0)

    re, im = slice(0, half), slice(half, 2 * half)
    for b in range(nblk):
        x_ref[b] = _dot(up[:, b * LANES:(b + 1) * LANES], bm_ref[b])
    for b in range(nblk):
        are = are_ref[b]
        aim = aim_ref[b]
        xr = jnp.zeros((8, half), F32)
        xi = jnp.zeros((8, half), F32)
        for t in range(nt):
            rs = slice(t * 8, (t + 1) * 8)
            xr, xi = (are * xr - aim * xi + x_ref[b, rs, re],
                      are * xi + aim * xr + x_ref[b, rs, im])
            x_ref[b, rs, re] = xr
            x_ref[b, rs, im] = xi

        er, ei = xr, xi
        alre = pw_ref[b, rows - 1:rows, re]
        alim = pw_ref[b, rows - 1:rows, im]
        cr = state_ref[b, 0:1, re]
        ci = state_ref[b, 0:1, im]
        ctr = jnp.zeros((8, half), F32)
        cti = jnp.zeros((8, half), F32)
        for s in range(8):
            ctr = jnp.where(sub == s, cr, ctr)
            cti = jnp.where(sub == s, ci, cti)
            cr, ci = (alre * cr - alim * ci + er[s:s + 1, :],
                      alre * ci + alim * cr + ei[s:s + 1, :])
        state_ref[b, :, re] = jnp.broadcast_to(cr, (8, half))
        state_ref[b, :, im] = jnp.broadcast_to(ci, (8, half))

        for t in range(nt):
            rs = slice(t * 8, (t + 1) * 8)
            pr = pw_ref[b, rs, re]
            pi = pw_ref[b, rs, im]
            x_ref[b, rs, re] = x_ref[b, rs, re] + (pr * ctr - pi * cti)
            x_ref[b, rs, im] = x_ref[b, rs, im] + (pr * cti + pi * ctr)
        yp_ref[:, b * LANES:(b + 1) * LANES] = _dot(x_ref[b].astype(BF16), cm_ref[b])

    y = _dot01(permt_ref[...], yp_ref[...], terms=2) + d_ref[...] * u
    o_ref[...] = _gelu_tanh(y).astype(o_ref.dtype)


def _s5(z, u_off, bm, cm, are, aim, pw, d_skip, rows=256):
    s_len = z.shape[0]
    width = d_skip.shape[0]
    nblk, _, two_half = bm.shape
    half = two_half // 2
    seg = rows // 8
    rho = np.arange(rows)
    t_of = (rho % 8) * seg + rho // 8
    perm_np = np.zeros((rows, rows), np.float32)
    perm_np[rho, t_of] = 1.0
    perm = jnp.asarray(perm_np, BF16)
    permt = jnp.asarray(perm_np.T, BF16)
    uo = u_off // width
    kern = functools.partial(_s5_kernel, rows=rows, half=half)
    c2 = lambda shape: pl.BlockSpec(shape, lambda i: (0, 0))
    c3 = lambda shape: pl.BlockSpec(shape, lambda i: (0, 0, 0))
    return pl.pallas_call(
        kern,
        grid=(s_len // rows,),
        in_specs=[pl.BlockSpec((rows, width), lambda i: (i, uo)),
                  c2((rows, rows)), c2((rows, rows)),
                  c3(bm.shape), c3(cm.shape),
                  c3(are.shape), c3(aim.shape),
                  pl.BlockSpec(pw.shape, lambda i: (0, 0, 0), pipeline_mode=pl.Buffered(1)),
                  c2((1, width))],
        out_specs=pl.BlockSpec((rows, width), lambda i: (i, 0)),
        out_shape=jax.ShapeDtypeStruct((s_len, width), BF16),
        scratch_shapes=[pltpu.VMEM((nblk, 8, two_half), F32),
                        pltpu.VMEM((nblk, rows, two_half), F32),
                        pltpu.VMEM((rows, width), F32)],
        compiler_params=_cparams(("arbitrary",)),
        name="s5",
    )(z, perm, permt, bm, cm, are, aim, pw, d_skip.reshape(1, width))


def _s5_params(a_re, a_im, log_dt, b_re, b_im, c_re, c_im, seg):
    g_n, p_n = a_re.shape
    nblk = g_n // S5_GPB
    dt = jnp.exp(log_dt)[:, None]
    mag = jnp.exp(a_re * dt)
    ang = a_im * dt
    abar_re = mag * jnp.cos(ang)
    abar_im = mag * jnp.sin(ang)
    nr = abar_re - 1.0
    den = a_re * a_re + a_im * a_im
    zr = (nr * a_re + abar_im * a_im) / den
    zi = (abar_im * a_re - nr * a_im) / den
    bbar_re = zr[:, :, None] * b_re - zi[:, :, None] * b_im
    bbar_im = zr[:, :, None] * b_im + zi[:, :, None] * b_re
    same = jnp.asarray(np.eye(S5_GPB, dtype=bool))

    def embed_b(bb):
        bb = bb.reshape(nblk, S5_GPB, 1, p_n, S5_GROUP).transpose(0, 1, 4, 2, 3)
        m = jnp.where(same[None, :, None, :, None], bb, 0.0)
        return m.reshape(nblk, S5_GPB * S5_GROUP, S5_GPB * p_n)

    bm = jnp.concatenate([embed_b(bbar_re), embed_b(bbar_im)], axis=2).astype(BF16)

    def embed_c(cc):
        cc = cc.reshape(nblk, S5_GPB, S5_GROUP, 1, p_n).transpose(0, 1, 4, 3, 2)
        m = jnp.where(same[None, :, None, :, None], cc, 0.0)
        return m.reshape(nblk, S5_GPB * p_n, S5_GPB * S5_GROUP)

    cm = jnp.concatenate([embed_c(c_re), embed_c(-c_im)], axis=1).astype(BF16)

    def tile8(v):
        v = v.reshape(nblk, 1, S5_GPB * p_n)
        return jnp.broadcast_to(v, (nblk, 8, S5_GPB * p_n))

    k = jnp.arange(1, seg + 1, dtype=F32)[:, None, None]
    mag_k = jnp.exp(k * (a_re * dt))
    ang_k = k * ang

    def table(t):
        t = t.reshape(seg, 1, nblk, S5_GPB * p_n)
        return jnp.broadcast_to(t, (seg, 8, nblk, S5_GPB * p_n)).transpose(2, 0, 1, 3).reshape(
            nblk, 8 * seg, S5_GPB * p_n)

    pw = jnp.concatenate([table(mag_k * jnp.cos(ang_k)), table(mag_k * jnp.sin(ang_k))], axis=2)
    return bm, cm, tile8(abar_re), tile8(abar_im), pw


def _glu_kernel(y_ref, w_ref, gate_ref, o_ref, *, width):
    zg = _dot(y_ref[...], w_ref[...])
    o = zg[:, :width] * _sigmoid(zg[:, width:]) * _silu(gate_ref[...])
    o_ref[...] = o.astype(o_ref.dtype)


def _glu(y, w_glu, layer, z, g_off, tm=512):
    m, width = y.shape
    go = g_off // width
    return pl.pallas_call(
        functools.partial(_glu_kernel, width=width),
        grid=(m // tm,),
        in_specs=[pl.BlockSpec((tm, width), lambda i: (i, 0)),
                  pl.BlockSpec((None, width, 2 * width), lambda i: (layer, 0, 0)),
                  pl.BlockSpec((tm, width), lambda i: (i, go))],
        out_specs=pl.BlockSpec((tm, width), lambda i: (i, 0)),
        out_shape=jax.ShapeDtypeStruct((m, width), BF16),
        compiler_params=_cparams(("parallel",)),
        name="s5_glu",
    )(y, w_glu, z)


def _merge_kernel(oa_ref, ob_ref, oc_ref, ga_ref, gb_ref, gc_ref, bg_ref, wa_ref, wb_ref, wc_ref,
                  wo_ref, x_ref, ng_ref, xo_ref, ho_ref, *, d):
    def gate(g_ref, i):
        return _sigmoid(g_ref[...] + bg_ref[:, i * d:(i + 1) * d])

    merged = gate(ga_ref, 0) * _dot(oa_ref[...], wa_ref[...])
    merged = merged + gate(gb_ref, 1) * _dot(ob_ref[...], wb_ref[...])
    merged = merged + gate(gc_ref, 2) * _dot(oc_ref[...], wc_ref[...])
    xn = x_ref[...] + _dot(merged.astype(BF16), wo_ref[...])
    xo_ref[...] = xn
    ms = jnp.mean(xn * xn, axis=-1, keepdims=True)
    ho_ref[...] = (xn * lax.rsqrt(ms + EPS) * ng_ref[...]).astype(ho_ref.dtype)


def _merge(oa, ob, oc, z, mg_off, b_gate, wa, wb, wc, wo, layer, x, next_g, h_dtype, tm=256):
    m, d = x.shape
    w = oa.shape[1]
    assert mg_off % d == 0
    mo = mg_off // d
    row = lambda width: pl.BlockSpec((tm, width), lambda i: (i, 0))
    gate = lambda k: pl.BlockSpec((tm, d), lambda i: (i, mo + k))
    resident = lambda shape: pl.BlockSpec(shape, lambda i: (0, 0), pipeline_mode=pl.Buffered(1))
    weight = lambda rows: pl.BlockSpec((None, rows, d), lambda i: (layer, 0, 0),
                                       pipeline_mode=pl.Buffered(1))
    return pl.pallas_call(
        functools.partial(_merge_kernel, d=d),
        grid=(m // tm,),
        in_specs=[row(w), row(w), row(w), gate(0), gate(1), gate(2),
                  resident((1, N_BRANCH * d)),
                  weight(w), weight(w), weight(w), weight(d),
                  row(d), resident((1, d))],
        out_specs=[row(d), row(d)],
        out_shape=[jax.ShapeDtypeStruct((m, d), F32), jax.ShapeDtypeStruct((m, d), h_dtype)],
        compiler_params=_cparams(("parallel",)),
        name="merge_out",
    )(oa, ob, oc, z, z, z, b_gate.reshape(1, -1), wa, wb, wc, wo, x, next_g.reshape(1, -1))


def kernel(x, norm_g, w_in, b_gate, fox_bf, hg_lb, hg_norm_g, s5_a_re, s5_a_im, s5_log_dt,
           s5_b_re, s5_b_im, s5_c_re, s5_c_im, s5_d, s5_w_glu, w_br_a, w_br_b, w_br_c, w_out,
           final_g):
    bsz, s_len, d = x.shape
    depth = w_in.shape[0]
    hg_w = hg_lb.shape[1]
    s5_w = s5_d.shape[1]
    n_fox = fox_bf.shape[1]
    fox_w = n_fox * HEAD_DIM
    n_hg = hg_w // HEAD_DIM
    sizes = (hg_w, hg_w, hg_w, hg_w, s5_w, s5_w, fox_w, fox_w, fox_w, n_fox, fox_w, N_BRANCH * d)
    offs = np.concatenate([[0], np.cumsum(sizes)])
    (o_hq, o_hf, o_hi, o_hg, o_su, o_sg, o_fq, o_fk, o_fv, o_ff, o_fg, o_mg, o_end) = (int(v) for v in offs)
    wt = jnp.transpose(w_in, (0, 2, 1))
    tn = fox_w
    tiles = lambda start, stop: list(range(start, stop, tn))
    o_mg2, o_fg2 = 0, o_end - o_mg
    w_glu, w_a, w_b, w_c, w_o = (w.astype(BF16) for w in (s5_w_glu, w_br_a, w_br_b, w_br_c, w_out))
    s5_rows = 256
    outs = []
    for b in range(bsz):
        xb = x[b]
        h = _rmsnorm(xb, norm_g[0], BF16)
        for l in range(depth):
            z = _inproj(h, wt, l, tiles(0, o_fq), F32, tn=tn)
            zqkv = _inproj(h, wt, l, tiles(o_fq, o_ff), BF16,
                           first_tile_scale=HEAD_DIM ** -0.5 * LOG2E, tn=tn)
            zg = _inproj(h, wt, l, tiles(o_mg, o_end) + tiles(o_fg, o_mg), F32, tn=tn)

            o_a = _hgrn(z, hg_lb, hg_norm_g[l], l, o_hq, o_hf, o_hi, o_hg, n_hg)

            s5p = _s5_params(s5_a_re[l], s5_a_im[l], s5_log_dt[l], s5_b_re[l], s5_b_im[l],
                             s5_c_re[l], s5_c_im[l], s5_rows // 8)
            y_b = _s5(z, o_su, *s5p, s5_d[l], rows=s5_rows)
            o_b = _glu(y_b, w_glu, l, z, o_sg)

            ct = _fox_cumlog(h, wt, l, o_ff, fox_bf[l])
            o_c = _fox_attn(zqkv, zg, ct[:, :, None], ct[:, None, :], o_fg2, n_fox)

            last = l == depth - 1
            next_g = final_g if last else norm_g[l + 1]
            xb, h = _merge(o_a, o_b, o_c, zg, o_mg2, b_gate[l], w_a, w_b, w_c, w_o, l,
                           xb, next_g, F32 if last else BF16)
        outs.append(h)
    return outs[0][None] if bsz == 1 else jnp.stack(outs, axis=0)
```

```python
import functools
import math

import numpy as np
import jax
import jax.numpy as jnp
from jax import lax
from jax.experimental import pallas as pl
from jax.experimental.pallas import tpu as pltpu

F32 = jnp.float32
BF16 = jnp.bfloat16

EPS = 1e-6
LANES = 128
VMEM_LIMIT = 56 * 1024 * 1024

HEAD_DIM = 128
S5_GROUP = 16
S5_STATE = 64
S5_GPB = 8
N_BRANCH = 3

HG_SUB = 16


def _cparams(sem, vmem=VMEM_LIMIT):
    return pltpu.CompilerParams(dimension_semantics=sem, vmem_limit_bytes=vmem)


def _dot(a, b):
    return jnp.dot(a, b, preferred_element_type=F32)


def _dot_nt(a, b):
    return lax.dot_general(a, b, (((1,), (1,)), ((), ())), preferred_element_type=F32)


def _dot_tn(a, b):
    return lax.dot_general(a, b, (((0,), (0,)), ((), ())), preferred_element_type=F32)


def _split3(x):
    hi = x.astype(BF16)
    r1 = x - hi.astype(F32)
    mid = r1.astype(BF16)
    lo = (r1 - mid.astype(F32)).astype(BF16)
    return hi, mid, lo


def _dot01(m, x, terms=3):
    parts = _split3(x)[:terms]
    out = _dot(m, parts[0])
    for part in parts[1:]:
        out = out + _dot(m, part)
    return out


def _log_sigmoid(z):
    return jnp.minimum(z, 0.0) - jnp.log1p(jnp.exp(-jnp.abs(z)))


def _sigmoid(z):
    return 1.0 / (1.0 + jnp.exp(-z))


def _silu(z):
    return z * _sigmoid(z)


def _rmsnorm_kernel(x_ref, g_ref, o_ref):
    x = x_ref[...]
    ms = jnp.mean(x * x, axis=-1, keepdims=True)
    o_ref[...] = (x * lax.rsqrt(ms + EPS) * g_ref[...]).astype(o_ref.dtype)


def _rmsnorm(x, g, out_dtype, tm=512):
    m, d = x.shape
    return pl.pallas_call(
        _rmsnorm_kernel,
        grid=(m // tm,),
        in_specs=[pl.BlockSpec((tm, d), lambda i: (i, 0)),
                  pl.BlockSpec((1, d), lambda i: (0, 0))],
        out_specs=pl.BlockSpec((tm, d), lambda i: (i, 0)),
        out_shape=jax.ShapeDtypeStruct((m, d), out_dtype),
        compiler_params=_cparams(("parallel",)),
        name="rmsnorm",
    )(x, g.reshape(1, d))


def _inproj_kernel(starts_ref, h_ref, w_ref, o_ref, wb_ref, *, first_tile_scale):
    del starts_ref
    @pl.when(pl.program_id(1) == 0)
    def _():
        wb_ref[...] = w_ref[0].T.astype(BF16)

    acc = _dot(h_ref[...], wb_ref[...])
    if first_tile_scale is not None:
        acc = acc * jnp.where(pl.program_id(0) == 0, first_tile_scale, 1.0)
    o_ref[...] = acc.astype(o_ref.dtype)


def _inproj(h, wt, layer, row_starts, out_dtype, first_tile_scale=None, tm=1024, tn=1024):
    m, k = h.shape
    n_tiles = len(row_starts)
    assert all(r % 8 == 0 for r in row_starts)
    starts = jnp.asarray(np.asarray(row_starts, np.int32) // 8)
    grid_spec = pltpu.PrefetchScalarGridSpec(
        num_scalar_prefetch=1,
        grid=(n_tiles, m // tm),
        in_specs=[pl.BlockSpec((tm, k), lambda j, i, st: (i, 0)),
                  pl.BlockSpec((pl.Element(1), pl.Element(tn), pl.Element(k)),
                               lambda j, i, st: (layer, st[j] * 8, 0))],
        out_specs=pl.BlockSpec((tm, tn), lambda j, i, st: (i, j)),
        scratch_shapes=[pltpu.VMEM((k, tn), BF16)])
    return pl.pallas_call(
        functools.partial(_inproj_kernel, first_tile_scale=first_tile_scale),
        grid_spec=grid_spec,
        out_shape=jax.ShapeDtypeStruct((m, n_tiles * tn), out_dtype),
        compiler_params=_cparams(("parallel", "arbitrary")),
        name="inproj",
    )(starts, h, wt)


def _foxc_kernel(h_ref, w_ref, b_ref, triu_ref, c_ref, carry_ref):
    @pl.when(pl.program_id(0) == 0)
    def _():
        carry_ref[...] = jnp.zeros_like(carry_ref)

    logits = _dot_nt(w_ref[...].astype(BF16), h_ref[...]) + b_ref[...]
    hi, mid, lo = _split3(_log_sigmoid(logits))
    tri = triu_ref[...]
    cum = _dot(hi, tri) + _dot(mid, tri) + _dot(lo, tri) + carry_ref[:, 0:1]
    c_ref[...] = cum
    tm = cum.shape[1]
    carry_ref[...] = jnp.broadcast_to(cum[:, tm - 1:tm], carry_ref.shape)


def _fox_cumlog(h, wt, layer, row0, b_ff, tm=512):
    m, k = h.shape
    n_heads = b_ff.shape[0]
    assert row0 % n_heads == 0
    triu = jnp.asarray(np.triu(np.ones((tm, tm), np.float32)), BF16)
    return pl.pallas_call(
        _foxc_kernel,
        grid=(m // tm,),
        in_specs=[pl.BlockSpec((tm, k), lambda i: (i, 0)),
                  pl.BlockSpec((None, n_heads, k), lambda i: (layer, row0 // n_heads, 0)),
                  pl.BlockSpec((n_heads, 1), lambda i: (0, 0)),
                  pl.BlockSpec((tm, tm), lambda i: (0, 0))],
        out_specs=pl.BlockSpec((n_heads, tm), lambda i: (0, i)),
        out_shape=jax.ShapeDtypeStruct((n_heads, m), F32),
        scratch_shapes=[pltpu.VMEM((n_heads, LANES), F32)],
        compiler_params=_cparams(("arbitrary",)),
        name="fox_cumlog",
    )(h, wt, b_ff.reshape(n_heads, 1), triu)


NEG_BIG = -1e30


LOG2E = 1.0 / math.log(2.0)
ATT_PASS_ELEMS = 16 * 1024
ATT_WIDTHS = (3, 2, 1)
ATT_VMEM_LIMIT = 60 * 1024 * 1024
ATT_UNROLL = 2


def _fox_attn_kernel(tab_ref, q_ref, k_ref, v_ref, ccol_ref, crow_ref, gate_ref, o_ref,
                     va_ref, sa_ref, sb_ref, pa_ref, pb_ref, ala_ref, alb_ref, acc_ref, m_ref,
                     *, tq, runs):
    @pl.when(pl.program_id(0) == 0)
    def _():
        va_ref[:, HEAD_DIM:2 * HEAD_DIM] = jnp.ones((va_ref.shape[0], HEAD_DIM), BF16)

    va_ref[:, 0:HEAD_DIM] = v_ref[...]
    m_ref[...] = jnp.full(m_ref.shape, NEG_BIG, F32)
    acc_ref[...] = jnp.zeros(acc_ref.shape, F32)

    def tile(n):
        q0 = pl.multiple_of(tab_ref[0, n] * tq, tq)
        k0 = pl.multiple_of(tab_ref[1, n] * tq, tq)
        return q0, k0

    def logits(n, s_ref, tk):
        q0, k0 = tile(n)
        s_ref[:, 0:tk] = _dot_nt(q_ref[pl.ds(q0, tq), :], k_ref[pl.ds(k0, tk), :])

    def softmax(n, s_ref, p_ref, al_ref, masked, tk):
        q0, k0 = tile(n)
        crow = crow_ref[0, :, pl.ds(k0, tk)] * LOG2E
        rows = max(16, ATT_PASS_ELEMS // tk)
        for r in range(tq // rows):
            rs = slice(r * rows, (r + 1) * rows)
            qs = pl.ds(q0 + r * rows, rows)
            s = s_ref[rs, 0:tk] - crow
            if masked:
                row = lax.broadcasted_iota(jnp.int32, (rows, tk), 0) + r * rows
                col = lax.broadcasted_iota(jnp.int32, (rows, tk), 1)
                s = jnp.where(col <= row, s, NEG_BIG)
            ct = ccol_ref[0, qs, :] * LOG2E
            m_old = m_ref[qs, :]
            m_new = jnp.maximum(m_old, jnp.max(s, axis=1, keepdims=True) + ct)
            p_ref[rs, 0:tk] = jnp.exp2(s - (m_new - ct)).astype(BF16)
            m_ref[qs, :] = m_new
            al_ref[rs, :] = jnp.exp2(m_old - m_new)

    def accumulate(n, p_ref, al_ref, tk):
        q0, k0 = tile(n)
        qs = pl.ds(q0, tq)
        acc_ref[qs, :] = (al_ref[...] * acc_ref[qs, :]
                          + _dot(p_ref[:, 0:tk], va_ref[pl.ds(k0, tk), :]))

    def run(first, count, masked, tk):
        if count == 0:
            return
        last = first + count - 1
        nxt = lambda n: jnp.minimum(n, last)
        s_buf = (sa_ref, sb_ref)
        p_buf = ((pa_ref, ala_ref), (pb_ref, alb_ref))
        logits(first, s_buf[0], tk)
        softmax(first, s_buf[0], *p_buf[0], masked, tk)
        logits(nxt(first + 1), s_buf[1], tk)
        n_loop = (count - 1) // ATT_UNROLL

        def body(j, carry):
            n = first + ATT_UNROLL * j
            for u in range(ATT_UNROLL):
                accumulate(n + u, *p_buf[u % 2], tk)
                softmax(n + u + 1, s_buf[(u + 1) % 2], *p_buf[(u + 1) % 2], masked, tk)
                logits(nxt(n + u + 2), s_buf[u % 2], tk)
            return carry

        lax.fori_loop(0, n_loop, body, 0)
        n = first + ATT_UNROLL * n_loop
        rest = count - 1 - ATT_UNROLL * n_loop
        for u in range(rest + 1):
            accumulate(n + u, *p_buf[u % 2], tk)
            if u + 1 <= rest:
                softmax(n + u + 1, s_buf[(u + 1) % 2], *p_buf[(u + 1) % 2], masked, tk)
            if u + 2 <= rest:
                logits(n + u + 2, s_buf[u % 2], tk)

    for first, count, masked, tk in runs:
        run(first, count, masked, tk)

    acc = acc_ref[...]
    out = acc[:, 0:HEAD_DIM] / acc[:, HEAD_DIM:2 * HEAD_DIM]
    o_ref[...] = (out * _silu(gate_ref[...])).astype(o_ref.dtype)


def _fox_attn(zqkv, z, ccol, crow, g_off, n_heads, tq=512):
    s_len = zqkv.shape[0]
    nq = s_len // tq
    go = g_off // HEAD_DIM
    tiles, runs = [], []
    for w in ATT_WIDTHS:
        cls = []
        for qi in range(nq):
            k = 0
            for w2 in ATT_WIDTHS:
                n_w2 = (qi - k) // w2
                if w2 == w:
                    cls += [(qi, k + i * w) for i in range(n_w2)]
                k += n_w2 * w2
        cls.sort(key=lambda t: (t[1], t[0]))
        runs.append((len(tiles), len(cls), False, w * tq))
        tiles += cls
    runs.append((len(tiles), nq, True, tq))
    tiles += [(i, i) for i in range(nq)]
    tab = jnp.asarray(np.array(tiles, np.int32).reshape(-1, 2).T)
    wmax = max(ATT_WIDTHS) * tq
    kern = functools.partial(_fox_attn_kernel, tq=tq, runs=tuple(runs))
    once = pl.Buffered(1)
    head_col = lambda base: pl.BlockSpec((s_len, HEAD_DIM), lambda h, t: (0, base + h),
                                         pipeline_mode=once)
    grid_spec = pltpu.PrefetchScalarGridSpec(
        num_scalar_prefetch=1,
        grid=(n_heads,),
        in_specs=[head_col(0), head_col(n_heads), head_col(2 * n_heads),
                  pl.BlockSpec((1, s_len, 1), lambda h, t: (h, 0, 0), pipeline_mode=once),
                  pl.BlockSpec((1, 1, s_len), lambda h, t: (h, 0, 0)),
                  head_col(go)],
        out_specs=pl.BlockSpec((s_len, HEAD_DIM), lambda h, t: (0, h), pipeline_mode=once),
        scratch_shapes=[pltpu.VMEM((s_len, 2 * HEAD_DIM), BF16),
                        pltpu.VMEM((tq, wmax), F32),
                        pltpu.VMEM((tq, wmax), F32),
                        pltpu.VMEM((tq, wmax), BF16),
                        pltpu.VMEM((tq, wmax), BF16),
                        pltpu.VMEM((tq, 1), F32),
                        pltpu.VMEM((tq, 1), F32),
                        pltpu.VMEM((s_len, 2 * HEAD_DIM), F32),
                        pltpu.VMEM((s_len, 1), F32)])
    return pl.pallas_call(
        kern,
        grid_spec=grid_spec,
        out_shape=jax.ShapeDtypeStruct((s_len, n_heads * HEAD_DIM), BF16),
        compiler_params=_cparams(("arbitrary",), vmem=ATT_VMEM_LIMIT),
        name="fox_attn",
    )(tab, zqkv, zqkv, zqkv, ccol, crow, z)


HG_PROWS = 8 * HG_SUB + 8 * (HG_SUB // 2)


HG_HEADS_PER_STEP = 4


def _hgrn_kernel(q_ref, f_ref, i_ref, gate_ref, lb_ref, ng_ref, tb_ref, ones_ref, mask_ref, o_ref,
                 *scratch, layer, rows):
    @pl.when(pl.program_id(1) == 0)
    def _():
        scratch[0][...] = jnp.zeros_like(scratch[0])

    for hh in range(HG_HEADS_PER_STEP):
        lanes = pl.ds(hh * HEAD_DIM, HEAD_DIM)
        _hgrn_head(*(r.at[:, lanes] for r in (q_ref, f_ref, i_ref, gate_ref, lb_ref, ng_ref)),
                   tb_ref, ones_ref, mask_ref, o_ref.at[:, lanes], *(s.at[hh] for s in scratch),
                   layer=layer, rows=rows)


def _hgrn_head(q_ref, f_ref, i_ref, gate_ref, lb_ref, ng_ref, tb_ref, ones_ref,
               mask_ref, o_ref, st_ref, qt_ref, kt_ref, w_ref, cum_ref, dd_ref, p_ref, sc_ref,
               acc_ref, *, layer, rows):
    z = f_ref[...]
    ls = _log_sigmoid(z)
    if layer == 0:
        g = ls
        logk = ls - z
    else:
        lbp = lb_ref[...]
        e = jnp.exp(lbp - jnp.max(lbp, axis=0, keepdims=True))
        p = e / jnp.sum(e, axis=0, keepdims=True)
        lb = jnp.sum(p[1:layer + 1, :], axis=0, keepdims=True)
        a = jnp.log(lb)
        l1m = jnp.log1p(-lb)
        b = l1m + ls
        g = jnp.maximum(a, b) + jnp.log1p(jnp.exp(-jnp.abs(a - b)))
        logk = l1m + (ls - z)

    cums, tots = [], []
    for r0 in range(0, rows, LANES):
        ct = _dot01(tb_ref[...], g[r0:r0 + LANES, :])
        cums.append(ct[0:LANES, :])
        tots.append(ct[LANES:2 * LANES, :])
    cum = jnp.concatenate(cums, axis=0) * LOG2E
    tot = jnp.concatenate(tots, axis=0) * LOG2E
    w = cum - logk * LOG2E
    qt_ref[...] = (q_ref[...] * jnp.exp2(cum)).astype(BF16)
    kt_ref[...] = jnp.exp2(tot - w).astype(BF16)
    w_ref[...] = w
    cum_ref[...] = cum
    dd_ref[...] = jnp.exp2(tot)

    half = HG_SUB // 2
    n_groups = rows // HG_SUB

    def bcast_row(ref, r):
        return jnp.broadcast_to(ref[r:r + 1, :], (half, HEAD_DIM))

    for g_i in range(n_groups):
        r0 = g_i * HG_SUB
        p0 = g_i * HG_PROWS
        c_lo, c_hi = cum_ref[r0:r0 + half, :], cum_ref[r0 + half:r0 + HG_SUB, :]
        q_lo, q_hi = q_ref[r0:r0 + half, :], q_ref[r0 + half:r0 + HG_SUB, :]
        for s in range(half):
            w_s = bcast_row(w_ref, r0 + s)
            p_lo = q_lo * jnp.exp2(c_lo - w_s + mask_ref[s * half:(s + 1) * half, :])
            p_hi = q_hi * jnp.exp2(c_hi - w_s)
            p_ref[p0 + s * HG_SUB:p0 + (s + 1) * HG_SUB, :] = (
                jnp.concatenate([p_lo, p_hi], axis=0).astype(BF16))
        for s in range(0, half, 2):
            pa = q_hi * jnp.exp2(c_hi - bcast_row(w_ref, r0 + half + s)
                                + mask_ref[s * half:(s + 1) * half, :])
            pb = q_hi * jnp.exp2(c_hi - bcast_row(w_ref, r0 + half + s + 1)
                                + mask_ref[(s + 1) * half:(s + 2) * half, :])
            base = p0 + half * HG_SUB + s * half
            p_ref[base:base + HG_SUB, :] = jnp.concatenate([pa, pb], axis=0).astype(BF16)

    sc_ref[...] = _dot(p_ref[...], ones_ref[...])

    upds = [_dot_tn(i_ref[g_i * HG_SUB:(g_i + 1) * HG_SUB, :].astype(BF16),
                    kt_ref[g_i * HG_SUB:(g_i + 1) * HG_SUB, :]) for g_i in range(n_groups)]
    st = st_ref[...]
    for g_i in range(n_groups):
        r0 = g_i * HG_SUB
        p0 = g_i * HG_PROWS
        o_lo = jnp.zeros((half, HEAD_DIM), F32)
        o_hi = jnp.zeros((half, HEAD_DIM), F32)
        for s in range(half):
            v_s = bcast_row(i_ref, r0 + s)
            o_lo = o_lo + sc_ref[p0 + s * HG_SUB:p0 + s * HG_SUB + half, :] * v_s
            o_hi = o_hi + sc_ref[p0 + s * HG_SUB + half:p0 + (s + 1) * HG_SUB, :] * v_s
        for s in range(half):
            base = p0 + half * HG_SUB + s * half
            o_hi = o_hi + sc_ref[base:base + half, :] * bcast_row(i_ref, r0 + half + s)
        o_inter = _dot_nt(qt_ref[r0:r0 + HG_SUB, :], st.astype(BF16))
        acc_ref[r0:r0 + HG_SUB, :] = o_inter + jnp.concatenate([o_lo, o_hi], axis=0)
        st = st * dd_ref[r0:r0 + 1, :] + upds[g_i]
    st_ref[...] = st

    o = acc_ref[...]
    ms = jnp.mean(o * o, axis=-1, keepdims=True)
    o = o * lax.rsqrt(ms + EPS) * ng_ref[...]
    o_ref[...] = (o * _silu(gate_ref[...])).astype(o_ref.dtype)


def _hgrn(z, hg_lb, norm_g, layer, q_off, f_off, i_off, g_off, n_heads, rows=512):
    s_len = z.shape[0]
    depth = hg_lb.shape[0]
    hps = HG_HEADS_PER_STEP
    width = hps * HEAD_DIM
    assert n_heads % hps == 0 and all(o % width == 0 for o in (q_off, f_off, i_off, g_off))
    qo, fo, io, go = (o // width for o in (q_off, f_off, i_off, g_off))
    r = np.arange(LANES)
    same = (r[:, None] // HG_SUB) == (r[None, :] // HG_SUB)
    tb = jnp.asarray(np.concatenate([same & (r[None, :] <= r[:, None]), same]).astype(np.float32), BF16)
    ones = jnp.ones((HEAD_DIM, HEAD_DIM), BF16)
    half = HG_SUB // 2
    t_idx = np.arange(half)
    mask_np = np.where(t_idx[None, :, None] >= t_idx[:, None, None], 0.0, NEG_BIG)
    mask = jnp.asarray(np.broadcast_to(mask_np, (half, half, HEAD_DIM)).reshape(half * half, HEAD_DIM), F32)
    n_prows = (rows // HG_SUB) * HG_PROWS
    kern = functools.partial(_hgrn_kernel, layer=layer, rows=rows)
    blk_spec = lambda off: pl.BlockSpec((rows, width), lambda h, i: (i, off + h))
    const = lambda shape: pl.BlockSpec(shape, lambda h, i: (0, 0))
    per_head = lambda shape, dtype: pltpu.VMEM((hps,) + shape, dtype)
    return pl.pallas_call(
        kern,
        grid=(n_heads // hps, s_len // rows),
        in_specs=[blk_spec(qo), blk_spec(fo), blk_spec(io), blk_spec(go),
                  pl.BlockSpec((depth, width), lambda h, i: (0, h)),
                  pl.BlockSpec((1, width), lambda h, i: (0, h)),
                  const((2 * LANES, LANES)), const((HEAD_DIM, HEAD_DIM)),
                  const((half * half, HEAD_DIM))],
        out_specs=pl.BlockSpec((rows, width), lambda h, i: (i, h)),
        out_shape=jax.ShapeDtypeStruct((s_len, n_heads * HEAD_DIM), BF16),
        scratch_shapes=[per_head((HEAD_DIM, HEAD_DIM), F32),
                        per_head((rows, HEAD_DIM), BF16),
                        per_head((rows, HEAD_DIM), BF16),
                        per_head((rows, HEAD_DIM), F32),
                        per_head((rows, HEAD_DIM), F32),
                        per_head((rows, HEAD_DIM), F32),
                        per_head((n_prows, HEAD_DIM), BF16),
                        per_head((n_prows, HEAD_DIM), F32),
                        per_head((rows, HEAD_DIM), F32)],
        compiler_params=_cparams(("parallel", "arbitrary")),
        name="hgrn2",
    )(z, z, z, z, hg_lb, norm_g.reshape(1, -1), tb, ones, mask)


def _gelu_tanh(x):
    c = math.sqrt(2.0 / math.pi)
    return 0.5 * x * (1.0 + jnp.tanh(c * (x + 0.044715 * (x * x * x))))


def _s5_kernel(u_ref, perm_ref, permt_ref, bm_ref, cm_ref, are_ref, aim_ref, pw_ref,
               d_ref, o_ref, state_ref, x_ref, yp_ref, *, rows, half):
    @pl.when(pl.program_id(0) == 0)
    def _():
        state_ref[...] = jnp.zeros_like(state_ref)

    nblk = bm_ref.shape[0]
    nt = rows // 8
    u = u_ref[...]
    up = _dot(perm_ref[...], u.astype(BF16)).astype(BF16)
    sub = lax.broadcasted_iota(jnp.int32, (8, half), 0)

    re, im = slice(0, half), slice(half, 2 * half)
    for b in range(nblk):
        x_ref[b] = _dot(up[:, b * LANES:(b + 1) * LANES], bm_ref[b])
    for b in range(nblk):
        are = are_ref[b]
        aim = aim_ref[b]
        xr = jnp.zeros((8, half), F32)
        xi = jnp.zeros((8, half), F32)
        for t in range(nt):
            rs = slice(t * 8, (t + 1) * 8)
            xr, xi = (are * xr - aim * xi + x_ref[b, rs, re],
                      are * xi + aim * xr + x_ref[b, rs, im])
            x_ref[b, rs, re] = xr
            x_ref[b, rs, im] = xi

        er, ei = xr, xi
        alre = pw_ref[b, rows - 1:rows, re]
        alim = pw_ref[b, rows - 1:rows, im]
        cr = state_ref[b, 0:1, re]
        ci = state_ref[b, 0:1, im]
        ctr = jnp.zeros((8, half), F32)
        cti = jnp.zeros((8, half), F32)
        for s in range(8):
            ctr = jnp.where(sub == s, cr, ctr)
            cti = jnp.where(sub == s, ci, cti)
            cr, ci = (alre * cr - alim * ci + er[s:s + 1, :],
                      alre * ci + alim * cr + ei[s:s + 1, :])
        state_ref[b, :, re] = jnp.broadcast_to(cr, (8, half))
        state_ref[b, :, im] = jnp.broadcast_to(ci, (8, half))

        for t in range(nt):
            rs = slice(t * 8, (t + 1) * 8)
            pr = pw_ref[b, rs, re]
            pi = pw_ref[b, rs, im]
            x_ref[b, rs, re] = x_ref[b, rs, re] + (pr * ctr - pi * cti)
            x_ref[b, rs, im] = x_ref[b, rs, im] + (pr * cti + pi * ctr)
        yp_ref[:, b * LANES:(b + 1) * LANES] = _dot(x_ref[b].astype(BF16), cm_ref[b])

    y = _dot01(permt_ref[...], yp_ref[...], terms=2) + d_ref[...] * u
    o_ref[...] = _gelu_tanh(y).astype(o_ref.dtype)


def _s5(z, u_off, bm, cm, are, aim, pw, d_skip, rows=256):
    s_len = z.shape[0]
    width = d_skip.shape[0]
    nblk, _, two_half = bm.shape
    half = two_half // 2
    seg = rows // 8
    rho = np.arange(rows)
    t_of = (rho % 8) * seg + rho // 8
    perm_np = np.zeros((rows, rows), np.float32)
    perm_np[rho, t_of] = 1.0
    perm = jnp.asarray(perm_np, BF16)
    permt = jnp.asarray(perm_np.T, BF16)
    uo = u_off // width
    kern = functools.partial(_s5_kernel, rows=rows, half=half)
    c2 = lambda shape: pl.BlockSpec(shape, lambda i: (0, 0))
    c3 = lambda shape: pl.BlockSpec(shape, lambda i: (0, 0, 0))
    return pl.pallas_call(
        kern,
        grid=(s_len // rows,),
        in_specs=[pl.BlockSpec((rows, width), lambda i: (i, uo)),
                  c2((rows, rows)), c2((rows, rows)),
                  c3(bm.shape), c3(cm.shape),
                  c3(are.shape), c3(aim.shape),
                  pl.BlockSpec(pw.shape, lambda i: (0, 0, 0), pipeline_mode=pl.Buffered(1)),
                  c2((1, width))],
        out_specs=pl.BlockSpec((rows, width), lambda i: (i, 0)),
        out_shape=jax.ShapeDtypeStruct((s_len, width), BF16),
        scratch_shapes=[pltpu.VMEM((nblk, 8, two_half), F32),
                        pltpu.VMEM((nblk, rows, two_half), F32),
                        pltpu.VMEM((rows, width), F32)],
        compiler_params=_cparams(("arbitrary",)),
        name="s5",
    )(z, perm, permt, bm, cm, are, aim, pw, d_skip.reshape(1, width))


def _s5_params(a_re, a_im, log_dt, b_re, b_im, c_re, c_im, seg):
    g_n, p_n = a_re.shape
    nblk = g_n // S5_GPB
    dt = jnp.exp(log_dt)[:, None]
    mag = jnp.exp(a_re * dt)
    ang = a_im * dt
    abar_re = mag * jnp.cos(ang)
    abar_im = mag * jnp.sin(ang)
    nr = abar_re - 1.0
    den = a_re * a_re + a_im * a_im
    zr = (nr * a_re + abar_im * a_im) / den
    zi = (abar_im * a_re - nr * a_im) / den
    bbar_re = zr[:, :, None] * b_re - zi[:, :, None] * b_im
    bbar_im = zr[:, :, None] * b_im + zi[:, :, None] * b_re
    same = jnp.asarray(np.eye(S5_GPB, dtype=bool))

    def embed_b(bb):
        bb = bb.reshape(nblk, S5_GPB, 1, p_n, S5_GROUP).transpose(0, 1, 4, 2, 3)
        m = jnp.where(same[None, :, None, :, None], bb, 0.0)
        return m.reshape(nblk, S5_GPB * S5_GROUP, S5_GPB * p_n)

    bm = jnp.concatenate([embed_b(bbar_re), embed_b(bbar_im)], axis=2).astype(BF16)

    def embed_c(cc):
        cc = cc.reshape(nblk, S5_GPB, S5_GROUP, 1, p_n).transpose(0, 1, 4, 3, 2)
        m = jnp.where(same[None, :, None, :, None], cc, 0.0)
        return m.reshape(nblk, S5_GPB * p_n, S5_GPB * S5_GROUP)

    cm = jnp.concatenate([embed_c(c_re), embed_c(-c_im)], axis=1).astype(BF16)

    def tile8(v):
        v = v.reshape(nblk, 1, S5_GPB * p_n)
        return jnp.broadcast_to(v, (nblk, 8, S5_GPB * p_n))

    k = jnp.arange(1, seg + 1, dtype=F32)[:, None, None]
    mag_k = jnp.exp(k * (a_re * dt))
    ang_k = k * ang

    def table(t):
        t = t.reshape(seg, 1, nblk, S5_GPB * p_n)
        return jnp.broadcast_to(t, (seg, 8, nblk, S5_GPB * p_n)).transpose(2, 0, 1, 3).reshape(
            nblk, 8 * seg, S5_GPB * p_n)

    pw = jnp.concatenate([table(mag_k * jnp.cos(ang_k)), table(mag_k * jnp.sin(ang_k))], axis=2)
    return bm, cm, tile8(abar_re), tile8(abar_im), pw


def _glu_kernel(y_ref, w_ref, gate_ref, o_ref, *, width):
    zg = _dot(y_ref[...], w_ref[...])
    o = zg[:, :width] * _sigmoid(zg[:, width:]) * _silu(gate_ref[...])
    o_ref[...] = o.astype(o_ref.dtype)


def _glu(y, w_glu, layer, z, g_off, tm=512):
    m, width = y.shape
    go = g_off // width
    return pl.pallas_call(
        functools.partial(_glu_kernel, width=width),
        grid=(m // tm,),
        in_specs=[pl.BlockSpec((tm, width), lambda i: (i, 0)),
                  pl.BlockSpec((None, width, 2 * width), lambda i: (layer, 0, 0)),
                  pl.BlockSpec((tm, width), lambda i: (i, go))],
        out_specs=pl.BlockSpec((tm, width), lambda i: (i, 0)),
        out_shape=jax.ShapeDtypeStruct((m, width), BF16),
        compiler_params=_cparams(("parallel",)),
        name="s5_glu",
    )(y, w_glu, z)


def _merge_kernel(oa_ref, ob_ref, oc_ref, ga_ref, gb_ref, gc_ref, bg_ref, wa_ref, wb_ref, wc_ref,
                  wo_ref, x_ref, ng_ref, xo_ref, ho_ref, *, d):
    def gate(g_ref, i):
        return _sigmoid(g_ref[...] + bg_ref[:, i * d:(i + 1) * d])

    merged = gate(ga_ref, 0) * _dot(oa_ref[...], wa_ref[...])
    merged = merged + gate(gb_ref, 1) * _dot(ob_ref[...], wb_ref[...])
    merged = merged + gate(gc_ref, 2) * _dot(oc_ref[...], wc_ref[...])
    xn = x_ref[...] + _dot(merged.astype(BF16), wo_ref[...])
    xo_ref[...] = xn
    ms = jnp.mean(xn * xn, axis=-1, keepdims=True)
    ho_ref[...] = (xn * lax.rsqrt(ms + EPS) * ng_ref[...]).astype(ho_ref.dtype)


def _merge(oa, ob, oc, z, mg_off, b_gate, wa, wb, wc, wo, layer, x, next_g, h_dtype, tm=256):
    m, d = x.shape
    w = oa.shape[1]
    assert mg_off % d == 0
    mo = mg_off // d
    row = lambda width: pl.BlockSpec((tm, width), lambda i: (i, 0))
    gate = lambda k: pl.BlockSpec((tm, d), lambda i: (i, mo + k))
    resident = lambda shape: pl.BlockSpec(shape, lambda i: (0, 0), pipeline_mode=pl.Buffered(1))
    weight = lambda rows: pl.BlockSpec((None, rows, d), lambda i: (layer, 0, 0),
                                       pipeline_mode=pl.Buffered(1))
    return pl.pallas_call(
        functools.partial(_merge_kernel, d=d),
        grid=(m // tm,),
        in_specs=[row(w), row(w), row(w), gate(0), gate(1), gate(2),
                  resident((1, N_BRANCH * d)),
                  weight(w), weight(w), weight(w), weight(d),
                  row(d), resident((1, d))],
        out_specs=[row(d), row(d)],
        out_shape=[jax.ShapeDtypeStruct((m, d), F32), jax.ShapeDtypeStruct((m, d), h_dtype)],
        compiler_params=_cparams(("parallel",)),
        name="merge_out",
    )(oa, ob, oc, z, z, z, b_gate.reshape(1, -1), wa, wb, wc, wo, x, next_g.reshape(1, -1))


def kernel(x, norm_g, w_in, b_gate, fox_bf, hg_lb, hg_norm_g, s5_a_re, s5_a_im, s5_log_dt,
           s5_b_re, s5_b_im, s5_c_re, s5_c_im, s5_d, s5_w_glu, w_br_a, w_br_b, w_br_c, w_out,
           final_g):
    bsz, s_len, d = x.shape
    depth = w_in.shape[0]
    hg_w = hg_lb.shape[1]
    s5_w = s5_d.shape[1]
    n_fox = fox_bf.shape[1]
    fox_w = n_fox * HEAD_DIM
    n_hg = hg_w // HEAD_DIM
    sizes = (hg_w, hg_w, hg_w, hg_w, s5_w, s5_w, fox_w, fox_w, fox_w, n_fox, fox_w, N_BRANCH * d)
    offs = np.concatenate([[0], np.cumsum(sizes)])
    (o_hq, o_hf, o_hi, o_hg, o_su, o_sg, o_fq, o_fk, o_fv, o_ff, o_fg, o_mg, o_end) = (int(v) for v in offs)
    wt = jnp.transpose(w_in, (0, 2, 1))
    tn = fox_w
    tiles = lambda start, stop: list(range(start, stop, tn))
    o_mg2, o_fg2 = 0, o_end - o_mg
    w_glu, w_a, w_b, w_c, w_o = (w.astype(BF16) for w in (s5_w_glu, w_br_a, w_br_b, w_br_c, w_out))
    s5_rows = 256
    outs = []
    for b in range(bsz):
        xb = x[b]
        h = _rmsnorm(xb, norm_g[0], BF16)
        for l in range(depth):
            z = _inproj(h, wt, l, tiles(0, o_fq), F32, tn=tn)
            zqkv = _inproj(h, wt, l, tiles(o_fq, o_ff), BF16,
                           first_tile_scale=HEAD_DIM ** -0.5 * LOG2E, tn=tn)
            zg = _inproj(h, wt, l, tiles(o_mg, o_end) + tiles(o_fg, o_mg), F32, tn=tn)

            o_a = _hgrn(z, hg_lb, hg_norm_g[l], l, o_hq, o_hf, o_hi, o_hg, n_hg)

            s5p = _s5_params(s5_a_re[l], s5_a_im[l], s5_log_dt[l], s5_b_re[l], s5_b_im[l],
                             s5_c_re[l], s5_c_im[l], s5_rows // 8)
            y_b = _s5(z, o_su, *s5p, s5_d[l], rows=s5_rows)
            o_b = _glu(y_b, w_glu, l, z, o_sg)

            ct = _fox_cumlog(h, wt, l, o_ff, fox_bf[l])
            o_c = _fox_attn(zqkv, zg, ct[:, :, None], ct[:, None, :], o_fg2, n_fox)

            last = l == depth - 1
            next_g = final_g if last else norm_g[l + 1]
            xb, h = _merge(o_a, o_b, o_c, zg, o_mg2, b_gate[l], w_a, w_b, w_c, w_o, l,
                           xb, next_g, F32 if last else BF16)
        outs.append(h)
    return outs[0][None] if bsz == 1 else jnp.stack(outs, axis=0)
```

```python
import functools
import math

import numpy as np
import jax
import jax.numpy as jnp
from jax import lax
from jax.experimental import pallas as pl
from jax.experimental.pallas import tpu as pltpu

F32 = jnp.float32
BF16 = jnp.bfloat16

EPS = 1e-6
LANES = 128
VMEM_LIMIT = 56 * 1024 * 1024

HEAD_DIM = 128
S5_GROUP = 16
S5_STATE = 64
S5_GPB = 8
N_BRANCH = 3

HG_SUB = 16


def _cparams(sem, vmem=VMEM_LIMIT):
    return pltpu.CompilerParams(dimension_semantics=sem, vmem_limit_bytes=vmem)


def _dot(a, b):
    return jnp.dot(a, b, preferred_element_type=F32)


def _dot_nt(a, b):
    return lax.dot_general(a, b, (((1,), (1,)), ((), ())), preferred_element_type=F32)


def _dot_tn(a, b):
    return lax.dot_general(a, b, (((0,), (0,)), ((), ())), preferred_element_type=F32)


def _split3(x):
    hi = x.astype(BF16)
    r1 = x - hi.astype(F32)
    mid = r1.astype(BF16)
    lo = (r1 - mid.astype(F32)).astype(BF16)
    return hi, mid, lo


def _dot01(m, x, terms=3):
    parts = _split3(x)[:terms]
    out = _dot(m, parts[0])
    for part in parts[1:]:
        out = out + _dot(m, part)
    return out


def _log_sigmoid(z):
    return jnp.minimum(z, 0.0) - jnp.log1p(jnp.exp(-jnp.abs(z)))


def _sigmoid(z):
    return 1.0 / (1.0 + jnp.exp(-z))


def _silu(z):
    return z * _sigmoid(z)


def _rmsnorm_kernel(x_ref, g_ref, o_ref):
    x = x_ref[...]
    ms = jnp.mean(x * x, axis=-1, keepdims=True)
    o_ref[...] = (x * lax.rsqrt(ms + EPS) * g_ref[...]).astype(o_ref.dtype)


def _rmsnorm(x, g, out_dtype, tm=512):
    m, d = x.shape
    return pl.pallas_call(
        _rmsnorm_kernel,
        grid=(m // tm,),
        in_specs=[pl.BlockSpec((tm, d), lambda i: (i, 0)),
                  pl.BlockSpec((1, d), lambda i: (0, 0))],
        out_specs=pl.BlockSpec((tm, d), lambda i: (i, 0)),
        out_shape=jax.ShapeDtypeStruct((m, d), out_dtype),
        compiler_params=_cparams(("parallel",)),
        name="rmsnorm",
    )(x, g.reshape(1, d))


def _inproj_kernel(starts_ref, h_ref, w_ref, o_ref, wb_ref, *, first_tile_scale):
    del starts_ref
    @pl.when(pl.program_id(1) == 0)
    def _():
        wb_ref[...] = w_ref[0].T.astype(BF16)

    acc = _dot(h_ref[...], wb_ref[...])
    if first_tile_scale is not None:
        acc = acc * jnp.where(pl.program_id(0) == 0, first_tile_scale, 1.0)
    o_ref[...] = acc.astype(o_ref.dtype)


def _inproj(h, wt, layer, row_starts, out_dtype, first_tile_scale=None, tm=1024, tn=1024):
    m, k = h.shape
    n_tiles = len(row_starts)
    assert all(r % 8 == 0 for r in row_starts)
    starts = jnp.asarray(np.asarray(row_starts, np.int32) // 8)
    grid_spec = pltpu.PrefetchScalarGridSpec(
        num_scalar_prefetch=1,
        grid=(n_tiles, m // tm),
        in_specs=[pl.BlockSpec((tm, k), lambda j, i, st: (i, 0)),
                  pl.BlockSpec((pl.Element(1), pl.Element(tn), pl.Element(k)),
                               lambda j, i, st: (layer, st[j] * 8, 0))],
        out_specs=pl.BlockSpec((tm, tn), lambda j, i, st: (i, j)),
        scratch_shapes=[pltpu.VMEM((k, tn), BF16)])
    return pl.pallas_call(
        functools.partial(_inproj_kernel, first_tile_scale=first_tile_scale),
        grid_spec=grid_spec,
        out_shape=jax.ShapeDtypeStruct((m, n_tiles * tn), out_dtype),
        compiler_params=_cparams(("parallel", "arbitrary")),
        name="inproj",
    )(starts, h, wt)


def _foxc_kernel(h_ref, w_ref, b_ref, triu_ref, c_ref, carry_ref):
    @pl.when(pl.program_id(0) == 0)
    def _():
        carry_ref[...] = jnp.zeros_like(carry_ref)

    logits = _dot_nt(w_ref[...].astype(BF16), h_ref[...]) + b_ref[...]
    hi, mid, lo = _split3(_log_sigmoid(logits))
    tri = triu_ref[...]
    cum = _dot(hi, tri) + _dot(mid, tri) + _dot(lo, tri) + carry_ref[:, 0:1]
    c_ref[...] = cum
    tm = cum.shape[1]
    carry_ref[...] = jnp.broadcast_to(cum[:, tm - 1:tm], carry_ref.shape)


def _fox_cumlog(h, wt, layer, row0, b_ff, tm=512):
    m, k = h.shape
    n_heads = b_ff.shape[0]
    assert row0 % n_heads == 0
    triu = jnp.asarray(np.triu(np.ones((tm, tm), np.float32)), BF16)
    return pl.pallas_call(
        _foxc_kernel,
        grid=(m // tm,),
        in_specs=[pl.BlockSpec((tm, k), lambda i: (i, 0)),
                  pl.BlockSpec((None, n_heads, k), lambda i: (layer, row0 // n_heads, 0)),
                  pl.BlockSpec((n_heads, 1), lambda i: (0, 0)),
                  pl.BlockSpec((tm, tm), lambda i: (0, 0))],
        out_specs=pl.BlockSpec((n_heads, tm), lambda i: (0, i)),
        out_shape=jax.ShapeDtypeStruct((n_heads, m), F32),
        scratch_shapes=[pltpu.VMEM((n_heads, LANES), F32)],
        compiler_params=_cparams(("arbitrary",)),
        name="fox_cumlog",
    )(h, wt, b_ff.reshape(n_heads, 1), triu)


NEG_BIG = -1e30


LOG2E = 1.0 / math.log(2.0)
ATT_PASS_ROWS = 32
ATT_XPOSE_ROWS = 512
ATT_WIDTHS = (2, 1)
ATT_UNROLL = 2


def _fox_attn_kernel(tab_ref, q_ref, k_ref, v_ref, ccol_ref, crow_ref, gate_ref, o_ref,
                     va_ref, kt_ref, sa_ref, sb_ref, pa_ref, pb_ref, ala_ref, alb_ref, acc_ref, m_ref,
                     *, tq, runs):
    @pl.when(pl.program_id(0) == 0)
    def _():
        va_ref[:, HEAD_DIM:2 * HEAD_DIM] = jnp.ones((va_ref.shape[0], HEAD_DIM), BF16)

    va_ref[:, 0:HEAD_DIM] = v_ref[...]
    for r0 in range(0, k_ref.shape[0], ATT_XPOSE_ROWS):
        kt_ref[:, r0:r0 + ATT_XPOSE_ROWS] = (
            k_ref[r0:r0 + ATT_XPOSE_ROWS, :].astype(F32).T.astype(BF16))
    m_ref[...] = jnp.full(m_ref.shape, NEG_BIG, F32)
    acc_ref[...] = jnp.zeros(acc_ref.shape, F32)

    def tile(n):
        q0 = pl.multiple_of(tab_ref[0, n] * tq, tq)
        k0 = pl.multiple_of(tab_ref[1, n] * tq, tq)
        return q0, k0

    def logits(n, s_ref, tk):
        q0, k0 = tile(n)
        s_ref[:, 0:tk] = _dot(q_ref[pl.ds(q0, tq), :], kt_ref[:, pl.ds(k0, tk)])

    def softmax(n, s_ref, p_ref, al_ref, masked, tk):
        q0, k0 = tile(n)
        crow = crow_ref[0, :, pl.ds(k0, tk)] * LOG2E
        rows = ATT_PASS_ROWS
        for r in range(tq // rows):
            rs = slice(r * rows, (r + 1) * rows)
            qs = pl.ds(q0 + r * rows, rows)
            s = s_ref[rs, 0:tk] - crow
            if masked:
                row = lax.broadcasted_iota(jnp.int32, (rows, tk), 0) + r * rows
                col = lax.broadcasted_iota(jnp.int32, (rows, tk), 1)
                s = jnp.where(col <= row, s, NEG_BIG)
            ct = ccol_ref[0, qs, :] * LOG2E
            m_old = m_ref[qs, :]
            m_new = jnp.maximum(m_old, jnp.max(s, axis=1, keepdims=True) + ct)
            p_ref[rs, 0:tk] = jnp.exp2(s - (m_new - ct)).astype(BF16)
            m_ref[qs, :] = m_new
            al_ref[rs, :] = jnp.exp2(m_old - m_new)

    def accumulate(n, p_ref, al_ref, tk):
        q0, k0 = tile(n)
        qs = pl.ds(q0, tq)
        acc_ref[qs, :] = (al_ref[...] * acc_ref[qs, :]
                          + _dot(p_ref[:, 0:tk], va_ref[pl.ds(k0, tk), :]))

    def run(first, count, masked, tk):
        if count == 0:
            return
        last = first + count - 1
        nxt = lambda n: jnp.minimum(n, last)
        s_buf = (sa_ref, sb_ref)
        p_buf = ((pa_ref, ala_ref), (pb_ref, alb_ref))
        logits(first, s_buf[0], tk)
        softmax(first, s_buf[0], *p_buf[0], masked, tk)
        logits(nxt(first + 1), s_buf[1], tk)
        n_loop = (count - 1) // ATT_UNROLL

        def body(j, carry):
            n = first + ATT_UNROLL * j
            for u in range(ATT_UNROLL):
                accumulate(n + u, *p_buf[u % 2], tk)
                softmax(n + u + 1, s_buf[(u + 1) % 2], *p_buf[(u + 1) % 2], masked, tk)
                logits(nxt(n + u + 2), s_buf[u % 2], tk)
            return carry

        lax.fori_loop(0, n_loop, body, 0)
        n = first + ATT_UNROLL * n_loop
        rest = count - 1 - ATT_UNROLL * n_loop
        for u in range(rest + 1):
            accumulate(n + u, *p_buf[u % 2], tk)
            if u + 1 <= rest:
                softmax(n + u + 1, s_buf[(u + 1) % 2], *p_buf[(u + 1) % 2], masked, tk)
            if u + 2 <= rest:
                logits(n + u + 2, s_buf[u % 2], tk)

    for first, count, masked, tk in runs:
        run(first, count, masked, tk)

    acc = acc_ref[...]
    out = acc[:, 0:HEAD_DIM] / acc[:, HEAD_DIM:2 * HEAD_DIM]
    o_ref[...] = (out * _silu(gate_ref[...])).astype(o_ref.dtype)


def _fox_attn(zqkv, z, ccol, crow, g_off, n_heads, tq=512):
    s_len = zqkv.shape[0]
    nq = s_len // tq
    go = g_off // HEAD_DIM
    tiles, runs = [], []
    for w in ATT_WIDTHS:
        cls = []
        for qi in range(nq):
            k = 0
            for w2 in ATT_WIDTHS:
                n_w2 = (qi - k) // w2
                if w2 == w:
                    cls += [(qi, k + i * w) for i in range(n_w2)]
                k += n_w2 * w2
        cls.sort(key=lambda t: (t[1], t[0]))
        runs.append((len(tiles), len(cls), False, w * tq))
        tiles += cls
    runs.append((len(tiles), nq, True, tq))
    tiles += [(i, i) for i in range(nq)]
    tab = jnp.asarray(np.array(tiles, np.int32).reshape(-1, 2).T)
    wmax = max(ATT_WIDTHS) * tq
    kern = functools.partial(_fox_attn_kernel, tq=tq, runs=tuple(runs))
    once = pl.Buffered(1)
    head_col = lambda base: pl.BlockSpec((s_len, HEAD_DIM), lambda h, t: (0, base + h),
                                         pipeline_mode=once)
    grid_spec = pltpu.PrefetchScalarGridSpec(
        num_scalar_prefetch=1,
        grid=(n_heads,),
        in_specs=[head_col(0), head_col(n_heads), head_col(2 * n_heads),
                  pl.BlockSpec((1, s_len, 1), lambda h, t: (h, 0, 0), pipeline_mode=once),
                  pl.BlockSpec((1, 1, s_len), lambda h, t: (h, 0, 0)),
                  head_col(go)],
        out_specs=pl.BlockSpec((s_len, HEAD_DIM), lambda h, t: (0, h)),
        scratch_shapes=[pltpu.VMEM((s_len, 2 * HEAD_DIM), BF16),
                        pltpu.VMEM((HEAD_DIM, s_len), BF16),
                        pltpu.VMEM((tq, wmax), F32),
                        pltpu.VMEM((tq, wmax), F32),
                        pltpu.VMEM((tq, wmax), BF16),
                        pltpu.VMEM((tq, wmax), BF16),
                        pltpu.VMEM((tq, 1), F32),
                        pltpu.VMEM((tq, 1), F32),
                        pltpu.VMEM((s_len, 2 * HEAD_DIM), F32),
                        pltpu.VMEM((s_len, 1), F32)])
    return pl.pallas_call(
        kern,
        grid_spec=grid_spec,
        out_shape=jax.ShapeDtypeStruct((s_len, n_heads * HEAD_DIM), BF16),
        compiler_params=_cparams(("arbitrary",)),
        name="fox_attn",
    )(tab, zqkv, zqkv, zqkv, ccol, crow, z)


HG_PROWS = 8 * HG_SUB + 8 * (HG_SUB // 2)


HG_HEADS_PER_STEP = 4


def _hgrn_kernel(q_ref, f_ref, i_ref, gate_ref, lb_ref, ng_ref, tb_ref, ones_ref, mask_ref, o_ref,
                 *scratch, layer, rows):
    @pl.when(pl.program_id(1) == 0)
    def _():
        scratch[0][...] = jnp.zeros_like(scratch[0])

    for hh in range(HG_HEADS_PER_STEP):
        lanes = pl.ds(hh * HEAD_DIM, HEAD_DIM)
        _hgrn_head(*(r.at[:, lanes] for r in (q_ref, f_ref, i_ref, gate_ref, lb_ref, ng_ref)),
                   tb_ref, ones_ref, mask_ref, o_ref.at[:, lanes], *(s.at[hh] for s in scratch),
                   layer=layer, rows=rows)


def _hgrn_head(q_ref, f_ref, i_ref, gate_ref, lb_ref, ng_ref, tb_ref, ones_ref,
               mask_ref, o_ref, st_ref, qt_ref, kt_ref, w_ref, cum_ref, dd_ref, p_ref, sc_ref,
               acc_ref, *, layer, rows):
    z = f_ref[...]
    ls = _log_sigmoid(z)
    if layer == 0:
        g = ls
        logk = ls - z
    else:
        lbp = lb_ref[...]
        e = jnp.exp(lbp - jnp.max(lbp, axis=0, keepdims=True))
        p = e / jnp.sum(e, axis=0, keepdims=True)
        lb = jnp.sum(p[1:layer + 1, :], axis=0, keepdims=True)
        a = jnp.log(lb)
        l1m = jnp.log1p(-lb)
        b = l1m + ls
        g = jnp.maximum(a, b) + jnp.log1p(jnp.exp(-jnp.abs(a - b)))
        logk = l1m + (ls - z)

    cums, tots = [], []
    for r0 in range(0, rows, LANES):
        ct = _dot01(tb_ref[...], g[r0:r0 + LANES, :])
        cums.append(ct[0:LANES, :])
        tots.append(ct[LANES:2 * LANES, :])
    cum = jnp.concatenate(cums, axis=0) * LOG2E
    tot = jnp.concatenate(tots, axis=0) * LOG2E
    w = cum - logk * LOG2E
    qt_ref[...] = (q_ref[...] * jnp.exp2(cum)).astype(BF16)
    kt_ref[...] = jnp.exp2(tot - w).astype(BF16)
    w_ref[...] = w
    cum_ref[...] = cum
    dd_ref[...] = jnp.exp2(tot)

    half = HG_SUB // 2
    n_groups = rows // HG_SUB

    def bcast_row(ref, r):
        return jnp.broadcast_to(ref[r:r + 1, :], (half, HEAD_DIM))

    for g_i in range(n_groups):
        r0 = g_i * HG_SUB
        p0 = g_i * HG_PROWS
        c_lo, c_hi = cum_ref[r0:r0 + half, :], cum_ref[r0 + half:r0 + HG_SUB, :]
        q_lo, q_hi = q_ref[r0:r0 + half, :], q_ref[r0 + half:r0 + HG_SUB, :]
        for s in range(half):
            w_s = bcast_row(w_ref, r0 + s)
            p_lo = q_lo * jnp.exp2(c_lo - w_s + mask_ref[s * half:(s + 1) * half, :])
            p_hi = q_hi * jnp.exp2(c_hi - w_s)
            p_ref[p0 + s * HG_SUB:p0 + (s + 1) * HG_SUB, :] = (
                jnp.concatenate([p_lo, p_hi], axis=0).astype(BF16))
        for s in range(0, half, 2):
            pa = q_hi * jnp.exp2(c_hi - bcast_row(w_ref, r0 + half + s)
                                + mask_ref[s * half:(s + 1) * half, :])
            pb = q_hi * jnp.exp2(c_hi - bcast_row(w_ref, r0 + half + s + 1)
                                + mask_ref[(s + 1) * half:(s + 2) * half, :])
            base = p0 + half * HG_SUB + s * half
            p_ref[base:base + HG_SUB, :] = jnp.concatenate([pa, pb], axis=0).astype(BF16)

    sc_ref[...] = _dot(p_ref[...], ones_ref[...])

    upds = [_dot_tn(i_ref[g_i * HG_SUB:(g_i + 1) * HG_SUB, :].astype(BF16),
                    kt_ref[g_i * HG_SUB:(g_i + 1) * HG_SUB, :]) for g_i in range(n_groups)]
    st = st_ref[...]
    for g_i in range(n_groups):
        r0 = g_i * HG_SUB
        p0 = g_i * HG_PROWS
        o_lo = jnp.zeros((half, HEAD_DIM), F32)
        o_hi = jnp.zeros((half, HEAD_DIM), F32)
        for s in range(half):
            v_s = bcast_row(i_ref, r0 + s)
            o_lo = o_lo + sc_ref[p0 + s * HG_SUB:p0 + s * HG_SUB + half, :] * v_s
            o_hi = o_hi + sc_ref[p0 + s * HG_SUB + half:p0 + (s + 1) * HG_SUB, :] * v_s
        for s in range(half):
            base = p0 + half * HG_SUB + s * half
            o_hi = o_hi + sc_ref[base:base + half, :] * bcast_row(i_ref, r0 + half + s)
        o_inter = _dot_nt(qt_ref[r0:r0 + HG_SUB, :], st.astype(BF16))
        acc_ref[r0:r0 + HG_SUB, :] = o_inter + jnp.concatenate([o_lo, o_hi], axis=0)
        st = st * dd_ref[r0:r0 + 1, :] + upds[g_i]
    st_ref[...] = st

    o = acc_ref[...]
    ms = jnp.mean(o * o, axis=-1, keepdims=True)
    o = o * lax.rsqrt(ms + EPS) * ng_ref[...]
    o_ref[...] = (o * _silu(gate_ref[...])).astype(o_ref.dtype)


def _hgrn(z, hg_lb, norm_g, layer, q_off, f_off, i_off, g_off, n_heads, rows=512):
    s_len = z.shape[0]
    depth = hg_lb.shape[0]
    hps = HG_HEADS_PER_STEP
    width = hps * HEAD_DIM
    assert n_heads % hps == 0 and all(o % width == 0 for o in (q_off, f_off, i_off, g_off))
    qo, fo, io, go = (o // width for o in (q_off, f_off, i_off, g_off))
    r = np.arange(LANES)
    same = (r[:, None] // HG_SUB) == (r[None, :] // HG_SUB)
    tb = jnp.asarray(np.concatenate([same & (r[None, :] <= r[:, None]), same]).astype(np.float32), BF16)
    ones = jnp.ones((HEAD_DIM, HEAD_DIM), BF16)
    half = HG_SUB // 2
    t_idx = np.arange(half)
    mask_np = np.where(t_idx[None, :, None] >= t_idx[:, None, None], 0.0, NEG_BIG)
    mask = jnp.asarray(np.broadcast_to(mask_np, (half, half, HEAD_DIM)).reshape(half * half, HEAD_DIM), F32)
    n_prows = (rows // HG_SUB) * HG_PROWS
    kern = functools.partial(_hgrn_kernel, layer=layer, rows=rows)
    blk_spec = lambda off: pl.BlockSpec((rows, width), lambda h, i: (i, off + h))
    const = lambda shape: pl.BlockSpec(shape, lambda h, i: (0, 0))
    per_head = lambda shape, dtype: pltpu.VMEM((hps,) + shape, dtype)
    return pl.pallas_call(
        kern,
        grid=(n_heads // hps, s_len // rows),
        in_specs=[blk_spec(qo), blk_spec(fo), blk_spec(io), blk_spec(go),
                  pl.BlockSpec((depth, width), lambda h, i: (0, h)),
                  pl.BlockSpec((1, width), lambda h, i: (0, h)),
                  const((2 * LANES, LANES)), const((HEAD_DIM, HEAD_DIM)),
                  const((half * half, HEAD_DIM))],
        out_specs=pl.BlockSpec((rows, width), lambda h, i: (i, h)),
        out_shape=jax.ShapeDtypeStruct((s_len, n_heads * HEAD_DIM), BF16),
        scratch_shapes=[per_head((HEAD_DIM, HEAD_DIM), F32),
                        per_head((rows, HEAD_DIM), BF16),
                        per_head((rows, HEAD_DIM), BF16),
                        per_head((rows, HEAD_DIM), F32),
                        per_head((rows, HEAD_DIM), F32),
                        per_head((rows, HEAD_DIM), F32),
                        per_head((n_prows, HEAD_DIM), BF16),
                        per_head((n_prows, HEAD_DIM), F32),
                        per_head((rows, HEAD_DIM), F32)],
        compiler_params=_cparams(("parallel", "arbitrary")),
        name="hgrn2",
    )(z, z, z, z, hg_lb, norm_g.reshape(1, -1), tb, ones, mask)


def _gelu_tanh(x):
    c = math.sqrt(2.0 / math.pi)
    return 0.5 * x * (1.0 + jnp.tanh(c * (x + 0.044715 * (x * x * x))))


def _s5_kernel(u_ref, perm_ref, permt_ref, bm_ref, cm_ref, are_ref, aim_ref, pw_ref,
               d_ref, o_ref, state_ref, x_ref, yp_ref, *, rows, half):
    @pl.when(pl.program_id(0) == 0)
    def _():
        state_ref[...] = jnp.zeros_like(state_ref)

    nblk = bm_ref.shape[0]
    nt = rows // 8
    u = u_ref[...]
    up = _dot(perm_ref[...], u.astype(BF16)).astype(BF16)
    sub = lax.broadcasted_iota(jnp.int32, (8, half), 0)

    re, im = slice(0, half), slice(half, 2 * half)
    for b in range(nblk):
        x_ref[b] = _dot(up[:, b * LANES:(b + 1) * LANES], bm_ref[b])
    for b in range(nblk):
        are = are_ref[b]
        aim = aim_ref[b]
        xr = jnp.zeros((8, half), F32)
        xi = jnp.zeros((8, half), F32)
        for t in range(nt):
            rs = slice(t * 8, (t + 1) * 8)
            xr, xi = (are * xr - aim * xi + x_ref[b, rs, re],
                      are * xi + aim * xr + x_ref[b, rs, im])
            x_ref[b, rs, re] = xr
            x_ref[b, rs, im] = xi

        er, ei = xr, xi
        alre = pw_ref[b, rows - 1:rows, re]
        alim = pw_ref[b, rows - 1:rows, im]
        cr = state_ref[b, 0:1, re]
        ci = state_ref[b, 0:1, im]
        ctr = jnp.zeros((8, half), F32)
        cti = jnp.zeros((8, half), F32)
        for s in range(8):
            ctr = jnp.where(sub == s, cr, ctr)
            cti = jnp.where(sub == s, ci, cti)
            cr, ci = (alre * cr - alim * ci + er[s:s + 1, :],
                      alre * ci + alim * cr + ei[s:s + 1, :])
        state_ref[b, :, re] = jnp.broadcast_to(cr, (8, half))
        state_ref[b, :, im] = jnp.broadcast_to(ci, (8, half))

        for t in range(nt):
            rs = slice(t * 8, (t + 1) * 8)
            pr = pw_ref[b, rs, re]
            pi = pw_ref[b, rs, im]
            x_ref[b, rs, re] = x_ref[b, rs, re] + (pr * ctr - pi * cti)
            x_ref[b, rs, im] = x_ref[b, rs, im] + (pr * cti + pi * ctr)
        yp_ref[:, b * LANES:(b + 1) * LANES] = _dot(x_ref[b].astype(BF16), cm_ref[b])

    y = _dot01(permt_ref[...], yp_ref[...], terms=2) + d_ref[...] * u
    o_ref[...] = _gelu_tanh(y).astype(o_ref.dtype)


def _s5(z, u_off, bm, cm, are, aim, pw, d_skip, rows=256):
    s_len = z.shape[0]
    width = d_skip.shape[0]
    nblk, _, two_half = bm.shape
    half = two_half // 2
    seg = rows // 8
    rho = np.arange(rows)
    t_of = (rho % 8) * seg + rho // 8
    perm_np = np.zeros((rows, rows), np.float32)
    perm_np[rho, t_of] = 1.0
    perm = jnp.asarray(perm_np, BF16)
    permt = jnp.asarray(perm_np.T, BF16)
    uo = u_off // width
    kern = functools.partial(_s5_kernel, rows=rows, half=half)
    c2 = lambda shape: pl.BlockSpec(shape, lambda i: (0, 0))
    c3 = lambda shape: pl.BlockSpec(shape, lambda i: (0, 0, 0))
    return pl.pallas_call(
        kern,
        grid=(s_len // rows,),
        in_specs=[pl.BlockSpec((rows, width), lambda i: (i, uo)),
                  c2((rows, rows)), c2((rows, rows)),
                  c3(bm.shape), c3(cm.shape),
                  c3(are.shape), c3(aim.shape),
                  pl.BlockSpec(pw.shape, lambda i: (0, 0, 0), pipeline_mode=pl.Buffered(1)),
                  c2((1, width))],
        out_specs=pl.BlockSpec((rows, width), lambda i: (i, 0)),
        out_shape=jax.ShapeDtypeStruct((s_len, width), BF16),
        scratch_shapes=[pltpu.VMEM((nblk, 8, two_half), F32),
                        pltpu.VMEM((nblk, rows, two_half), F32),
                        pltpu.VMEM((rows, width), F32)],
        compiler_params=_cparams(("arbitrary",)),
        name="s5",
    )(z, perm, permt, bm, cm, are, aim, pw, d_skip.reshape(1, width))


def _s5_params(a_re, a_im, log_dt, b_re, b_im, c_re, c_im, seg):
    g_n, p_n = a_re.shape
    nblk = g_n // S5_GPB
    dt = jnp.exp(log_dt)[:, None]
    mag = jnp.exp(a_re * dt)
    ang = a_im * dt
    abar_re = mag * jnp.cos(ang)
    abar_im = mag * jnp.sin(ang)
    nr = abar_re - 1.0
    den = a_re * a_re + a_im * a_im
    zr = (nr * a_re + abar_im * a_im) / den
    zi = (abar_im * a_re - nr * a_im) / den
    bbar_re = zr[:, :, None] * b_re - zi[:, :, None] * b_im
    bbar_im = zr[:, :, None] * b_im + zi[:, :, None] * b_re
    same = jnp.asarray(np.eye(S5_GPB, dtype=bool))

    def embed_b(bb):
        bb = bb.reshape(nblk, S5_GPB, 1, p_n, S5_GROUP).transpose(0, 1, 4, 2, 3)
        m = jnp.where(same[None, :, None, :, None], bb, 0.0)
        return m.reshape(nblk, S5_GPB * S5_GROUP, S5_GPB * p_n)

    bm = jnp.concatenate([embed_b(bbar_re), embed_b(bbar_im)], axis=2).astype(BF16)

    def embed_c(cc):
        cc = cc.reshape(nblk, S5_GPB, S5_GROUP, 1, p_n).transpose(0, 1, 4, 3, 2)
        m = jnp.where(same[None, :, None, :, None], cc, 0.0)
        return m.reshape(nblk, S5_GPB * p_n, S5_GPB * S5_GROUP)

    cm = jnp.concatenate([embed_c(c_re), embed_c(-c_im)], axis=1).astype(BF16)

    def tile8(v):
        v = v.reshape(nblk, 1, S5_GPB * p_n)
        return jnp.broadcast_to(v, (nblk, 8, S5_GPB * p_n))

    k = jnp.arange(1, seg + 1, dtype=F32)[:, None, None]
    mag_k = jnp.exp(k * (a_re * dt))
    ang_k = k * ang

    def table(t):
        t = t.reshape(seg, 1, nblk, S5_GPB * p_n)
        return jnp.broadcast_to(t, (seg, 8, nblk, S5_GPB * p_n)).transpose(2, 0, 1, 3).reshape(
            nblk, 8 * seg, S5_GPB * p_n)

    pw = jnp.concatenate([table(mag_k * jnp.cos(ang_k)), table(mag_k * jnp.sin(ang_k))], axis=2)
    return bm, cm, tile8(abar_re), tile8(abar_im), pw


def _glu_kernel(y_ref, w_ref, gate_ref, o_ref, *, width):
    zg = _dot(y_ref[...], w_ref[...])
    o = zg[:, :width] * _sigmoid(zg[:, width:]) * _silu(gate_ref[...])
    o_ref[...] = o.astype(o_ref.dtype)


def _glu(y, w_glu, layer, z, g_off, tm=512):
    m, width = y.shape
    go = g_off // width
    return pl.pallas_call(
        functools.partial(_glu_kernel, width=width),
        grid=(m // tm,),
        in_specs=[pl.BlockSpec((tm, width), lambda i: (i, 0)),
                  pl.BlockSpec((None, width, 2 * width), lambda i: (layer, 0, 0)),
                  pl.BlockSpec((tm, width), lambda i: (i, go))],
        out_specs=pl.BlockSpec((tm, width), lambda i: (i, 0)),
        out_shape=jax.ShapeDtypeStruct((m, width), BF16),
        compiler_params=_cparams(("parallel",)),
        name="s5_glu",
    )(y, w_glu, z)


def _merge_kernel(oa_ref, ob_ref, oc_ref, ga_ref, gb_ref, gc_ref, bg_ref, wa_ref, wb_ref, wc_ref,
                  wo_ref, x_ref, ng_ref, xo_ref, ho_ref, *, d):
    def gate(g_ref, i):
        return _sigmoid(g_ref[...] + bg_ref[:, i * d:(i + 1) * d])

    merged = gate(ga_ref, 0) * _dot(oa_ref[...], wa_ref[...])
    merged = merged + gate(gb_ref, 1) * _dot(ob_ref[...], wb_ref[...])
    merged = merged + gate(gc_ref, 2) * _dot(oc_ref[...], wc_ref[...])
    xn = x_ref[...] + _dot(merged.astype(BF16), wo_ref[...])
    xo_ref[...] = xn
    ms = jnp.mean(xn * xn, axis=-1, keepdims=True)
    ho_ref[...] = (xn * lax.rsqrt(ms + EPS) * ng_ref[...]).astype(ho_ref.dtype)


def _merge(oa, ob, oc, z, mg_off, b_gate, wa, wb, wc, wo, layer, x, next_g, h_dtype, tm=256):
    m, d = x.shape
    w = oa.shape[1]
    assert mg_off % d == 0
    mo = mg_off // d
    row = lambda width: pl.BlockSpec((tm, width), lambda i: (i, 0))
    gate = lambda k: pl.BlockSpec((tm, d), lambda i: (i, mo + k))
    resident = lambda shape: pl.BlockSpec(shape, lambda i: (0, 0), pipeline_mode=pl.Buffered(1))
    weight = lambda rows: pl.BlockSpec((None, rows, d), lambda i: (layer, 0, 0),
                                       pipeline_mode=pl.Buffered(1))
    return pl.pallas_call(
        functools.partial(_merge_kernel, d=d),
        grid=(m // tm,),
        in_specs=[row(w), row(w), row(w), gate(0), gate(1), gate(2),
                  resident((1, N_BRANCH * d)),
                  weight(w), weight(w), weight(w), weight(d),
                  row(d), resident((1, d))],
        out_specs=[row(d), row(d)],
        out_shape=[jax.ShapeDtypeStruct((m, d), F32), jax.ShapeDtypeStruct((m, d), h_dtype)],
        compiler_params=_cparams(("parallel",)),
        name="merge_out",
    )(oa, ob, oc, z, z, z, b_gate.reshape(1, -1), wa, wb, wc, wo, x, next_g.reshape(1, -1))


def kernel(x, norm_g, w_in, b_gate, fox_bf, hg_lb, hg_norm_g, s5_a_re, s5_a_im, s5_log_dt,
           s5_b_re, s5_b_im, s5_c_re, s5_c_im, s5_d, s5_w_glu, w_br_a, w_br_b, w_br_c, w_out,
           final_g):
    bsz, s_len, d = x.shape
    depth = w_in.shape[0]
    hg_w = hg_lb.shape[1]
    s5_w = s5_d.shape[1]
    n_fox = fox_bf.shape[1]
    fox_w = n_fox * HEAD_DIM
    n_hg = hg_w // HEAD_DIM
    sizes = (hg_w, hg_w, hg_w, hg_w, s5_w, s5_w, fox_w, fox_w, fox_w, n_fox, fox_w, N_BRANCH * d)
    offs = np.concatenate([[0], np.cumsum(sizes)])
    (o_hq, o_hf, o_hi, o_hg, o_su, o_sg, o_fq, o_fk, o_fv, o_ff, o_fg, o_mg, o_end) = (int(v) for v in offs)
    wt = jnp.transpose(w_in, (0, 2, 1))
    tn = fox_w
    tiles = lambda start, stop: list(range(start, stop, tn))
    o_mg2, o_fg2 = 0, o_end - o_mg
    w_glu, w_a, w_b, w_c, w_o = (w.astype(BF16) for w in (s5_w_glu, w_br_a, w_br_b, w_br_c, w_out))
    s5_rows = 256
    outs = []
    for b in range(bsz):
        xb = x[b]
        h = _rmsnorm(xb, norm_g[0], BF16)
        for l in range(depth):
            z = _inproj(h, wt, l, tiles(0, o_fq), F32, tn=tn)
            zqkv = _inproj(h, wt, l, tiles(o_fq, o_ff), BF16,
                           first_tile_scale=HEAD_DIM ** -0.5 * LOG2E, tn=tn)
            zg = _inproj(h, wt, l, tiles(o_mg, o_end) + tiles(o_fg, o_mg), F32, tn=tn)

            o_a = _hgrn(z, hg_lb, hg_norm_g[l], l, o_hq, o_hf, o_hi, o_hg, n_hg)

            s5p = _s5_params(s5_a_re[l], s5_a_im[l], s5_log_dt[l], s5_b_re[l], s5_b_im[l],
                             s5_c_re[l], s5_c_im[l], s5_rows // 8)
            y_b = _s5(z, o_su, *s5p, s5_d[l], rows=s5_rows)
            o_b = _glu(y_b, w_glu, l, z, o_sg)

            ct = _fox_cumlog(h, wt, l, o_ff, fox_bf[l])
            o_c = _fox_attn(zqkv, zg, ct[:, :, None], ct[:, None, :], o_fg2, n_fox)

            last = l == depth - 1
            next_g = final_g if last else norm_g[l + 1]
            xb, h = _merge(o_a, o_b, o_c, zg, o_mg2, b_gate[l], w_a, w_b, w_c, w_o, l,
                           xb, next_g, F32 if last else BF16)
        outs.append(h)
    return outs[0][None] if bsz == 1 else jnp.stack(outs, axis=0)
```

```python
import functools
import math

import numpy as np
import jax
import jax.numpy as jnp
from jax import lax
from jax.experimental import pallas as pl
from jax.experimental.pallas import tpu as pltpu

F32 = jnp.float32
BF16 = jnp.bfloat16

EPS = 1e-6
LANES = 128
VMEM_LIMIT = 56 * 1024 * 1024

HEAD_DIM = 128
S5_GROUP = 16
S5_STATE = 64
S5_GPB = 8
N_BRANCH = 3

HG_SUB = 16


def _cparams(sem, vmem=VMEM_LIMIT):
    return pltpu.CompilerParams(dimension_semantics=sem, vmem_limit_bytes=vmem)


def _dot(a, b):
    return jnp.dot(a, b, preferred_element_type=F32)


def _dot_nt(a, b):
    return lax.dot_general(a, b, (((1,), (1,)), ((), ())), preferred_element_type=F32)


def _dot_tn(a, b):
    return lax.dot_general(a, b, (((0,), (0,)), ((), ())), preferred_element_type=F32)


def _split3(x):
    hi = x.astype(BF16)
    r1 = x - hi.astype(F32)
    mid = r1.astype(BF16)
    lo = (r1 - mid.astype(F32)).astype(BF16)
    return hi, mid, lo


def _dot01(m, x, terms=3):
    parts = _split3(x)[:terms]
    out = _dot(m, parts[0])
    for part in parts[1:]:
        out = out + _dot(m, part)
    return out


def _log_sigmoid(z):
    return jnp.minimum(z, 0.0) - jnp.log1p(jnp.exp(-jnp.abs(z)))


def _sigmoid(z):
    return 1.0 / (1.0 + jnp.exp(-z))


def _silu(z):
    return z * _sigmoid(z)


def _rmsnorm_kernel(x_ref, g_ref, o_ref):
    x = x_ref[...]
    ms = jnp.mean(x * x, axis=-1, keepdims=True)
    o_ref[...] = (x * lax.rsqrt(ms + EPS) * g_ref[...]).astype(o_ref.dtype)


def _rmsnorm(x, g, out_dtype, tm=1024):
    m, d = x.shape
    return pl.pallas_call(
        _rmsnorm_kernel,
        grid=(m // tm,),
        in_specs=[pl.BlockSpec((tm, d), lambda i: (i, 0)),
                  pl.BlockSpec((1, d), lambda i: (0, 0))],
        out_specs=pl.BlockSpec((tm, d), lambda i: (i, 0)),
        out_shape=jax.ShapeDtypeStruct((m, d), out_dtype),
        compiler_params=_cparams(("parallel",)),
        name="rmsnorm",
    )(x, g.reshape(1, d))


def _inproj_kernel(starts_ref, h_ref, w_ref, o_ref, wb_ref, *, first_tile_scale):
    del starts_ref
    @pl.when(pl.program_id(1) == 0)
    def _():
        wb_ref[...] = w_ref[0].T.astype(BF16)

    acc = _dot(h_ref[...], wb_ref[...])
    if first_tile_scale is not None:
        acc = acc * jnp.where(pl.program_id(0) == 0, first_tile_scale, 1.0)
    o_ref[...] = acc.astype(o_ref.dtype)


def _inproj(h, wt, layer, row_starts, out_dtype, first_tile_scale=None, tm=1024, tn=1024):
    m, k = h.shape
    n_tiles = len(row_starts)
    assert all(r % 8 == 0 for r in row_starts)
    starts = jnp.asarray(np.asarray(row_starts, np.int32) // 8)
    grid_spec = pltpu.PrefetchScalarGridSpec(
        num_scalar_prefetch=1,
        grid=(n_tiles, m // tm),
        in_specs=[pl.BlockSpec((tm, k), lambda j, i, st: (i, 0)),
                  pl.BlockSpec((pl.Element(1), pl.Element(tn), pl.Element(k)),
                               lambda j, i, st: (layer, st[j] * 8, 0))],
        out_specs=pl.BlockSpec((tm, tn), lambda j, i, st: (i, j)),
        scratch_shapes=[pltpu.VMEM((k, tn), BF16)])
    return pl.pallas_call(
        functools.partial(_inproj_kernel, first_tile_scale=first_tile_scale),
        grid_spec=grid_spec,
        out_shape=jax.ShapeDtypeStruct((m, n_tiles * tn), out_dtype),
        compiler_params=_cparams(("parallel", "arbitrary")),
        name="inproj",
    )(starts, h, wt)


def _foxc_kernel(h_ref, w_ref, b_ref, triu_ref, c_ref, carry_ref):
    @pl.when(pl.program_id(0) == 0)
    def _():
        carry_ref[...] = jnp.zeros_like(carry_ref)

    logits = _dot_nt(w_ref[...].astype(BF16), h_ref[...]) + b_ref[...]
    hi, mid, lo = _split3(_log_sigmoid(logits))
    tri = triu_ref[...]
    cum = _dot(hi, tri) + _dot(mid, tri) + _dot(lo, tri) + carry_ref[:, 0:1]
    c_ref[...] = cum
    tm = cum.shape[1]
    carry_ref[...] = jnp.broadcast_to(cum[:, tm - 1:tm], carry_ref.shape)


def _fox_cumlog(h, wt, layer, row0, b_ff, tm=512):
    m, k = h.shape
    n_heads = b_ff.shape[0]
    assert row0 % n_heads == 0
    triu = jnp.asarray(np.triu(np.ones((tm, tm), np.float32)), BF16)
    return pl.pallas_call(
        _foxc_kernel,
        grid=(m // tm,),
        in_specs=[pl.BlockSpec((tm, k), lambda i: (i, 0)),
                  pl.BlockSpec((None, n_heads, k), lambda i: (layer, row0 // n_heads, 0)),
                  pl.BlockSpec((n_heads, 1), lambda i: (0, 0)),
                  pl.BlockSpec((tm, tm), lambda i: (0, 0))],
        out_specs=pl.BlockSpec((n_heads, tm), lambda i: (0, i)),
        out_shape=jax.ShapeDtypeStruct((n_heads, m), F32),
        scratch_shapes=[pltpu.VMEM((n_heads, LANES), F32)],
        compiler_params=_cparams(("arbitrary",)),
        name="fox_cumlog",
    )(h, wt, b_ff.reshape(n_heads, 1), triu)


NEG_BIG = -1e30


LOG2E = 1.0 / math.log(2.0)
ATT_PASS_ROWS = 32
ATT_XPOSE_ROWS = 512
ATT_WIDTHS = (2, 1)
ATT_UNROLL = 2


def _fox_attn_kernel(tab_ref, q_ref, k_ref, v_ref, ccol_ref, crow_ref, gate_ref, o_ref,
                     va_ref, kt_ref, sa_ref, sb_ref, pa_ref, pb_ref, ala_ref, alb_ref, acc_ref, m_ref,
                     *, tq, runs):
    @pl.when(pl.program_id(0) == 0)
    def _():
        va_ref[:, HEAD_DIM:2 * HEAD_DIM] = jnp.ones((va_ref.shape[0], HEAD_DIM), BF16)

    va_ref[:, 0:HEAD_DIM] = v_ref[...]
    for r0 in range(0, k_ref.shape[0], ATT_XPOSE_ROWS):
        kt_ref[:, r0:r0 + ATT_XPOSE_ROWS] = (
            k_ref[r0:r0 + ATT_XPOSE_ROWS, :].astype(F32).T.astype(BF16))

    def tile(n):
        q0 = pl.multiple_of(tab_ref[0, n] * tq, tq)
        k0 = pl.multiple_of(tab_ref[1, n] * tq, tq)
        return q0, k0

    def logits(n, s_ref, tk):
        q0, k0 = tile(n)
        s_ref[:, 0:tk] = _dot(q_ref[pl.ds(q0, tq), :], kt_ref[:, pl.ds(k0, tk)])

    def softmax(n, s_ref, p_ref, al_ref, masked, tk):
        q0, k0 = tile(n)
        crow = crow_ref[0, :, pl.ds(k0, tk)] * LOG2E
        rows = ATT_PASS_ROWS
        for r in range(tq // rows):
            rs = slice(r * rows, (r + 1) * rows)
            qs = pl.ds(q0 + r * rows, rows)
            s = s_ref[rs, 0:tk] - crow
            if masked:
                row = lax.broadcasted_iota(jnp.int32, (rows, tk), 0) + r * rows
                col = lax.broadcasted_iota(jnp.int32, (rows, tk), 1)
                s = jnp.where(col <= row, s, NEG_BIG)
            ct = ccol_ref[0, qs, :] * LOG2E
            m_new = jnp.max(s, axis=1, keepdims=True) + ct
            if not masked:
                m_old = m_ref[qs, :]
                m_new = jnp.maximum(m_old, m_new)
                al_ref[rs, :] = jnp.exp2(m_old - m_new)
            p_ref[rs, 0:tk] = jnp.exp2(s - (m_new - ct)).astype(BF16)
            m_ref[qs, :] = m_new

    def accumulate(n, p_ref, al_ref, masked, tk):
        q0, k0 = tile(n)
        qs = pl.ds(q0, tq)
        pv = _dot(p_ref[:, 0:tk], va_ref[pl.ds(k0, tk), :])
        acc_ref[qs, :] = pv if masked else al_ref[...] * acc_ref[qs, :] + pv

    def run(first, count, masked, tk):
        if count == 0:
            return
        last = first + count - 1
        nxt = lambda n: jnp.minimum(n, last)
        s_buf = (sa_ref, sb_ref)
        p_buf = ((pa_ref, ala_ref), (pb_ref, alb_ref))
        logits(first, s_buf[0], tk)
        softmax(first, s_buf[0], *p_buf[0], masked, tk)
        logits(nxt(first + 1), s_buf[1], tk)
        n_loop = (count - 1) // ATT_UNROLL

        def body(j, carry):
            n = first + ATT_UNROLL * j
            for u in range(ATT_UNROLL):
                accumulate(n + u, *p_buf[u % 2], masked, tk)
                softmax(n + u + 1, s_buf[(u + 1) % 2], *p_buf[(u + 1) % 2], masked, tk)
                logits(nxt(n + u + 2), s_buf[u % 2], tk)
            return carry

        lax.fori_loop(0, n_loop, body, 0)
        n = first + ATT_UNROLL * n_loop
        rest = count - 1 - ATT_UNROLL * n_loop
        for u in range(rest + 1):
            accumulate(n + u, *p_buf[u % 2], masked, tk)
            if u + 1 <= rest:
                softmax(n + u + 1, s_buf[(u + 1) % 2], *p_buf[(u + 1) % 2], masked, tk)
            if u + 2 <= rest:
                logits(n + u + 2, s_buf[u % 2], tk)

    for first, count, masked, tk in runs:
        run(first, count, masked, tk)

    acc = acc_ref[...]
    out = acc[:, 0:HEAD_DIM] / acc[:, HEAD_DIM:2 * HEAD_DIM]
    o_ref[...] = (out * _silu(gate_ref[...])).astype(o_ref.dtype)


def _fox_attn(zqkv, z, ccol, crow, g_off, n_heads, tq=512):
    s_len = zqkv.shape[0]
    nq = s_len // tq
    go = g_off // HEAD_DIM
    tiles = [(i, i) for i in range(nq)]
    runs = [(0, nq, True, tq)]
    for w in ATT_WIDTHS:
        cls = []
        for qi in range(nq):
            k = 0
            for w2 in ATT_WIDTHS:
                n_w2 = (qi - k) // w2
                if w2 == w:
                    cls += [(qi, k + i * w) for i in range(n_w2)]
                k += n_w2 * w2
        cls.sort(key=lambda t: (t[1], t[0]))
        runs.append((len(tiles), len(cls), False, w * tq))
        tiles += cls
    tab = jnp.asarray(np.array(tiles, np.int32).reshape(-1, 2).T)
    wmax = max(ATT_WIDTHS) * tq
    kern = functools.partial(_fox_attn_kernel, tq=tq, runs=tuple(runs))
    once = pl.Buffered(1)
    head_col = lambda base: pl.BlockSpec((s_len, HEAD_DIM), lambda h, t: (0, base + h),
                                         pipeline_mode=once)
    grid_spec = pltpu.PrefetchScalarGridSpec(
        num_scalar_prefetch=1,
        grid=(n_heads,),
        in_specs=[head_col(0), head_col(n_heads), head_col(2 * n_heads),
                  pl.BlockSpec((1, s_len, 1), lambda h, t: (h, 0, 0), pipeline_mode=once),
                  pl.BlockSpec((1, 1, s_len), lambda h, t: (h, 0, 0)),
                  head_col(go)],
        out_specs=pl.BlockSpec((s_len, HEAD_DIM), lambda h, t: (0, h)),
        scratch_shapes=[pltpu.VMEM((s_len, 2 * HEAD_DIM), BF16),
                        pltpu.VMEM((HEAD_DIM, s_len), BF16),
                        pltpu.VMEM((tq, wmax), F32),
                        pltpu.VMEM((tq, wmax), F32),
                        pltpu.VMEM((tq, wmax), BF16),
                        pltpu.VMEM((tq, wmax), BF16),
                        pltpu.VMEM((tq, 1), F32),
                        pltpu.VMEM((tq, 1), F32),
                        pltpu.VMEM((s_len, 2 * HEAD_DIM), F32),
                        pltpu.VMEM((s_len, 1), F32)])
    return pl.pallas_call(
        kern,
        grid_spec=grid_spec,
        out_shape=jax.ShapeDtypeStruct((s_len, n_heads * HEAD_DIM), BF16),
        compiler_params=_cparams(("arbitrary",)),
        name="fox_attn",
    )(tab, zqkv, zqkv, zqkv, ccol, crow, z)


HG_PROWS = 8 * HG_SUB + 8 * (HG_SUB // 2)


HG_HEADS_PER_STEP = 4


def _hgrn_kernel(q_ref, f_ref, i_ref, gate_ref, lb_ref, ng_ref, tb_ref, ones_ref, mask_ref, o_ref,
                 *scratch, layer, rows):
    @pl.when(pl.program_id(1) == 0)
    def _():
        scratch[0][...] = jnp.zeros_like(scratch[0])

    for hh in range(HG_HEADS_PER_STEP):
        lanes = pl.ds(hh * HEAD_DIM, HEAD_DIM)
        _hgrn_head(*(r.at[:, lanes] for r in (q_ref, f_ref, i_ref, gate_ref, lb_ref, ng_ref)),
                   tb_ref, ones_ref, mask_ref, o_ref.at[:, lanes], *(s.at[hh] for s in scratch),
                   layer=layer, rows=rows)


def _hgrn_head(q_ref, f_ref, i_ref, gate_ref, lb_ref, ng_ref, tb_ref, ones_ref,
               mask_ref, o_ref, st_ref, qt_ref, kt_ref, w_ref, cum_ref, dd_ref, p_ref, sc_ref,
               acc_ref, *, layer, rows):
    z = f_ref[...]
    ls = _log_sigmoid(z)
    if layer == 0:
        g = ls
        logk = ls - z
    else:
        lbp = lb_ref[...]
        e = jnp.exp(lbp - jnp.max(lbp, axis=0, keepdims=True))
        p = e / jnp.sum(e, axis=0, keepdims=True)
        lb = jnp.sum(p[1:layer + 1, :], axis=0, keepdims=True)
        a = jnp.log(lb)
        l1m = jnp.log1p(-lb)
        b = l1m + ls
        g = jnp.maximum(a, b) + jnp.log1p(jnp.exp(-jnp.abs(a - b)))
        logk = l1m + (ls - z)

    cums, tots = [], []
    for r0 in range(0, rows, LANES):
        ct = _dot01(tb_ref[...], g[r0:r0 + LANES, :])
        cums.append(ct[0:LANES, :])
        tots.append(ct[LANES:2 * LANES, :])
    cum = jnp.concatenate(cums, axis=0) * LOG2E
    tot = jnp.concatenate(tots, axis=0) * LOG2E
    w = cum - logk * LOG2E
    qt_ref[...] = (q_ref[...] * jnp.exp2(cum)).astype(BF16)
    kt_ref[...] = jnp.exp2(tot - w).astype(BF16)
    w_ref[...] = w
    cum_ref[...] = cum
    dd_ref[...] = jnp.exp2(tot)

    half = HG_SUB // 2
    n_groups = rows // HG_SUB

    def bcast_row(ref, r):
        return jnp.broadcast_to(ref[r:r + 1, :], (half, HEAD_DIM))

    for g_i in range(n_groups):
        r0 = g_i * HG_SUB
        p0 = g_i * HG_PROWS
        c_lo, c_hi = cum_ref[r0:r0 + half, :], cum_ref[r0 + half:r0 + HG_SUB, :]
        q_lo, q_hi = q_ref[r0:r0 + half, :], q_ref[r0 + half:r0 + HG_SUB, :]
        for s in range(half):
            w_s = bcast_row(w_ref, r0 + s)
            p_lo = q_lo * jnp.exp2(c_lo - w_s + mask_ref[s * half:(s + 1) * half, :])
            p_hi = q_hi * jnp.exp2(c_hi - w_s)
            p_ref[p0 + s * HG_SUB:p0 + (s + 1) * HG_SUB, :] = (
                jnp.concatenate([p_lo, p_hi], axis=0).astype(BF16))
        for s in range(0, half, 2):
            pa = q_hi * jnp.exp2(c_hi - bcast_row(w_ref, r0 + half + s)
                                + mask_ref[s * half:(s + 1) * half, :])
            pb = q_hi * jnp.exp2(c_hi - bcast_row(w_ref, r0 + half + s + 1)
                                + mask_ref[(s + 1) * half:(s + 2) * half, :])
            base = p0 + half * HG_SUB + s * half
            p_ref[base:base + HG_SUB, :] = jnp.concatenate([pa, pb], axis=0).astype(BF16)

    sc_ref[...] = _dot(p_ref[...], ones_ref[...])

    upds = [_dot_tn(i_ref[g_i * HG_SUB:(g_i + 1) * HG_SUB, :].astype(BF16),
                    kt_ref[g_i * HG_SUB:(g_i + 1) * HG_SUB, :]) for g_i in range(n_groups)]
    st = st_ref[...]
    for g_i in range(n_groups):
        r0 = g_i * HG_SUB
        p0 = g_i * HG_PROWS
        o_lo = jnp.zeros((half, HEAD_DIM), F32)
        o_hi = jnp.zeros((half, HEAD_DIM), F32)
        for s in range(half):
            v_s = bcast_row(i_ref, r0 + s)
            o_lo = o_lo + sc_ref[p0 + s * HG_SUB:p0 + s * HG_SUB + half, :] * v_s
            o_hi = o_hi + sc_ref[p0 + s * HG_SUB + half:p0 + (s + 1) * HG_SUB, :] * v_s
        for s in range(half):
            base = p0 + half * HG_SUB + s * half
            o_hi = o_hi + sc_ref[base:base + half, :] * bcast_row(i_ref, r0 + half + s)
        o_inter = _dot_nt(qt_ref[r0:r0 + HG_SUB, :], st.astype(BF16))
        acc_ref[r0:r0 + HG_SUB, :] = o_inter + jnp.concatenate([o_lo, o_hi], axis=0)
        st = st * dd_ref[r0:r0 + 1, :] + upds[g_i]
    st_ref[...] = st

    o = acc_ref[...]
    ms = jnp.mean(o * o, axis=-1, keepdims=True)
    o = o * lax.rsqrt(ms + EPS) * ng_ref[...]
    o_ref[...] = (o * _silu(gate_ref[...])).astype(o_ref.dtype)


def _hgrn(z, hg_lb, norm_g, layer, q_off, f_off, i_off, g_off, n_heads, rows=512):
    s_len = z.shape[0]
    depth = hg_lb.shape[0]
    hps = HG_HEADS_PER_STEP
    width = hps * HEAD_DIM
    assert n_heads % hps == 0 and all(o % width == 0 for o in (q_off, f_off, i_off, g_off))
    qo, fo, io, go = (o // width for o in (q_off, f_off, i_off, g_off))
    r = np.arange(LANES)
    same = (r[:, None] // HG_SUB) == (r[None, :] // HG_SUB)
    tb = jnp.asarray(np.concatenate([same & (r[None, :] <= r[:, None]), same]).astype(np.float32), BF16)
    ones = jnp.ones((HEAD_DIM, HEAD_DIM), BF16)
    half = HG_SUB // 2
    t_idx = np.arange(half)
    mask_np = np.where(t_idx[None, :, None] >= t_idx[:, None, None], 0.0, NEG_BIG)
    mask = jnp.asarray(np.broadcast_to(mask_np, (half, half, HEAD_DIM)).reshape(half * half, HEAD_DIM), F32)
    n_prows = (rows // HG_SUB) * HG_PROWS
    kern = functools.partial(_hgrn_kernel, layer=layer, rows=rows)
    blk_spec = lambda off: pl.BlockSpec((rows, width), lambda h, i: (i, off + h))
    const = lambda shape: pl.BlockSpec(shape, lambda h, i: (0, 0))
    per_head = lambda shape, dtype: pltpu.VMEM((hps,) + shape, dtype)
    return pl.pallas_call(
        kern,
        grid=(n_heads // hps, s_len // rows),
        in_specs=[blk_spec(qo), blk_spec(fo), blk_spec(io), blk_spec(go),
                  pl.BlockSpec((depth, width), lambda h, i: (0, h)),
                  pl.BlockSpec((1, width), lambda h, i: (0, h)),
                  const((2 * LANES, LANES)), const((HEAD_DIM, HEAD_DIM)),
                  const((half * half, HEAD_DIM))],
        out_specs=pl.BlockSpec((rows, width), lambda h, i: (i, h)),
        out_shape=jax.ShapeDtypeStruct((s_len, n_heads * HEAD_DIM), BF16),
        scratch_shapes=[per_head((HEAD_DIM, HEAD_DIM), F32),
                        per_head((rows, HEAD_DIM), BF16),
                        per_head((rows, HEAD_DIM), BF16),
                        per_head((rows, HEAD_DIM), F32),
                        per_head((rows, HEAD_DIM), F32),
                        per_head((rows, HEAD_DIM), F32),
                        per_head((n_prows, HEAD_DIM), BF16),
                        per_head((n_prows, HEAD_DIM), F32),
                        per_head((rows, HEAD_DIM), F32)],
        compiler_params=_cparams(("parallel", "arbitrary")),
        name="hgrn2",
    )(z, z, z, z, hg_lb, norm_g.reshape(1, -1), tb, ones, mask)


def _gelu_tanh(x):
    c = math.sqrt(2.0 / math.pi)
    return 0.5 * x * (1.0 + jnp.tanh(c * (x + 0.044715 * (x * x * x))))


def _s5_kernel(u_ref, perm_ref, permt_ref, bm_ref, cm_ref, are_ref, aim_ref, pw_ref,
               d_ref, o_ref, state_ref, x_ref, yp_ref, *, rows, half):
    @pl.when(pl.program_id(0) == 0)
    def _():
        state_ref[...] = jnp.zeros_like(state_ref)

    nblk = bm_ref.shape[0]
    nt = rows // 8
    u = u_ref[...]
    up = _dot(perm_ref[...], u.astype(BF16)).astype(BF16)
    sub = lax.broadcasted_iota(jnp.int32, (8, half), 0)

    re, im = slice(0, half), slice(half, 2 * half)
    for b in range(nblk):
        x_ref[b] = _dot(up[:, b * LANES:(b + 1) * LANES], bm_ref[b])
    for b in range(nblk):
        are = are_ref[b]
        aim = aim_ref[b]
        xr = jnp.zeros((8, half), F32)
        xi = jnp.zeros((8, half), F32)
        for t in range(nt):
            rs = slice(t * 8, (t + 1) * 8)
            xr, xi = (are * xr - aim * xi + x_ref[b, rs, re],
                      are * xi + aim * xr + x_ref[b, rs, im])
            x_ref[b, rs, re] = xr
            x_ref[b, rs, im] = xi

        er, ei = xr, xi
        alre = pw_ref[b, rows - 1:rows, re]
        alim = pw_ref[b, rows - 1:rows, im]
        cr = state_ref[b, 0:1, re]
        ci = state_ref[b, 0:1, im]
        ctr = jnp.zeros((8, half), F32)
        cti = jnp.zeros((8, half), F32)
        for s in range(8):
            ctr = jnp.where(sub == s, cr, ctr)
            cti = jnp.where(sub == s, ci, cti)
            cr, ci = (alre * cr - alim * ci + er[s:s + 1, :],
                      alre * ci + alim * cr + ei[s:s + 1, :])
        state_ref[b, :, re] = jnp.broadcast_to(cr, (8, half))
        state_ref[b, :, im] = jnp.broadcast_to(ci, (8, half))

        for t in range(nt):
            rs = slice(t * 8, (t + 1) * 8)
            pr = pw_ref[b, rs, re]
            pi = pw_ref[b, rs, im]
            x_ref[b, rs, re] = x_ref[b, rs, re] + (pr * ctr - pi * cti)
            x_ref[b, rs, im] = x_ref[b, rs, im] + (pr * cti + pi * ctr)
        yp_ref[:, b * LANES:(b + 1) * LANES] = _dot(x_ref[b].astype(BF16), cm_ref[b])

    y = _dot01(permt_ref[...], yp_ref[...], terms=2) + d_ref[...] * u
    o_ref[...] = _gelu_tanh(y).astype(o_ref.dtype)


def _s5(z, u_off, bm, cm, are, aim, pw, d_skip, rows=256):
    s_len = z.shape[0]
    width = d_skip.shape[0]
    nblk, _, two_half = bm.shape
    half = two_half // 2
    seg = rows // 8
    rho = np.arange(rows)
    t_of = (rho % 8) * seg + rho // 8
    perm_np = np.zeros((rows, rows), np.float32)
    perm_np[rho, t_of] = 1.0
    perm = jnp.asarray(perm_np, BF16)
    permt = jnp.asarray(perm_np.T, BF16)
    uo = u_off // width
    kern = functools.partial(_s5_kernel, rows=rows, half=half)
    c2 = lambda shape: pl.BlockSpec(shape, lambda i: (0, 0))
    c3 = lambda shape: pl.BlockSpec(shape, lambda i: (0, 0, 0))
    return pl.pallas_call(
        kern,
        grid=(s_len // rows,),
        in_specs=[pl.BlockSpec((rows, width), lambda i: (i, uo)),
                  c2((rows, rows)), c2((rows, rows)),
                  c3(bm.shape), c3(cm.shape),
                  c3(are.shape), c3(aim.shape),
                  pl.BlockSpec(pw.shape, lambda i: (0, 0, 0), pipeline_mode=pl.Buffered(1)),
                  c2((1, width))],
        out_specs=pl.BlockSpec((rows, width), lambda i: (i, 0)),
        out_shape=jax.ShapeDtypeStruct((s_len, width), BF16),
        scratch_shapes=[pltpu.VMEM((nblk, 8, two_half), F32),
                        pltpu.VMEM((nblk, rows, two_half), F32),
                        pltpu.VMEM((rows, width), F32)],
        compiler_params=_cparams(("arbitrary",)),
        name="s5",
    )(z, perm, permt, bm, cm, are, aim, pw, d_skip.reshape(1, width))


def _s5_params(a_re, a_im, log_dt, b_re, b_im, c_re, c_im, seg):
    g_n, p_n = a_re.shape
    nblk = g_n // S5_GPB
    dt = jnp.exp(log_dt)[:, None]
    mag = jnp.exp(a_re * dt)
    ang = a_im * dt
    abar_re = mag * jnp.cos(ang)
    abar_im = mag * jnp.sin(ang)
    nr = abar_re - 1.0
    den = a_re * a_re + a_im * a_im
    zr = (nr * a_re + abar_im * a_im) / den
    zi = (abar_im * a_re - nr * a_im) / den
    bbar_re = zr[:, :, None] * b_re - zi[:, :, None] * b_im
    bbar_im = zr[:, :, None] * b_im + zi[:, :, None] * b_re
    same = jnp.asarray(np.eye(S5_GPB, dtype=bool))

    def embed_b(bb):
        bb = bb.reshape(nblk, S5_GPB, 1, p_n, S5_GROUP).transpose(0, 1, 4, 2, 3)
        m = jnp.where(same[None, :, None, :, None], bb, 0.0)
        return m.reshape(nblk, S5_GPB * S5_GROUP, S5_GPB * p_n)

    bm = jnp.concatenate([embed_b(bbar_re), embed_b(bbar_im)], axis=2).astype(BF16)

    def embed_c(cc):
        cc = cc.reshape(nblk, S5_GPB, S5_GROUP, 1, p_n).transpose(0, 1, 4, 3, 2)
        m = jnp.where(same[None, :, None, :, None], cc, 0.0)
        return m.reshape(nblk, S5_GPB * p_n, S5_GPB * S5_GROUP)

    cm = jnp.concatenate([embed_c(c_re), embed_c(-c_im)], axis=1).astype(BF16)

    def tile8(v):
        v = v.reshape(nblk, 1, S5_GPB * p_n)
        return jnp.broadcast_to(v, (nblk, 8, S5_GPB * p_n))

    k = jnp.arange(1, seg + 1, dtype=F32)[:, None, None]
    mag_k = jnp.exp(k * (a_re * dt))
    ang_k = k * ang

    def table(t):
        t = t.reshape(seg, 1, nblk, S5_GPB * p_n)
        return jnp.broadcast_to(t, (seg, 8, nblk, S5_GPB * p_n)).transpose(2, 0, 1, 3).reshape(
            nblk, 8 * seg, S5_GPB * p_n)

    pw = jnp.concatenate([table(mag_k * jnp.cos(ang_k)), table(mag_k * jnp.sin(ang_k))], axis=2)
    return bm, cm, tile8(abar_re), tile8(abar_im), pw


def _glu_kernel(y_ref, w_ref, gate_ref, o_ref, *, width):
    zg = _dot(y_ref[...], w_ref[...])
    o = zg[:, :width] * _sigmoid(zg[:, width:]) * _silu(gate_ref[...])
    o_ref[...] = o.astype(o_ref.dtype)


def _glu(y, w_glu, layer, z, g_off, tm=1024):
    m, width = y.shape
    go = g_off // width
    return pl.pallas_call(
        functools.partial(_glu_kernel, width=width),
        grid=(m // tm,),
        in_specs=[pl.BlockSpec((tm, width), lambda i: (i, 0)),
                  pl.BlockSpec((None, width, 2 * width), lambda i: (layer, 0, 0)),
                  pl.BlockSpec((tm, width), lambda i: (i, go))],
        out_specs=pl.BlockSpec((tm, width), lambda i: (i, 0)),
        out_shape=jax.ShapeDtypeStruct((m, width), BF16),
        compiler_params=_cparams(("parallel",)),
        name="s5_glu",
    )(y, w_glu, z)


def _merge_kernel(oa_ref, ob_ref, oc_ref, ga_ref, gb_ref, gc_ref, bg_ref, wa_ref, wb_ref, wc_ref,
                  wo_ref, x_ref, ng_ref, xo_ref, ho_ref, *, d):
    def gate(g_ref, i):
        return _sigmoid(g_ref[...] + bg_ref[:, i * d:(i + 1) * d])

    merged = gate(ga_ref, 0) * _dot(oa_ref[...], wa_ref[...])
    merged = merged + gate(gb_ref, 1) * _dot(ob_ref[...], wb_ref[...])
    merged = merged + gate(gc_ref, 2) * _dot(oc_ref[...], wc_ref[...])
    xn = x_ref[...] + _dot(merged.astype(BF16), wo_ref[...])
    xo_ref[...] = xn
    ms = jnp.mean(xn * xn, axis=-1, keepdims=True)
    ho_ref[...] = (xn * lax.rsqrt(ms + EPS) * ng_ref[...]).astype(ho_ref.dtype)


def _merge(oa, ob, oc, z, mg_off, b_gate, wa, wb, wc, wo, layer, x, next_g, h_dtype, tm=256):
    m, d = x.shape
    w = oa.shape[1]
    assert mg_off % d == 0
    mo = mg_off // d
    row = lambda width: pl.BlockSpec((tm, width), lambda i: (i, 0))
    gate = lambda k: pl.BlockSpec((tm, d), lambda i: (i, mo + k))
    resident = lambda shape: pl.BlockSpec(shape, lambda i: (0, 0), pipeline_mode=pl.Buffered(1))
    weight = lambda rows: pl.BlockSpec((None, rows, d), lambda i: (layer, 0, 0),
                                       pipeline_mode=pl.Buffered(1))
    return pl.pallas_call(
        functools.partial(_merge_kernel, d=d),
        grid=(m // tm,),
        in_specs=[row(w), row(w), row(w), gate(0), gate(1), gate(2),
                  resident((1, N_BRANCH * d)),
                  weight(w), weight(w), weight(w), weight(d),
                  row(d), resident((1, d))],
        out_specs=[row(d), row(d)],
        out_shape=[jax.ShapeDtypeStruct((m, d), F32), jax.ShapeDtypeStruct((m, d), h_dtype)],
        compiler_params=_cparams(("parallel",)),
        name="merge_out",
    )(oa, ob, oc, z, z, z, b_gate.reshape(1, -1), wa, wb, wc, wo, x, next_g.reshape(1, -1))


def kernel(x, norm_g, w_in, b_gate, fox_bf, hg_lb, hg_norm_g, s5_a_re, s5_a_im, s5_log_dt,
           s5_b_re, s5_b_im, s5_c_re, s5_c_im, s5_d, s5_w_glu, w_br_a, w_br_b, w_br_c, w_out,
           final_g):
    bsz, s_len, d = x.shape
    depth = w_in.shape[0]
    hg_w = hg_lb.shape[1]
    s5_w = s5_d.shape[1]
    n_fox = fox_bf.shape[1]
    fox_w = n_fox * HEAD_DIM
    n_hg = hg_w // HEAD_DIM
    sizes = (hg_w, hg_w, hg_w, hg_w, s5_w, s5_w, fox_w, fox_w, fox_w, n_fox, fox_w, N_BRANCH * d)
    offs = np.concatenate([[0], np.cumsum(sizes)])
    (o_hq, o_hf, o_hi, o_hg, o_su, o_sg, o_fq, o_fk, o_fv, o_ff, o_fg, o_mg, o_end) = (int(v) for v in offs)
    wt = jnp.transpose(w_in, (0, 2, 1))
    tn = fox_w
    tiles = lambda start, stop: list(range(start, stop, tn))
    o_mg2, o_fg2 = 0, o_end - o_mg
    w_glu, w_a, w_b, w_c, w_o = (w.astype(BF16) for w in (s5_w_glu, w_br_a, w_br_b, w_br_c, w_out))
    s5_rows = 256
    outs = []
    for b in range(bsz):
        xb = x[b]
        h = _rmsnorm(xb, norm_g[0], BF16)
        for l in range(depth):
            z = _inproj(h, wt, l, tiles(0, o_fq), F32, tn=tn)
            zqkv = _inproj(h, wt, l, tiles(o_fq, o_ff), BF16,
                           first_tile_scale=HEAD_DIM ** -0.5 * LOG2E, tn=tn)
            zg = _inproj(h, wt, l, tiles(o_mg, o_end) + tiles(o_fg, o_mg), F32, tn=tn)

            o_a = _hgrn(z, hg_lb, hg_norm_g[l], l, o_hq, o_hf, o_hi, o_hg, n_hg)

            s5p = _s5_params(s5_a_re[l], s5_a_im[l], s5_log_dt[l], s5_b_re[l], s5_b_im[l],
                             s5_c_re[l], s5_c_im[l], s5_rows // 8)
            y_b = _s5(z, o_su, *s5p, s5_d[l], rows=s5_rows)
            o_b = _glu(y_b, w_glu, l, z, o_sg)

            ct = _fox_cumlog(h, wt, l, o_ff, fox_bf[l])
            o_c = _fox_attn(zqkv, zg, ct[:, :, None], ct[:, None, :], o_fg2, n_fox)

            last = l == depth - 1
            next_g = final_g if last else norm_g[l + 1]
            xb, h = _merge(o_a, o_b, o_c, zg, o_mg2, b_gate[l], w_a, w_b, w_c, w_o, l,
                           xb, next_g, F32 if last else BF16)
        outs.append(h)
    return outs[0][None] if bsz == 1 else jnp.stack(outs, axis=0)
```

```python
import functools
import math

import numpy as np
import jax
import jax.numpy as jnp
from jax import lax
from jax.experimental import pallas as pl
from jax.experimental.pallas import tpu as pltpu

F32 = jnp.float32
BF16 = jnp.bfloat16

EPS = 1e-6
LANES = 128
VMEM_LIMIT = 56 * 1024 * 1024

HEAD_DIM = 128
S5_GROUP = 16
S5_STATE = 64
S5_GPB = 8
N_BRANCH = 3

HG_SUB = 16


def _cparams(sem, vmem=VMEM_LIMIT):
    return pltpu.CompilerParams(dimension_semantics=sem, vmem_limit_bytes=vmem)


def _dot(a, b):
    return jnp.dot(a, b, preferred_element_type=F32)


def _dot_nt(a, b):
    return lax.dot_general(a, b, (((1,), (1,)), ((), ())), preferred_element_type=F32)


def _dot_tn(a, b):
    return lax.dot_general(a, b, (((0,), (0,)), ((), ())), preferred_element_type=F32)


def _split3(x):
    hi = x.astype(BF16)
    r1 = x - hi.astype(F32)
    mid = r1.astype(BF16)
    lo = (r1 - mid.astype(F32)).astype(BF16)
    return hi, mid, lo


def _dot01(m, x, terms=3):
    parts = _split3(x)[:terms]
    out = _dot(m, parts[0])
    for part in parts[1:]:
        out = out + _dot(m, part)
    return out


def _log_sigmoid(z):
    return jnp.minimum(z, 0.0) - jnp.log1p(jnp.exp(-jnp.abs(z)))


def _sigmoid(z):
    return 1.0 / (1.0 + jnp.exp(-z))


def _silu(z):
    return z * _sigmoid(z)


def _rmsnorm_kernel(x_ref, g_ref, o_ref):
    x = x_ref[...]
    ms = jnp.mean(x * x, axis=-1, keepdims=True)
    o_ref[...] = (x * lax.rsqrt(ms + EPS) * g_ref[...]).astype(o_ref.dtype)


def _rmsnorm(x, g, out_dtype, tm=1024):
    m, d = x.shape
    return pl.pallas_call(
        _rmsnorm_kernel,
        grid=(m // tm,),
        in_specs=[pl.BlockSpec((tm, d), lambda i: (i, 0)),
                  pl.BlockSpec((1, d), lambda i: (0, 0))],
        out_specs=pl.BlockSpec((tm, d), lambda i: (i, 0)),
        out_shape=jax.ShapeDtypeStruct((m, d), out_dtype),
        compiler_params=_cparams(("parallel",)),
        name="rmsnorm",
    )(x, g.reshape(1, d))


def _inproj_kernel(starts_ref, h_ref, w_ref, o_ref, wb_ref, *, first_tile_scale):
    del starts_ref
    @pl.when(pl.program_id(1) == 0)
    def _():
        wb_ref[...] = w_ref[0].T.astype(BF16)

    acc = _dot(h_ref[...], wb_ref[...])
    if first_tile_scale is not None:
        acc = acc * jnp.where(pl.program_id(0) == 0, first_tile_scale, 1.0)
    o_ref[...] = acc.astype(o_ref.dtype)


def _inproj(h, wt, layer, row_starts, out_dtype, first_tile_scale=None, tm=1024, tn=1024):
    m, k = h.shape
    n_tiles = len(row_starts)
    assert all(r % 8 == 0 for r in row_starts)
    starts = jnp.asarray(np.asarray(row_starts, np.int32) // 8)
    grid_spec = pltpu.PrefetchScalarGridSpec(
        num_scalar_prefetch=1,
        grid=(n_tiles, m // tm),
        in_specs=[pl.BlockSpec((tm, k), lambda j, i, st: (i, 0)),
                  pl.BlockSpec((pl.Element(1), pl.Element(tn), pl.Element(k)),
                               lambda j, i, st: (layer, st[j] * 8, 0))],
        out_specs=pl.BlockSpec((tm, tn), lambda j, i, st: (i, j)),
        scratch_shapes=[pltpu.VMEM((k, tn), BF16)])
    return pl.pallas_call(
        functools.partial(_inproj_kernel, first_tile_scale=first_tile_scale),
        grid_spec=grid_spec,
        out_shape=jax.ShapeDtypeStruct((m, n_tiles * tn), out_dtype),
        compiler_params=_cparams(("parallel", "arbitrary")),
        name="inproj",
    )(starts, h, wt)


def _foxc_kernel(h_ref, w_ref, b_ref, triu_ref, c_ref, carry_ref):
    @pl.when(pl.program_id(0) == 0)
    def _():
        carry_ref[...] = jnp.zeros_like(carry_ref)

    logits = _dot_nt(w_ref[...].astype(BF16), h_ref[...]) + b_ref[...]
    hi, mid, lo = _split3(_log_sigmoid(logits))
    tri = triu_ref[...]
    cum = _dot(hi, tri) + _dot(mid, tri) + _dot(lo, tri) + carry_ref[:, 0:1]
    c_ref[...] = cum
    tm = cum.shape[1]
    carry_ref[...] = jnp.broadcast_to(cum[:, tm - 1:tm], carry_ref.shape)


def _fox_cumlog(h, wt, layer, row0, b_ff, tm=512):
    m, k = h.shape
    n_heads = b_ff.shape[0]
    assert row0 % n_heads == 0
    triu = jnp.asarray(np.triu(np.ones((tm, tm), np.float32)), BF16)
    return pl.pallas_call(
        _foxc_kernel,
        grid=(m // tm,),
        in_specs=[pl.BlockSpec((tm, k), lambda i: (i, 0)),
                  pl.BlockSpec((None, n_heads, k), lambda i: (layer, row0 // n_heads, 0)),
                  pl.BlockSpec((n_heads, 1), lambda i: (0, 0)),
                  pl.BlockSpec((tm, tm), lambda i: (0, 0))],
        out_specs=pl.BlockSpec((n_heads, tm), lambda i: (0, i)),
        out_shape=jax.ShapeDtypeStruct((n_heads, m), F32),
        scratch_shapes=[pltpu.VMEM((n_heads, LANES), F32)],
        compiler_params=_cparams(("arbitrary",)),
        name="fox_cumlog",
    )(h, wt, b_ff.reshape(n_heads, 1), triu)


NEG_BIG = -1e30


LOG2E = 1.0 / math.log(2.0)
ATT_PASS_ROWS = 32
ATT_XPOSE_ROWS = 512
ATT_WIDTHS = (2, 1)
ATT_UNROLL = 2


def _fox_attn_kernel(tab_ref, q_ref, k_ref, v_ref, crow_ref, gate_ref, o_ref,
                     va_ref, kt_ref, sa_ref, sb_ref, pa_ref, pb_ref, ala_ref, alb_ref, acc_ref, m_ref,
                     ct_ref, *, tq, runs):
    @pl.when(pl.program_id(0) == 0)
    def _():
        va_ref[:, HEAD_DIM:2 * HEAD_DIM] = jnp.ones((va_ref.shape[0], HEAD_DIM), BF16)

    va_ref[:, 0:HEAD_DIM] = v_ref[...]
    for r0 in range(0, k_ref.shape[0], ATT_XPOSE_ROWS):
        kt_ref[:, r0:r0 + ATT_XPOSE_ROWS] = (
            k_ref[r0:r0 + ATT_XPOSE_ROWS, :].astype(F32).T.astype(BF16))
    for r0 in range(0, k_ref.shape[0], ATT_XPOSE_ROWS):
        row = crow_ref[0, :, r0:r0 + ATT_XPOSE_ROWS] * LOG2E
        ct_ref[r0:r0 + ATT_XPOSE_ROWS, :] = jnp.broadcast_to(row, (8, ATT_XPOSE_ROWS)).T[:, 0:1]

    def tile(n):
        q0 = pl.multiple_of(tab_ref[0, n] * tq, tq)
        k0 = pl.multiple_of(tab_ref[1, n] * tq, tq)
        return q0, k0

    def logits(n, s_ref, tk):
        q0, k0 = tile(n)
        s_ref[:, 0:tk] = _dot(q_ref[pl.ds(q0, tq), :], kt_ref[:, pl.ds(k0, tk)])

    def softmax(n, s_ref, p_ref, al_ref, masked, tk):
        q0, k0 = tile(n)
        crow = crow_ref[0, :, pl.ds(k0, tk)] * LOG2E
        rows = ATT_PASS_ROWS
        for r in range(tq // rows):
            rs = slice(r * rows, (r + 1) * rows)
            qs = pl.ds(q0 + r * rows, rows)
            s = s_ref[rs, 0:tk] - crow
            if masked:
                row = lax.broadcasted_iota(jnp.int32, (rows, tk), 0) + r * rows
                col = lax.broadcasted_iota(jnp.int32, (rows, tk), 1)
                s = jnp.where(col <= row, s, NEG_BIG)
            ct = ct_ref[qs, :]
            m_new = jnp.max(s, axis=1, keepdims=True) + ct
            if not masked:
                m_old = m_ref[qs, :]
                m_new = jnp.maximum(m_old, m_new)
                al_ref[rs, :] = jnp.exp2(m_old - m_new)
            p_ref[rs, 0:tk] = jnp.exp2(s - (m_new - ct)).astype(BF16)
            m_ref[qs, :] = m_new

    def accumulate(n, p_ref, al_ref, masked, tk):
        q0, k0 = tile(n)
        qs = pl.ds(q0, tq)
        pv = _dot(p_ref[:, 0:tk], va_ref[pl.ds(k0, tk), :])
        acc_ref[qs, :] = pv if masked else al_ref[...] * acc_ref[qs, :] + pv

    def run(first, count, masked, tk):
        if count == 0:
            return
        last = first + count - 1
        nxt = lambda n: jnp.minimum(n, last)
        s_buf = (sa_ref, sb_ref)
        p_buf = ((pa_ref, ala_ref), (pb_ref, alb_ref))
        logits(first, s_buf[0], tk)
        softmax(first, s_buf[0], *p_buf[0], masked, tk)
        logits(nxt(first + 1), s_buf[1], tk)
        n_loop = (count - 1) // ATT_UNROLL

        def body(j, carry):
            n = first + ATT_UNROLL * j
            for u in range(ATT_UNROLL):
                accumulate(n + u, *p_buf[u % 2], masked, tk)
                softmax(n + u + 1, s_buf[(u + 1) % 2], *p_buf[(u + 1) % 2], masked, tk)
                logits(nxt(n + u + 2), s_buf[u % 2], tk)
            return carry

        lax.fori_loop(0, n_loop, body, 0)
        n = first + ATT_UNROLL * n_loop
        rest = count - 1 - ATT_UNROLL * n_loop
        for u in range(rest + 1):
            accumulate(n + u, *p_buf[u % 2], masked, tk)
            if u + 1 <= rest:
                softmax(n + u + 1, s_buf[(u + 1) % 2], *p_buf[(u + 1) % 2], masked, tk)
            if u + 2 <= rest:
                logits(n + u + 2, s_buf[u % 2], tk)

    for first, count, masked, tk in runs:
        run(first, count, masked, tk)

    acc = acc_ref[...]
    out = acc[:, 0:HEAD_DIM] / acc[:, HEAD_DIM:2 * HEAD_DIM]
    o_ref[...] = (out * _silu(gate_ref[...])).astype(o_ref.dtype)


def _fox_attn(zqkv, z, crow, g_off, n_heads, tq=512):
    s_len = zqkv.shape[0]
    nq = s_len // tq
    go = g_off // HEAD_DIM
    tiles = [(i, i) for i in range(nq)]
    runs = [(0, nq, True, tq)]
    for w in ATT_WIDTHS:
        cls = []
        for qi in range(nq):
            k = 0
            for w2 in ATT_WIDTHS:
                n_w2 = (qi - k) // w2
                if w2 == w:
                    cls += [(qi, k + i * w) for i in range(n_w2)]
                k += n_w2 * w2
        cls.sort(key=lambda t: (t[1], t[0]))
        runs.append((len(tiles), len(cls), False, w * tq))
        tiles += cls
    tab = jnp.asarray(np.array(tiles, np.int32).reshape(-1, 2).T)
    wmax = max(ATT_WIDTHS) * tq
    kern = functools.partial(_fox_attn_kernel, tq=tq, runs=tuple(runs))
    once = pl.Buffered(1)
    head_col = lambda base: pl.BlockSpec((s_len, HEAD_DIM), lambda h, t: (0, base + h),
                                         pipeline_mode=once)
    grid_spec = pltpu.PrefetchScalarGridSpec(
        num_scalar_prefetch=1,
        grid=(n_heads,),
        in_specs=[head_col(0), head_col(n_heads), head_col(2 * n_heads),
                  pl.BlockSpec((1, 1, s_len), lambda h, t: (h, 0, 0)),
                  head_col(go)],
        out_specs=pl.BlockSpec((s_len, HEAD_DIM), lambda h, t: (0, h)),
        scratch_shapes=[pltpu.VMEM((s_len, 2 * HEAD_DIM), BF16),
                        pltpu.VMEM((HEAD_DIM, s_len), BF16),
                        pltpu.VMEM((tq, wmax), F32),
                        pltpu.VMEM((tq, wmax), F32),
                        pltpu.VMEM((tq, wmax), BF16),
                        pltpu.VMEM((tq, wmax), BF16),
                        pltpu.VMEM((tq, 1), F32),
                        pltpu.VMEM((tq, 1), F32),
                        pltpu.VMEM((s_len, 2 * HEAD_DIM), F32),
                        pltpu.VMEM((s_len, 1), F32),
                        pltpu.VMEM((s_len, 1), F32)])
    return pl.pallas_call(
        kern,
        grid_spec=grid_spec,
        out_shape=jax.ShapeDtypeStruct((s_len, n_heads * HEAD_DIM), BF16),
        compiler_params=_cparams(("arbitrary",)),
        name="fox_attn",
    )(tab, zqkv, zqkv, zqkv, crow, z)


HG_PROWS = 8 * HG_SUB + 8 * (HG_SUB // 2)


HG_HEADS_PER_STEP = 4


def _hgrn_kernel(q_ref, f_ref, i_ref, gate_ref, lb_ref, ng_ref, tb_ref, ones_ref, mask_ref, o_ref,
                 *scratch, layer, rows):
    @pl.when(pl.program_id(1) == 0)
    def _():
        scratch[0][...] = jnp.zeros_like(scratch[0])

    for hh in range(HG_HEADS_PER_STEP):
        lanes = pl.ds(hh * HEAD_DIM, HEAD_DIM)
        _hgrn_head(*(r.at[:, lanes] for r in (q_ref, f_ref, i_ref, gate_ref, lb_ref, ng_ref)),
                   tb_ref, ones_ref, mask_ref, o_ref.at[:, lanes], *(s.at[hh] for s in scratch),
                   layer=layer, rows=rows)


def _hgrn_head(q_ref, f_ref, i_ref, gate_ref, lb_ref, ng_ref, tb_ref, ones_ref,
               mask_ref, o_ref, st_ref, qt_ref, kt_ref, w_ref, cum_ref, dd_ref, p_ref, sc_ref,
               acc_ref, *, layer, rows):
    z = f_ref[...]
    ls = _log_sigmoid(z)
    if layer == 0:
        g = ls
        logk = ls - z
    else:
        lbp = lb_ref[...]
        e = jnp.exp(lbp - jnp.max(lbp, axis=0, keepdims=True))
        p = e / jnp.sum(e, axis=0, keepdims=True)
        lb = jnp.sum(p[1:layer + 1, :], axis=0, keepdims=True)
        a = jnp.log(lb)
        l1m = jnp.log1p(-lb)
        b = l1m + ls
        g = jnp.maximum(a, b) + jnp.log1p(jnp.exp(-jnp.abs(a - b)))
        logk = l1m + (ls - z)

    cums, tots = [], []
    for r0 in range(0, rows, LANES):
        ct = _dot01(tb_ref[...], g[r0:r0 + LANES, :])
        cums.append(ct[0:LANES, :])
        tots.append(ct[LANES:2 * LANES, :])
    cum = jnp.concatenate(cums, axis=0) * LOG2E
    tot = jnp.concatenate(tots, axis=0) * LOG2E
    w = cum - logk * LOG2E
    qt_ref[...] = (q_ref[...] * jnp.exp2(cum)).astype(BF16)
    kt_ref[...] = jnp.exp2(tot - w).astype(BF16)
    w_ref[...] = w
    cum_ref[...] = cum
    dd_ref[...] = jnp.exp2(tot)

    half = HG_SUB // 2
    n_groups = rows // HG_SUB

    def bcast_row(ref, r):
        return jnp.broadcast_to(ref[r:r + 1, :], (half, HEAD_DIM))

    for g_i in range(n_groups):
        r0 = g_i * HG_SUB
        p0 = g_i * HG_PROWS
        c_lo, c_hi = cum_ref[r0:r0 + half, :], cum_ref[r0 + half:r0 + HG_SUB, :]
        q_lo, q_hi = q_ref[r0:r0 + half, :], q_ref[r0 + half:r0 + HG_SUB, :]
        for s in range(half):
            w_s = bcast_row(w_ref, r0 + s)
            p_lo = q_lo * jnp.exp2(c_lo - w_s + mask_ref[s * half:(s + 1) * half, :])
            p_hi = q_hi * jnp.exp2(c_hi - w_s)
            p_ref[p0 + s * HG_SUB:p0 + (s + 1) * HG_SUB, :] = (
                jnp.concatenate([p_lo, p_hi], axis=0).astype(BF16))
        for s in range(0, half, 2):
            pa = q_hi * jnp.exp2(c_hi - bcast_row(w_ref, r0 + half + s)
                                + mask_ref[s * half:(s + 1) * half, :])
            pb = q_hi * jnp.exp2(c_hi - bcast_row(w_ref, r0 + half + s + 1)
                                + mask_ref[(s + 1) * half:(s + 2) * half, :])
            base = p0 + half * HG_SUB + s * half
            p_ref[base:base + HG_SUB, :] = jnp.concatenate([pa, pb], axis=0).astype(BF16)

    sc_ref[...] = _dot(p_ref[...], ones_ref[...])

    upds = [_dot_tn(i_ref[g_i * HG_SUB:(g_i + 1) * HG_SUB, :].astype(BF16),
                    kt_ref[g_i * HG_SUB:(g_i + 1) * HG_SUB, :]) for g_i in range(n_groups)]
    st = st_ref[...]
    for g_i in range(n_groups):
        r0 = g_i * HG_SUB
        p0 = g_i * HG_PROWS
        o_lo = jnp.zeros((half, HEAD_DIM), F32)
        o_hi = jnp.zeros((half, HEAD_DIM), F32)
        for s in range(half):
            v_s = bcast_row(i_ref, r0 + s)
            o_lo = o_lo + sc_ref[p0 + s * HG_SUB:p0 + s * HG_SUB + half, :] * v_s
            o_hi = o_hi + sc_ref[p0 + s * HG_SUB + half:p0 + (s + 1) * HG_SUB, :] * v_s
        for s in range(half):
            base = p0 + half * HG_SUB + s * half
            o_hi = o_hi + sc_ref[base:base + half, :] * bcast_row(i_ref, r0 + half + s)
        o_inter = _dot_nt(qt_ref[r0:r0 + HG_SUB, :], st.astype(BF16))
        acc_ref[r0:r0 + HG_SUB, :] = o_inter + jnp.concatenate([o_lo, o_hi], axis=0)
        st = st * dd_ref[r0:r0 + 1, :] + upds[g_i]
    st_ref[...] = st

    o = acc_ref[...]
    ms = jnp.mean(o * o, axis=-1, keepdims=True)
    o = o * lax.rsqrt(ms + EPS) * ng_ref[...]
    o_ref[...] = (o * _silu(gate_ref[...])).astype(o_ref.dtype)


def _hgrn(z, hg_lb, norm_g, layer, q_off, f_off, i_off, g_off, n_heads, rows=512):
    s_len = z.shape[0]
    depth = hg_lb.shape[0]
    hps = HG_HEADS_PER_STEP
    width = hps * HEAD_DIM
    assert n_heads % hps == 0 and all(o % width == 0 for o in (q_off, f_off, i_off, g_off))
    qo, fo, io, go = (o // width for o in (q_off, f_off, i_off, g_off))
    r = np.arange(LANES)
    same = (r[:, None] // HG_SUB) == (r[None, :] // HG_SUB)
    tb = jnp.asarray(np.concatenate([same & (r[None, :] <= r[:, None]), same]).astype(np.float32), BF16)
    ones = jnp.ones((HEAD_DIM, HEAD_DIM), BF16)
    half = HG_SUB // 2
    t_idx = np.arange(half)
    mask_np = np.where(t_idx[None, :, None] >= t_idx[:, None, None], 0.0, NEG_BIG)
    mask = jnp.asarray(np.broadcast_to(mask_np, (half, half, HEAD_DIM)).reshape(half * half, HEAD_DIM), F32)
    n_prows = (rows // HG_SUB) * HG_PROWS
    kern = functools.partial(_hgrn_kernel, layer=layer, rows=rows)
    blk_spec = lambda off: pl.BlockSpec((rows, width), lambda h, i: (i, off + h))
    const = lambda shape: pl.BlockSpec(shape, lambda h, i: (0, 0))
    per_head = lambda shape, dtype: pltpu.VMEM((hps,) + shape, dtype)
    return pl.pallas_call(
        kern,
        grid=(n_heads // hps, s_len // rows),
        in_specs=[blk_spec(qo), blk_spec(fo), blk_spec(io), blk_spec(go),
                  pl.BlockSpec((depth, width), lambda h, i: (0, h)),
                  pl.BlockSpec((1, width), lambda h, i: (0, h)),
                  const((2 * LANES, LANES)), const((HEAD_DIM, HEAD_DIM)),
                  const((half * half, HEAD_DIM))],
        out_specs=pl.BlockSpec((rows, width), lambda h, i: (i, h)),
        out_shape=jax.ShapeDtypeStruct((s_len, n_heads * HEAD_DIM), BF16),
        scratch_shapes=[per_head((HEAD_DIM, HEAD_DIM), F32),
                        per_head((rows, HEAD_DIM), BF16),
                        per_head((rows, HEAD_DIM), BF16),
                        per_head((rows, HEAD_DIM), F32),
                        per_head((rows, HEAD_DIM), F32),
                        per_head((rows, HEAD_DIM), F32),
                        per_head((n_prows, HEAD_DIM), BF16),
                        per_head((n_prows, HEAD_DIM), F32),
                        per_head((rows, HEAD_DIM), F32)],
        compiler_params=_cparams(("parallel", "arbitrary")),
        name="hgrn2",
    )(z, z, z, z, hg_lb, norm_g.reshape(1, -1), tb, ones, mask)


def _gelu_tanh(x):
    c = math.sqrt(2.0 / math.pi)
    return 0.5 * x * (1.0 + jnp.tanh(c * (x + 0.044715 * (x * x * x))))


def _s5_kernel(u_ref, perm_ref, permt_ref, bm_ref, cm_ref, are_ref, aim_ref, pw_ref,
               d_ref, o_ref, state_ref, x_ref, yp_ref, *, rows, half):
    @pl.when(pl.program_id(0) == 0)
    def _():
        state_ref[...] = jnp.zeros_like(state_ref)

    nblk = bm_ref.shape[0]
    nt = rows // 8
    u = u_ref[...]
    up = _dot(perm_ref[...], u.astype(BF16)).astype(BF16)
    sub = lax.broadcasted_iota(jnp.int32, (8, half), 0)

    re, im = slice(0, half), slice(half, 2 * half)
    for b in range(nblk):
        x_ref[b] = _dot(up[:, b * LANES:(b + 1) * LANES], bm_ref[b])
    for b in range(nblk):
        are = are_ref[b]
        aim = aim_ref[b]
        xr = jnp.zeros((8, half), F32)
        xi = jnp.zeros((8, half), F32)
        for t in range(nt):
            rs = slice(t * 8, (t + 1) * 8)
            xr, xi = (are * xr - aim * xi + x_ref[b, rs, re],
                      are * xi + aim * xr + x_ref[b, rs, im])
            x_ref[b, rs, re] = xr
            x_ref[b, rs, im] = xi

        er, ei = xr, xi
        alre = pw_ref[b, rows - 1:rows, re]
        alim = pw_ref[b, rows - 1:rows, im]
        cr = state_ref[b, 0:1, re]
        ci = state_ref[b, 0:1, im]
        ctr = jnp.zeros((8, half), F32)
        cti = jnp.zeros((8, half), F32)
        for s in range(8):
            ctr = jnp.where(sub == s, cr, ctr)
            cti = jnp.where(sub == s, ci, cti)
            cr, ci = (alre * cr - alim * ci + er[s:s + 1, :],
                      alre * ci + alim * cr + ei[s:s + 1, :])
        state_ref[b, :, re] = jnp.broadcast_to(cr, (8, half))
        state_ref[b, :, im] = jnp.broadcast_to(ci, (8, half))

        for t in range(nt):
            rs = slice(t * 8, (t + 1) * 8)
            pr = pw_ref[b, rs, re]
            pi = pw_ref[b, rs, im]
            x_ref[b, rs, re] = x_ref[b, rs, re] + (pr * ctr - pi * cti)
            x_ref[b, rs, im] = x_ref[b, rs, im] + (pr * cti + pi * ctr)
        yp_ref[:, b * LANES:(b + 1) * LANES] = _dot(x_ref[b].astype(BF16), cm_ref[b])

    y = _dot01(permt_ref[...], yp_ref[...], terms=2) + d_ref[...] * u
    o_ref[...] = _gelu_tanh(y).astype(o_ref.dtype)


def _s5(z, u_off, bm, cm, are, aim, pw, d_skip, rows=256):
    s_len = z.shape[0]
    width = d_skip.shape[0]
    nblk, _, two_half = bm.shape
    half = two_half // 2
    seg = rows // 8
    rho = np.arange(rows)
    t_of = (rho % 8) * seg + rho // 8
    perm_np = np.zeros((rows, rows), np.float32)
    perm_np[rho, t_of] = 1.0
    perm = jnp.asarray(perm_np, BF16)
    permt = jnp.asarray(perm_np.T, BF16)
    uo = u_off // width
    kern = functools.partial(_s5_kernel, rows=rows, half=half)
    c2 = lambda shape: pl.BlockSpec(shape, lambda i: (0, 0))
    c3 = lambda shape: pl.BlockSpec(shape, lambda i: (0, 0, 0))
    return pl.pallas_call(
        kern,
        grid=(s_len // rows,),
        in_specs=[pl.BlockSpec((rows, width), lambda i: (i, uo)),
                  c2((rows, rows)), c2((rows, rows)),
                  c3(bm.shape), c3(cm.shape),
                  c3(are.shape), c3(aim.shape),
                  pl.BlockSpec(pw.shape, lambda i: (0, 0, 0), pipeline_mode=pl.Buffered(1)),
                  c2((1, width))],
        out_specs=pl.BlockSpec((rows, width), lambda i: (i, 0)),
        out_shape=jax.ShapeDtypeStruct((s_len, width), BF16),
        scratch_shapes=[pltpu.VMEM((nblk, 8, two_half), F32),
                        pltpu.VMEM((nblk, rows, two_half), F32),
                        pltpu.VMEM((rows, width), F32)],
        compiler_params=_cparams(("arbitrary",)),
        name="s5",
    )(z, perm, permt, bm, cm, are, aim, pw, d_skip.reshape(1, width))


def _s5_params(a_re, a_im, log_dt, b_re, b_im, c_re, c_im, seg):
    g_n, p_n = a_re.shape
    nblk = g_n // S5_GPB
    dt = jnp.exp(log_dt)[:, None]
    mag = jnp.exp(a_re * dt)
    ang = a_im * dt
    abar_re = mag * jnp.cos(ang)
    abar_im = mag * jnp.sin(ang)
    nr = abar_re - 1.0
    den = a_re * a_re + a_im * a_im
    zr = (nr * a_re + abar_im * a_im) / den
    zi = (abar_im * a_re - nr * a_im) / den
    bbar_re = zr[:, :, None] * b_re - zi[:, :, None] * b_im
    bbar_im = zr[:, :, None] * b_im + zi[:, :, None] * b_re
    same = jnp.asarray(np.eye(S5_GPB, dtype=bool))

    def embed_b(bb):
        bb = bb.reshape(nblk, S5_GPB, 1, p_n, S5_GROUP).transpose(0, 1, 4, 2, 3)
        m = jnp.where(same[None, :, None, :, None], bb, 0.0)
        return m.reshape(nblk, S5_GPB * S5_GROUP, S5_GPB * p_n)

    bm = jnp.concatenate([embed_b(bbar_re), embed_b(bbar_im)], axis=2).astype(BF16)

    def embed_c(cc):
        cc = cc.reshape(nblk, S5_GPB, S5_GROUP, 1, p_n).transpose(0, 1, 4, 3, 2)
        m = jnp.where(same[None, :, None, :, None], cc, 0.0)
        return m.reshape(nblk, S5_GPB * p_n, S5_GPB * S5_GROUP)

    cm = jnp.concatenate([embed_c(c_re), embed_c(-c_im)], axis=1).astype(BF16)

    def tile8(v):
        v = v.reshape(nblk, 1, S5_GPB * p_n)
        return jnp.broadcast_to(v, (nblk, 8, S5_GPB * p_n))

    k = jnp.arange(1, seg + 1, dtype=F32)[:, None, None]
    mag_k = jnp.exp(k * (a_re * dt))
    ang_k = k * ang

    def table(t):
        t = t.reshape(seg, 1, nblk, S5_GPB * p_n)
        return jnp.broadcast_to(t, (seg, 8, nblk, S5_GPB * p_n)).transpose(2, 0, 1, 3).reshape(
            nblk, 8 * seg, S5_GPB * p_n)

    pw = jnp.concatenate([table(mag_k * jnp.cos(ang_k)), table(mag_k * jnp.sin(ang_k))], axis=2)
    return bm, cm, tile8(abar_re), tile8(abar_im), pw


def _glu_kernel(y_ref, w_ref, gate_ref, o_ref, *, width):
    zg = _dot(y_ref[...], w_ref[...])
    o = zg[:, :width] * _sigmoid(zg[:, width:]) * _silu(gate_ref[...])
    o_ref[...] = o.astype(o_ref.dtype)


def _glu(y, w_glu, layer, z, g_off, tm=1024):
    m, width = y.shape
    go = g_off // width
    return pl.pallas_call(
        functools.partial(_glu_kernel, width=width),
        grid=(m // tm,),
        in_specs=[pl.BlockSpec((tm, width), lambda i: (i, 0)),
                  pl.BlockSpec((None, width, 2 * width), lambda i: (layer, 0, 0)),
                  pl.BlockSpec((tm, width), lambda i: (i, go))],
        out_specs=pl.BlockSpec((tm, width), lambda i: (i, 0)),
        out_shape=jax.ShapeDtypeStruct((m, width), BF16),
        compiler_params=_cparams(("parallel",)),
        name="s5_glu",
    )(y, w_glu, z)


def _merge_kernel(oa_ref, ob_ref, oc_ref, ga_ref, gb_ref, gc_ref, bg_ref, wa_ref, wb_ref, wc_ref,
                  wo_ref, x_ref, ng_ref, xo_ref, ho_ref, *, d):
    def gate(g_ref, i):
        return _sigmoid(g_ref[...] + bg_ref[:, i * d:(i + 1) * d])

    merged = gate(ga_ref, 0) * _dot(oa_ref[...], wa_ref[...])
    merged = merged + gate(gb_ref, 1) * _dot(ob_ref[...], wb_ref[...])
    merged = merged + gate(gc_ref, 2) * _dot(oc_ref[...], wc_ref[...])
    xn = x_ref[...] + _dot(merged.astype(BF16), wo_ref[...])
    xo_ref[...] = xn
    ms = jnp.mean(xn * xn, axis=-1, keepdims=True)
    ho_ref[...] = (xn * lax.rsqrt(ms + EPS) * ng_ref[...]).astype(ho_ref.dtype)


def _merge(oa, ob, oc, z, mg_off, b_gate, wa, wb, wc, wo, layer, x, next_g, h_dtype, tm=256):
    m, d = x.shape
    w = oa.shape[1]
    assert mg_off % d == 0
    mo = mg_off // d
    row = lambda width: pl.BlockSpec((tm, width), lambda i: (i, 0))
    gate = lambda k: pl.BlockSpec((tm, d), lambda i: (i, mo + k))
    resident = lambda shape: pl.BlockSpec(shape, lambda i: (0, 0), pipeline_mode=pl.Buffered(1))
    weight = lambda rows: pl.BlockSpec((None, rows, d), lambda i: (layer, 0, 0),
                                       pipeline_mode=pl.Buffered(1))
    return pl.pallas_call(
        functools.partial(_merge_kernel, d=d),
        grid=(m // tm,),
        in_specs=[row(w), row(w), row(w), gate(0), gate(1), gate(2),
                  resident((1, N_BRANCH * d)),
                  weight(w), weight(w), weight(w), weight(d),
                  row(d), resident((1, d))],
        out_specs=[row(d), row(d)],
        out_shape=[jax.ShapeDtypeStruct((m, d), F32), jax.ShapeDtypeStruct((m, d), h_dtype)],
        compiler_params=_cparams(("parallel",)),
        name="merge_out",
    )(oa, ob, oc, z, z, z, b_gate.reshape(1, -1), wa, wb, wc, wo, x, next_g.reshape(1, -1))


def kernel(x, norm_g, w_in, b_gate, fox_bf, hg_lb, hg_norm_g, s5_a_re, s5_a_im, s5_log_dt,
           s5_b_re, s5_b_im, s5_c_re, s5_c_im, s5_d, s5_w_glu, w_br_a, w_br_b, w_br_c, w_out,
           final_g):
    bsz, s_len, d = x.shape
    depth = w_in.shape[0]
    hg_w = hg_lb.shape[1]
    s5_w = s5_d.shape[1]
    n_fox = fox_bf.shape[1]
    fox_w = n_fox * HEAD_DIM
    n_hg = hg_w // HEAD_DIM
    sizes = (hg_w, hg_w, hg_w, hg_w, s5_w, s5_w, fox_w, fox_w, fox_w, n_fox, fox_w, N_BRANCH * d)
    offs = np.concatenate([[0], np.cumsum(sizes)])
    (o_hq, o_hf, o_hi, o_hg, o_su, o_sg, o_fq, o_fk, o_fv, o_ff, o_fg, o_mg, o_end) = (int(v) for v in offs)
    wt = jnp.transpose(w_in, (0, 2, 1))
    tn = fox_w
    tiles = lambda start, stop: list(range(start, stop, tn))
    o_mg2, o_fg2 = 0, o_end - o_mg
    w_glu, w_a, w_b, w_c, w_o = (w.astype(BF16) for w in (s5_w_glu, w_br_a, w_br_b, w_br_c, w_out))
    s5_rows = 256
    outs = []
    for b in range(bsz):
        xb = x[b]
        h = _rmsnorm(xb, norm_g[0], BF16)
        for l in range(depth):
            z = _inproj(h, wt, l, tiles(0, o_fq), F32, tn=tn)
            zqkv = _inproj(h, wt, l, tiles(o_fq, o_ff), BF16,
                           first_tile_scale=HEAD_DIM ** -0.5 * LOG2E, tn=tn)
            zg = _inproj(h, wt, l, tiles(o_mg, o_end) + tiles(o_fg, o_mg), F32, tn=tn)

            o_a = _hgrn(z, hg_lb, hg_norm_g[l], l, o_hq, o_hf, o_hi, o_hg, n_hg)

            s5p = _s5_params(s5_a_re[l], s5_a_im[l], s5_log_dt[l], s5_b_re[l], s5_b_im[l],
                             s5_c_re[l], s5_c_im[l], s5_rows // 8)
            y_b = _s5(z, o_su, *s5p, s5_d[l], rows=s5_rows)
            o_b = _glu(y_b, w_glu, l, z, o_sg)

            ct = _fox_cumlog(h, wt, l, o_ff, fox_bf[l])
            o_c = _fox_attn(zqkv, zg, ct[:, None, :], o_fg2, n_fox)

            last = l == depth - 1
            next_g = final_g if last else norm_g[l + 1]
            xb, h = _merge(o_a, o_b, o_c, zg, o_mg2, b_gate[l], w_a, w_b, w_c, w_o, l,
                           xb, next_g, F32 if last else BF16)
        outs.append(h)
    return outs[0][None] if bsz == 1 else jnp.stack(outs, axis=0)
```

```python
import functools
import math

import numpy as np
import jax
import jax.numpy as jnp
from jax import lax
from jax.experimental import pallas as pl
from jax.experimental.pallas import tpu as pltpu

F32 = jnp.float32
BF16 = jnp.bfloat16

EPS = 1e-6
LANES = 128
VMEM_LIMIT = 56 * 1024 * 1024

HEAD_DIM = 128
S5_GROUP = 16
S5_STATE = 64
S5_GPB = 8
N_BRANCH = 3

HG_SUB = 16


def _cparams(sem, vmem=VMEM_LIMIT):
    return pltpu.CompilerParams(dimension_semantics=sem, vmem_limit_bytes=vmem)


def _dot(a, b):
    return jnp.dot(a, b, preferred_element_type=F32)


def _dot_nt(a, b):
    return lax.dot_general(a, b, (((1,), (1,)), ((), ())), preferred_element_type=F32)


def _dot_tn(a, b):
    return lax.dot_general(a, b, (((0,), (0,)), ((), ())), preferred_element_type=F32)


def _split3(x):
    hi = x.astype(BF16)
    r1 = x - hi.astype(F32)
    mid = r1.astype(BF16)
    lo = (r1 - mid.astype(F32)).astype(BF16)
    return hi, mid, lo


def _dot01(m, x, terms=3):
    parts = _split3(x)[:terms]
    out = _dot(m, parts[0])
    for part in parts[1:]:
        out = out + _dot(m, part)
    return out


def _log_sigmoid(z):
    return jnp.minimum(z, 0.0) - jnp.log1p(jnp.exp(-jnp.abs(z)))


def _sigmoid(z):
    return 1.0 / (1.0 + jnp.exp(-z))


def _silu(z):
    return z * _sigmoid(z)


def _rmsnorm_kernel(x_ref, g_ref, o_ref):
    x = x_ref[...]
    ms = jnp.mean(x * x, axis=-1, keepdims=True)
    o_ref[...] = (x * lax.rsqrt(ms + EPS) * g_ref[...]).astype(o_ref.dtype)


def _rmsnorm(x, g, out_dtype, tm=1024):
    m, d = x.shape
    return pl.pallas_call(
        _rmsnorm_kernel,
        grid=(m // tm,),
        in_specs=[pl.BlockSpec((tm, d), lambda i: (i, 0)),
                  pl.BlockSpec((1, d), lambda i: (0, 0))],
        out_specs=pl.BlockSpec((tm, d), lambda i: (i, 0)),
        out_shape=jax.ShapeDtypeStruct((m, d), out_dtype),
        compiler_params=_cparams(("parallel",)),
        name="rmsnorm",
    )(x, g.reshape(1, d))


def _inproj_kernel(starts_ref, h_ref, w_ref, o_ref, wb_ref, *, first_tile_scale):
    del starts_ref
    @pl.when(pl.program_id(1) == 0)
    def _():
        wb_ref[...] = w_ref[0].T.astype(BF16)

    acc = _dot(h_ref[...], wb_ref[...])
    if first_tile_scale is not None:
        acc = acc * jnp.where(pl.program_id(0) == 0, first_tile_scale, 1.0)
    o_ref[...] = acc.astype(o_ref.dtype)


def _inproj(h, wt, layer, row_starts, out_dtype, first_tile_scale=None, tm=1024, tn=1024):
    m, k = h.shape
    n_tiles = len(row_starts)
    assert all(r % 8 == 0 for r in row_starts)
    starts = jnp.asarray(np.asarray(row_starts, np.int32) // 8)
    grid_spec = pltpu.PrefetchScalarGridSpec(
        num_scalar_prefetch=1,
        grid=(n_tiles, m // tm),
        in_specs=[pl.BlockSpec((tm, k), lambda j, i, st: (i, 0)),
                  pl.BlockSpec((pl.Element(1), pl.Element(tn), pl.Element(k)),
                               lambda j, i, st: (layer, st[j] * 8, 0))],
        out_specs=pl.BlockSpec((tm, tn), lambda j, i, st: (i, j)),
        scratch_shapes=[pltpu.VMEM((k, tn), BF16)])
    return pl.pallas_call(
        functools.partial(_inproj_kernel, first_tile_scale=first_tile_scale),
        grid_spec=grid_spec,
        out_shape=jax.ShapeDtypeStruct((m, n_tiles * tn), out_dtype),
        compiler_params=_cparams(("parallel", "arbitrary")),
        name="inproj",
    )(starts, h, wt)


def _foxc_kernel(h_ref, w_ref, b_ref, triu_ref, c_ref, carry_ref):
    @pl.when(pl.program_id(0) == 0)
    def _():
        carry_ref[...] = jnp.zeros_like(carry_ref)

    logits = _dot_nt(w_ref[...].astype(BF16), h_ref[...]) + b_ref[...]
    hi, mid, lo = _split3(_log_sigmoid(logits))
    tri = triu_ref[...]
    cum = _dot(hi, tri) + _dot(mid, tri) + _dot(lo, tri) + carry_ref[:, 0:1]
    c_ref[...] = cum
    tm = cum.shape[1]
    carry_ref[...] = jnp.broadcast_to(cum[:, tm - 1:tm], carry_ref.shape)


def _fox_cumlog(h, wt, layer, row0, b_ff, tm=512):
    m, k = h.shape
    n_heads = b_ff.shape[0]
    assert row0 % n_heads == 0
    triu = jnp.asarray(np.triu(np.ones((tm, tm), np.float32)), BF16)
    return pl.pallas_call(
        _foxc_kernel,
        grid=(m // tm,),
        in_specs=[pl.BlockSpec((tm, k), lambda i: (i, 0)),
                  pl.BlockSpec((None, n_heads, k), lambda i: (layer, row0 // n_heads, 0)),
                  pl.BlockSpec((n_heads, 1), lambda i: (0, 0)),
                  pl.BlockSpec((tm, tm), lambda i: (0, 0))],
        out_specs=pl.BlockSpec((n_heads, tm), lambda i: (0, i)),
        out_shape=jax.ShapeDtypeStruct((n_heads, m), F32),
        scratch_shapes=[pltpu.VMEM((n_heads, LANES), F32)],
        compiler_params=_cparams(("arbitrary",)),
        name="fox_cumlog",
    )(h, wt, b_ff.reshape(n_heads, 1), triu)


NEG_BIG = -1e30


LOG2E = 1.0 / math.log(2.0)
ATT_PASS_ROWS = 32
ATT_XPOSE_ROWS = 512
ATT_WIDTHS = (2, 1)
ATT_UNROLL = 2


def _fox_attn_kernel(tab_ref, q_ref, k_ref, v_ref, crow_ref, gate_ref, o_ref,
                     va_ref, kt_ref, sa_ref, sb_ref, pa_ref, pb_ref, ala_ref, alb_ref, acc_ref, m_ref,
                     ct_ref, *, tq, runs):
    @pl.when(pl.program_id(0) == 0)
    def _():
        va_ref[:, HEAD_DIM:2 * HEAD_DIM] = jnp.ones((va_ref.shape[0], HEAD_DIM), BF16)

    va_ref[:, 0:HEAD_DIM] = v_ref[...]
    for r0 in range(0, k_ref.shape[0], ATT_XPOSE_ROWS):
        kt_ref[:, r0:r0 + ATT_XPOSE_ROWS] = (
            k_ref[r0:r0 + ATT_XPOSE_ROWS, :].astype(F32).T.astype(BF16))
    for r0 in range(0, k_ref.shape[0], ATT_XPOSE_ROWS):
        row = crow_ref[0, :, r0:r0 + ATT_XPOSE_ROWS] * LOG2E
        ct_ref[r0:r0 + ATT_XPOSE_ROWS, :] = jnp.broadcast_to(row, (8, ATT_XPOSE_ROWS)).T[:, 0:1]

    def tile(n):
        q0 = pl.multiple_of(tab_ref[0, n] * tq, tq)
        k0 = pl.multiple_of(tab_ref[1, n] * tq, tq)
        return q0, k0

    def logits(n, s_ref, tk):
        q0, k0 = tile(n)
        s_ref[:, 0:tk] = _dot(q_ref[pl.ds(q0, tq), :], kt_ref[:, pl.ds(k0, tk)])

    def softmax(n, s_ref, p_ref, al_ref, masked, tk):
        q0, k0 = tile(n)
        crow = crow_ref[0, :, pl.ds(k0, tk)] * LOG2E
        rows = ATT_PASS_ROWS
        for r in range(tq // rows):
            rs = slice(r * rows, (r + 1) * rows)
            qs = pl.ds(q0 + r * rows, rows)
            s = s_ref[rs, 0:tk] - crow
            if masked:
                row = lax.broadcasted_iota(jnp.int32, (rows, tk), 0) + r * rows
                col = lax.broadcasted_iota(jnp.int32, (rows, tk), 1)
                s = jnp.where(col <= row, s, NEG_BIG)
            ct = ct_ref[qs, :]
            m_new = jnp.max(s, axis=1, keepdims=True) + ct
            if not masked:
                m_old = m_ref[qs, :]
                m_new = jnp.maximum(m_old, m_new)
                al_ref[rs, :] = jnp.exp2(m_old - m_new)
            p_ref[rs, 0:tk] = jnp.exp2(s - (m_new - ct)).astype(BF16)
            m_ref[qs, :] = m_new

    def accumulate(n, p_ref, al_ref, masked, tk):
        q0, k0 = tile(n)
        qs = pl.ds(q0, tq)
        pv = _dot(p_ref[:, 0:tk], va_ref[pl.ds(k0, tk), :])
        acc_ref[qs, :] = pv if masked else al_ref[...] * acc_ref[qs, :] + pv

    def run(first, count, masked, tk):
        if count == 0:
            return
        last = first + count - 1
        nxt = lambda n: jnp.minimum(n, last)
        s_buf = (sa_ref, sb_ref)
        p_buf = ((pa_ref, ala_ref), (pb_ref, alb_ref))
        logits(first, s_buf[0], tk)
        softmax(first, s_buf[0], *p_buf[0], masked, tk)
        logits(nxt(first + 1), s_buf[1], tk)
        n_loop = (count - 1) // ATT_UNROLL

        def body(j, carry):
            n = first + ATT_UNROLL * j
            for u in range(ATT_UNROLL):
                accumulate(n + u, *p_buf[u % 2], masked, tk)
                softmax(n + u + 1, s_buf[(u + 1) % 2], *p_buf[(u + 1) % 2], masked, tk)
                logits(nxt(n + u + 2), s_buf[u % 2], tk)
            return carry

        lax.fori_loop(0, n_loop, body, 0)
        n = first + ATT_UNROLL * n_loop
        rest = count - 1 - ATT_UNROLL * n_loop
        for u in range(rest + 1):
            accumulate(n + u, *p_buf[u % 2], masked, tk)
            if u + 1 <= rest:
                softmax(n + u + 1, s_buf[(u + 1) % 2], *p_buf[(u + 1) % 2], masked, tk)
            if u + 2 <= rest:
                logits(n + u + 2, s_buf[u % 2], tk)

    for first, count, masked, tk in runs:
        run(first, count, masked, tk)

    acc = acc_ref[...]
    out = acc[:, 0:HEAD_DIM] / acc[:, HEAD_DIM:2 * HEAD_DIM]
    o_ref[...] = (out * _silu(gate_ref[...])).astype(o_ref.dtype)


def _fox_attn(zqkv, z, crow, g_off, n_heads, tq=512):
    s_len = zqkv.shape[0]
    nq = s_len // tq
    go = g_off // HEAD_DIM
    tiles = [(i, i) for i in range(nq)]
    runs = [(0, nq, True, tq)]
    for w in ATT_WIDTHS:
        cls = []
        for qi in range(nq):
            k = 0
            for w2 in ATT_WIDTHS:
                n_w2 = (qi - k) // w2
                if w2 == w:
                    cls += [(qi, k + i * w) for i in range(n_w2)]
                k += n_w2 * w2
        cls.sort(key=lambda t: (t[1], t[0]))
        runs.append((len(tiles), len(cls), False, w * tq))
        tiles += cls
    tab = jnp.asarray(np.array(tiles, np.int32).reshape(-1, 2).T)
    wmax = max(ATT_WIDTHS) * tq
    kern = functools.partial(_fox_attn_kernel, tq=tq, runs=tuple(runs))
    once = pl.Buffered(1)
    head_col = lambda base: pl.BlockSpec((s_len, HEAD_DIM), lambda h, t: (0, base + h),
                                         pipeline_mode=once)
    grid_spec = pltpu.PrefetchScalarGridSpec(
        num_scalar_prefetch=1,
        grid=(n_heads,),
        in_specs=[head_col(0), head_col(n_heads), head_col(2 * n_heads),
                  pl.BlockSpec((1, 1, s_len), lambda h, t: (h, 0, 0)),
                  head_col(go)],
        out_specs=pl.BlockSpec((s_len, HEAD_DIM), lambda h, t: (0, h)),
        scratch_shapes=[pltpu.VMEM((s_len, 2 * HEAD_DIM), BF16),
                        pltpu.VMEM((HEAD_DIM, s_len), BF16),
                        pltpu.VMEM((tq, wmax), F32),
                        pltpu.VMEM((tq, wmax), F32),
                        pltpu.VMEM((tq, wmax), BF16),
                        pltpu.VMEM((tq, wmax), BF16),
                        pltpu.VMEM((tq, 1), F32),
                        pltpu.VMEM((tq, 1), F32),
                        pltpu.VMEM((s_len, 2 * HEAD_DIM), F32),
                        pltpu.VMEM((s_len, 1), F32),
                        pltpu.VMEM((s_len, 1), F32)])
    return pl.pallas_call(
        kern,
        grid_spec=grid_spec,
        out_shape=jax.ShapeDtypeStruct((s_len, n_heads * HEAD_DIM), BF16),
        compiler_params=_cparams(("arbitrary",)),
        name="fox_attn",
    )(tab, zqkv, zqkv, zqkv, crow, z)


HG_PROWS = 8 * HG_SUB + 8 * (HG_SUB // 2)


HG_HEADS_PER_STEP = 4


def _hgrn_kernel(q_ref, f_ref, i_ref, gate_ref, lb_ref, ng_ref, tb_ref, ones_ref, mask_ref, o_ref,
                 *scratch, layer, rows):
    @pl.when(pl.program_id(1) == 0)
    def _():
        scratch[0][...] = jnp.zeros_like(scratch[0])

    for hh in range(HG_HEADS_PER_STEP):
        lanes = pl.ds(hh * HEAD_DIM, HEAD_DIM)
        _hgrn_head(*(r.at[:, lanes] for r in (q_ref, f_ref, i_ref, gate_ref, lb_ref, ng_ref)),
                   tb_ref, ones_ref, mask_ref, o_ref.at[:, lanes], *(s.at[hh] for s in scratch),
                   layer=layer, rows=rows)


def _hgrn_head(q_ref, f_ref, i_ref, gate_ref, lb_ref, ng_ref, tb_ref, ones_ref,
               mask_ref, o_ref, st_ref, qt_ref, kt_ref, w_ref, cum_ref, dd_ref, p_ref, sc_ref,
               acc_ref, *, layer, rows):
    z = f_ref[...]
    ls = _log_sigmoid(z)
    if layer == 0:
        g = ls
        logk = ls - z
    else:
        lbp = lb_ref[...]
        e = jnp.exp(lbp - jnp.max(lbp, axis=0, keepdims=True))
        p = e / jnp.sum(e, axis=0, keepdims=True)
        lb = jnp.sum(p[1:layer + 1, :], axis=0, keepdims=True)
        a = jnp.log(lb)
        l1m = jnp.log1p(-lb)
        b = l1m + ls
        g = jnp.maximum(a, b) + jnp.log1p(jnp.exp(-jnp.abs(a - b)))
        logk = l1m + (ls - z)

    cums, tots = [], []
    for r0 in range(0, rows, LANES):
        ct = _dot01(tb_ref[...], g[r0:r0 + LANES, :])
        cums.append(ct[0:LANES, :])
        tots.append(ct[LANES:2 * LANES, :])
    cum = jnp.concatenate(cums, axis=0) * LOG2E
    tot = jnp.concatenate(tots, axis=0) * LOG2E
    w = cum - logk * LOG2E
    qt_ref[...] = (q_ref[...] * jnp.exp2(cum)).astype(BF16)
    kt_ref[...] = jnp.exp2(tot - w).astype(BF16)
    w_ref[...] = w
    cum_ref[...] = cum
    dd_ref[...] = jnp.exp2(tot)

    half = HG_SUB // 2
    n_groups = rows // HG_SUB

    def bcast_row(ref, r):
        return jnp.broadcast_to(ref[r:r + 1, :], (half, HEAD_DIM))

    for g_i in range(n_groups):
        r0 = g_i * HG_SUB
        p0 = g_i * HG_PROWS
        c_lo, c_hi = cum_ref[r0:r0 + half, :], cum_ref[r0 + half:r0 + HG_SUB, :]
        q_lo, q_hi = q_ref[r0:r0 + half, :], q_ref[r0 + half:r0 + HG_SUB, :]
        for s in range(half):
            w_s = bcast_row(w_ref, r0 + s)
            p_lo = q_lo * jnp.exp2(c_lo - w_s + mask_ref[s * half:(s + 1) * half, :])
            p_hi = q_hi * jnp.exp2(c_hi - w_s)
            p_ref[p0 + s * HG_SUB:p0 + (s + 1) * HG_SUB, :] = (
                jnp.concatenate([p_lo, p_hi], axis=0).astype(BF16))
        for s in range(0, half, 2):
            pa = q_hi * jnp.exp2(c_hi - bcast_row(w_ref, r0 + half + s)
                                + mask_ref[s * half:(s + 1) * half, :])
            pb = q_hi * jnp.exp2(c_hi - bcast_row(w_ref, r0 + half + s + 1)
                                + mask_ref[(s + 1) * half:(s + 2) * half, :])
            base = p0 + half * HG_SUB + s * half
            p_ref[base:base + HG_SUB, :] = jnp.concatenate([pa, pb], axis=0).astype(BF16)

    sc_ref[...] = _dot(p_ref[...], ones_ref[...])

    upds = [_dot_tn(i_ref[g_i * HG_SUB:(g_i + 1) * HG_SUB, :].astype(BF16),
                    kt_ref[g_i * HG_SUB:(g_i + 1) * HG_SUB, :]) for g_i in range(n_groups)]
    st = st_ref[...]
    for g_i in range(n_groups):
        r0 = g_i * HG_SUB
        p0 = g_i * HG_PROWS
        o_lo = jnp.zeros((half, HEAD_DIM), F32)
        o_hi = jnp.zeros((half, HEAD_DIM), F32)
        for s in range(half):
            v_s = bcast_row(i_ref, r0 + s)
            o_lo = o_lo + sc_ref[p0 + s * HG_SUB:p0 + s * HG_SUB + half, :] * v_s
            o_hi = o_hi + sc_ref[p0 + s * HG_SUB + half:p0 + (s + 1) * HG_SUB, :] * v_s
        for s in range(half):
            base = p0 + half * HG_SUB + s * half
            o_hi = o_hi + sc_ref[base:base + half, :] * bcast_row(i_ref, r0 + half + s)
        o_inter = _dot(qt_ref[r0:r0 + HG_SUB, :], st.T.astype(BF16))
        acc_ref[r0:r0 + HG_SUB, :] = o_inter + jnp.concatenate([o_lo, o_hi], axis=0)
        st = st * dd_ref[r0:r0 + 1, :] + upds[g_i]
    st_ref[...] = st

    o = acc_ref[...]
    ms = jnp.mean(o * o, axis=-1, keepdims=True)
    o = o * lax.rsqrt(ms + EPS) * ng_ref[...]
    o_ref[...] = (o * _silu(gate_ref[...])).astype(o_ref.dtype)


def _hgrn(z, hg_lb, norm_g, layer, q_off, f_off, i_off, g_off, n_heads, rows=512):
    s_len = z.shape[0]
    depth = hg_lb.shape[0]
    hps = HG_HEADS_PER_STEP
    width = hps * HEAD_DIM
    assert n_heads % hps == 0 and all(o % width == 0 for o in (q_off, f_off, i_off, g_off))
    qo, fo, io, go = (o // width for o in (q_off, f_off, i_off, g_off))
    r = np.arange(LANES)
    same = (r[:, None] // HG_SUB) == (r[None, :] // HG_SUB)
    tb = jnp.asarray(np.concatenate([same & (r[None, :] <= r[:, None]), same]).astype(np.float32), BF16)
    ones = jnp.ones((HEAD_DIM, HEAD_DIM), BF16)
    half = HG_SUB // 2
    t_idx = np.arange(half)
    mask_np = np.where(t_idx[None, :, None] >= t_idx[:, None, None], 0.0, NEG_BIG)
    mask = jnp.asarray(np.broadcast_to(mask_np, (half, half, HEAD_DIM)).reshape(half * half, HEAD_DIM), F32)
    n_prows = (rows // HG_SUB) * HG_PROWS
    kern = functools.partial(_hgrn_kernel, layer=layer, rows=rows)
    blk_spec = lambda off: pl.BlockSpec((rows, width), lambda h, i: (i, off + h))
    const = lambda shape: pl.BlockSpec(shape, lambda h, i: (0, 0))
    per_head = lambda shape, dtype: pltpu.VMEM((hps,) + shape, dtype)
    return pl.pallas_call(
        kern,
        grid=(n_heads // hps, s_len // rows),
        in_specs=[blk_spec(qo), blk_spec(fo), blk_spec(io), blk_spec(go),
                  pl.BlockSpec((depth, width), lambda h, i: (0, h)),
                  pl.BlockSpec((1, width), lambda h, i: (0, h)),
                  const((2 * LANES, LANES)), const((HEAD_DIM, HEAD_DIM)),
                  const((half * half, HEAD_DIM))],
        out_specs=pl.BlockSpec((rows, width), lambda h, i: (i, h)),
        out_shape=jax.ShapeDtypeStruct((s_len, n_heads * HEAD_DIM), BF16),
        scratch_shapes=[per_head((HEAD_DIM, HEAD_DIM), F32),
                        per_head((rows, HEAD_DIM), BF16),
                        per_head((rows, HEAD_DIM), BF16),
                        per_head((rows, HEAD_DIM), F32),
                        per_head((rows, HEAD_DIM), F32),
                        per_head((rows, HEAD_DIM), F32),
                        per_head((n_prows, HEAD_DIM), BF16),
                        per_head((n_prows, HEAD_DIM), F32),
                        per_head((rows, HEAD_DIM), F32)],
        compiler_params=_cparams(("parallel", "arbitrary")),
        name="hgrn2",
    )(z, z, z, z, hg_lb, norm_g.reshape(1, -1), tb, ones, mask)


def _gelu_tanh(x):
    c = math.sqrt(2.0 / math.pi)
    return 0.5 * x * (1.0 + jnp.tanh(c * (x + 0.044715 * (x * x * x))))


def _s5_kernel(u_ref, perm_ref, permt_ref, bm_ref, cm_ref, are_ref, aim_ref, pw_ref,
               d_ref, o_ref, state_ref, x_ref, yp_ref, *, rows, half):
    @pl.when(pl.program_id(0) == 0)
    def _():
        state_ref[...] = jnp.zeros_like(state_ref)

    nblk = bm_ref.shape[0]
    nt = rows // 8
    u = u_ref[...]
    up = _dot(perm_ref[...], u.astype(BF16)).astype(BF16)
    sub = lax.broadcasted_iota(jnp.int32, (8, half), 0)

    re, im = slice(0, half), slice(half, 2 * half)
    for b in range(nblk):
        x_ref[b] = _dot(up[:, b * LANES:(b + 1) * LANES], bm_ref[b])
    for b in range(nblk):
        are = are_ref[b]
        aim = aim_ref[b]
        xr = jnp.zeros((8, half), F32)
        xi = jnp.zeros((8, half), F32)
        for t in range(nt):
            rs = slice(t * 8, (t + 1) * 8)
            xr, xi = (are * xr - aim * xi + x_ref[b, rs, re],
                      are * xi + aim * xr + x_ref[b, rs, im])
            x_ref[b, rs, re] = xr
            x_ref[b, rs, im] = xi

        er, ei = xr, xi
        alre = pw_ref[b, rows - 1:rows, re]
        alim = pw_ref[b, rows - 1:rows, im]
        cr = state_ref[b, 0:1, re]
        ci = state_ref[b, 0:1, im]
        ctr = jnp.zeros((8, half), F32)
        cti = jnp.zeros((8, half), F32)
        for s in range(8):
            ctr = jnp.where(sub == s, cr, ctr)
            cti = jnp.where(sub == s, ci, cti)
            cr, ci = (alre * cr - alim * ci + er[s:s + 1, :],
                      alre * ci + alim * cr + ei[s:s + 1, :])
        state_ref[b, :, re] = jnp.broadcast_to(cr, (8, half))
        state_ref[b, :, im] = jnp.broadcast_to(ci, (8, half))

        for t in range(nt):
            rs = slice(t * 8, (t + 1) * 8)
            pr = pw_ref[b, rs, re]
            pi = pw_ref[b, rs, im]
            x_ref[b, rs, re] = x_ref[b, rs, re] + (pr * ctr - pi * cti)
            x_ref[b, rs, im] = x_ref[b, rs, im] + (pr * cti + pi * ctr)
        yp_ref[:, b * LANES:(b + 1) * LANES] = _dot(x_ref[b].astype(BF16), cm_ref[b])

    y = _dot01(permt_ref[...], yp_ref[...], terms=2) + d_ref[...] * u
    o_ref[...] = _gelu_tanh(y).astype(o_ref.dtype)


def _s5(z, u_off, bm, cm, are, aim, pw, d_skip, rows=256):
    s_len = z.shape[0]
    width = d_skip.shape[0]
    nblk, _, two_half = bm.shape
    half = two_half // 2
    seg = rows // 8
    rho = np.arange(rows)
    t_of = (rho % 8) * seg + rho // 8
    perm_np = np.zeros((rows, rows), np.float32)
    perm_np[rho, t_of] = 1.0
    perm = jnp.asarray(perm_np, BF16)
    permt = jnp.asarray(perm_np.T, BF16)
    uo = u_off // width
    kern = functools.partial(_s5_kernel, rows=rows, half=half)
    c2 = lambda shape: pl.BlockSpec(shape, lambda i: (0, 0))
    c3 = lambda shape: pl.BlockSpec(shape, lambda i: (0, 0, 0))
    return pl.pallas_call(
        kern,
        grid=(s_len // rows,),
        in_specs=[pl.BlockSpec((rows, width), lambda i: (i, uo)),
                  c2((rows, rows)), c2((rows, rows)),
                  c3(bm.shape), c3(cm.shape),
                  c3(are.shape), c3(aim.shape),
                  pl.BlockSpec(pw.shape, lambda i: (0, 0, 0), pipeline_mode=pl.Buffered(1)),
                  c2((1, width))],
        out_specs=pl.BlockSpec((rows, width), lambda i: (i, 0)),
        out_shape=jax.ShapeDtypeStruct((s_len, width), BF16),
        scratch_shapes=[pltpu.VMEM((nblk, 8, two_half), F32),
                        pltpu.VMEM((nblk, rows, two_half), F32),
                        pltpu.VMEM((rows, width), F32)],
        compiler_params=_cparams(("arbitrary",)),
        name="s5",
    )(z, perm, permt, bm, cm, are, aim, pw, d_skip.reshape(1, width))


def _s5_params(a_re, a_im, log_dt, b_re, b_im, c_re, c_im, seg):
    g_n, p_n = a_re.shape
    nblk = g_n // S5_GPB
    dt = jnp.exp(log_dt)[:, None]
    mag = jnp.exp(a_re * dt)
    ang = a_im * dt
    abar_re = mag * jnp.cos(ang)
    abar_im = mag * jnp.sin(ang)
    nr = abar_re - 1.0
    den = a_re * a_re + a_im * a_im
    zr = (nr * a_re + abar_im * a_im) / den
    zi = (abar_im * a_re - nr * a_im) / den
    bbar_re = zr[:, :, None] * b_re - zi[:, :, None] * b_im
    bbar_im = zr[:, :, None] * b_im + zi[:, :, None] * b_re
    same = jnp.asarray(np.eye(S5_GPB, dtype=bool))

    def embed_b(bb):
        bb = bb.reshape(nblk, S5_GPB, 1, p_n, S5_GROUP).transpose(0, 1, 4, 2, 3)
        m = jnp.where(same[None, :, None, :, None], bb, 0.0)
        return m.reshape(nblk, S5_GPB * S5_GROUP, S5_GPB * p_n)

    bm = jnp.concatenate([embed_b(bbar_re), embed_b(bbar_im)], axis=2).astype(BF16)

    def embed_c(cc):
        cc = cc.reshape(nblk, S5_GPB, S5_GROUP, 1, p_n).transpose(0, 1, 4, 3, 2)
        m = jnp.where(same[None, :, None, :, None], cc, 0.0)
        return m.reshape(nblk, S5_GPB * p_n, S5_GPB * S5_GROUP)

    cm = jnp.concatenate([embed_c(c_re), embed_c(-c_im)], axis=1).astype(BF16)

    def tile8(v):
        v = v.reshape(nblk, 1, S5_GPB * p_n)
        return jnp.broadcast_to(v, (nblk, 8, S5_GPB * p_n))

    k = jnp.arange(1, seg + 1, dtype=F32)[:, None, None]
    mag_k = jnp.exp(k * (a_re * dt))
    ang_k = k * ang

    def table(t):
        t = t.reshape(seg, 1, nblk, S5_GPB * p_n)
        return jnp.broadcast_to(t, (seg, 8, nblk, S5_GPB * p_n)).transpose(2, 0, 1, 3).reshape(
            nblk, 8 * seg, S5_GPB * p_n)

    pw = jnp.concatenate([table(mag_k * jnp.cos(ang_k)), table(mag_k * jnp.sin(ang_k))], axis=2)
    return bm, cm, tile8(abar_re), tile8(abar_im), pw


def _glu_kernel(y_ref, w_ref, gate_ref, o_ref, *, width):
    zg = _dot(y_ref[...], w_ref[...])
    o = zg[:, :width] * _sigmoid(zg[:, width:]) * _silu(gate_ref[...])
    o_ref[...] = o.astype(o_ref.dtype)


def _glu(y, w_glu, layer, z, g_off, tm=1024):
    m, width = y.shape
    go = g_off // width
    return pl.pallas_call(
        functools.partial(_glu_kernel, width=width),
        grid=(m // tm,),
        in_specs=[pl.BlockSpec((tm, width), lambda i: (i, 0)),
                  pl.BlockSpec((None, width, 2 * width), lambda i: (layer, 0, 0)),
                  pl.BlockSpec((tm, width), lambda i: (i, go))],
        out_specs=pl.BlockSpec((tm, width), lambda i: (i, 0)),
        out_shape=jax.ShapeDtypeStruct((m, width), BF16),
        compiler_params=_cparams(("parallel",)),
        name="s5_glu",
    )(y, w_glu, z)


def _merge_kernel(oa_ref, ob_ref, oc_ref, ga_ref, gb_ref, gc_ref, bg_ref, wa_ref, wb_ref, wc_ref,
                  wo_ref, x_ref, ng_ref, xo_ref, ho_ref, *, d):
    def gate(g_ref, i):
        return _sigmoid(g_ref[...] + bg_ref[:, i * d:(i + 1) * d])

    merged = gate(ga_ref, 0) * _dot(oa_ref[...], wa_ref[...])
    merged = merged + gate(gb_ref, 1) * _dot(ob_ref[...], wb_ref[...])
    merged = merged + gate(gc_ref, 2) * _dot(oc_ref[...], wc_ref[...])
    xn = x_ref[...] + _dot(merged.astype(BF16), wo_ref[...])
    xo_ref[...] = xn
    ms = jnp.mean(xn * xn, axis=-1, keepdims=True)
    ho_ref[...] = (xn * lax.rsqrt(ms + EPS) * ng_ref[...]).astype(ho_ref.dtype)


def _merge(oa, ob, oc, z, mg_off, b_gate, wa, wb, wc, wo, layer, x, next_g, h_dtype, tm=256):
    m, d = x.shape
    w = oa.shape[1]
    assert mg_off % d == 0
    mo = mg_off // d
    row = lambda width: pl.BlockSpec((tm, width), lambda i: (i, 0))
    gate = lambda k: pl.BlockSpec((tm, d), lambda i: (i, mo + k))
    resident = lambda shape: pl.BlockSpec(shape, lambda i: (0, 0), pipeline_mode=pl.Buffered(1))
    weight = lambda rows: pl.BlockSpec((None, rows, d), lambda i: (layer, 0, 0),
                                       pipeline_mode=pl.Buffered(1))
    return pl.pallas_call(
        functools.partial(_merge_kernel, d=d),
        grid=(m // tm,),
        in_specs=[row(w), row(w), row(w), gate(0), gate(1), gate(2),
                  resident((1, N_BRANCH * d)),
                  weight(w), weight(w), weight(w), weight(d),
                  row(d), resident((1, d))],
        out_specs=[row(d), row(d)],
        out_shape=[jax.ShapeDtypeStruct((m, d), F32), jax.ShapeDtypeStruct((m, d), h_dtype)],
        compiler_params=_cparams(("parallel",)),
        name="merge_out",
    )(oa, ob, oc, z, z, z, b_gate.reshape(1, -1), wa, wb, wc, wo, x, next_g.reshape(1, -1))


def kernel(x, norm_g, w_in, b_gate, fox_bf, hg_lb, hg_norm_g, s5_a_re, s5_a_im, s5_log_dt,
           s5_b_re, s5_b_im, s5_c_re, s5_c_im, s5_d, s5_w_glu, w_br_a, w_br_b, w_br_c, w_out,
           final_g):
    bsz, s_len, d = x.shape
    depth = w_in.shape[0]
    hg_w = hg_lb.shape[1]
    s5_w = s5_d.shape[1]
    n_fox = fox_bf.shape[1]
    fox_w = n_fox * HEAD_DIM
    n_hg = hg_w // HEAD_DIM
    sizes = (hg_w, hg_w, hg_w, hg_w, s5_w, s5_w, fox_w, fox_w, fox_w, n_fox, fox_w, N_BRANCH * d)
    offs = np.concatenate([[0], np.cumsum(sizes)])
    (o_hq, o_hf, o_hi, o_hg, o_su, o_sg, o_fq, o_fk, o_fv, o_ff, o_fg, o_mg, o_end) = (int(v) for v in offs)
    wt = jnp.transpose(w_in, (0, 2, 1))
    tn = fox_w
    tiles = lambda start, stop: list(range(start, stop, tn))
    o_mg2, o_fg2 = 0, o_end - o_mg
    w_glu, w_a, w_b, w_c, w_o = (w.astype(BF16) for w in (s5_w_glu, w_br_a, w_br_b, w_br_c, w_out))
    s5_rows = 256
    outs = []
    for b in range(bsz):
        xb = x[b]
        h = _rmsnorm(xb, norm_g[0], BF16)
        for l in range(depth):
            z = _inproj(h, wt, l, tiles(0, o_fq), F32, tn=tn)
            zqkv = _inproj(h, wt, l, tiles(o_fq, o_ff), BF16,
                           first_tile_scale=HEAD_DIM ** -0.5 * LOG2E, tn=tn)
            zg = _inproj(h, wt, l, tiles(o_mg, o_end) + tiles(o_fg, o_mg), F32, tn=tn)

            o_a = _hgrn(z, hg_lb, hg_norm_g[l], l, o_hq, o_hf, o_hi, o_hg, n_hg)

            s5p = _s5_params(s5_a_re[l], s5_a_im[l], s5_log_dt[l], s5_b_re[l], s5_b_im[l],
                             s5_c_re[l], s5_c_im[l], s5_rows // 8)
            y_b = _s5(z, o_su, *s5p, s5_d[l], rows=s5_rows)
            o_b = _glu(y_b, w_glu, l, z, o_sg)

            ct = _fox_cumlog(h, wt, l, o_ff, fox_bf[l])
            o_c = _fox_attn(zqkv, zg, ct[:, None, :], o_fg2, n_fox)

            last = l == depth - 1
            next_g = final_g if last else norm_g[l + 1]
            xb, h = _merge(o_a, o_b, o_c, zg, o_mg2, b_gate[l], w_a, w_b, w_c, w_o, l,
                           xb, next_g, F32 if last else BF16)
        outs.append(h)
    return outs[0][None] if bsz == 1 else jnp.stack(outs, axis=0)
```

```python
import functools
import math

import numpy as np
import jax
import jax.numpy as jnp
from jax import lax
from jax.experimental import pallas as pl
from jax.experimental.pallas import tpu as pltpu

F32 = jnp.float32
BF16 = jnp.bfloat16

EPS = 1e-6
LANES = 128
VMEM_LIMIT = 56 * 1024 * 1024

HEAD_DIM = 128
S5_GROUP = 16
S5_STATE = 64
S5_GPB = 8
N_BRANCH = 3

HG_SUB = 16


def _cparams(sem, vmem=VMEM_LIMIT):
    return pltpu.CompilerParams(dimension_semantics=sem, vmem_limit_bytes=vmem)


def _dot(a, b):
    return jnp.dot(a, b, preferred_element_type=F32)


def _dot_nt(a, b):
    return lax.dot_general(a, b, (((1,), (1,)), ((), ())), preferred_element_type=F32)


def _dot_tn(a, b):
    return lax.dot_general(a, b, (((0,), (0,)), ((), ())), preferred_element_type=F32)


def _split3(x):
    hi = x.astype(BF16)
    r1 = x - hi.astype(F32)
    mid = r1.astype(BF16)
    lo = (r1 - mid.astype(F32)).astype(BF16)
    return hi, mid, lo


def _dot01(m, x, terms=3):
    parts = _split3(x)[:terms]
    out = _dot(m, parts[0])
    for part in parts[1:]:
        out = out + _dot(m, part)
    return out


def _log_sigmoid(z):
    return jnp.minimum(z, 0.0) - jnp.log1p(jnp.exp(-jnp.abs(z)))


def _sigmoid(z):
    return 1.0 / (1.0 + jnp.exp(-z))


def _silu(z):
    return z * _sigmoid(z)


def _rmsnorm_kernel(x_ref, g_ref, o_ref):
    x = x_ref[...]
    ms = jnp.mean(x * x, axis=-1, keepdims=True)
    o_ref[...] = (x * lax.rsqrt(ms + EPS) * g_ref[...]).astype(o_ref.dtype)


def _rmsnorm(x, g, out_dtype, tm=1024):
    m, d = x.shape
    return pl.pallas_call(
        _rmsnorm_kernel,
        grid=(m // tm,),
        in_specs=[pl.BlockSpec((tm, d), lambda i: (i, 0)),
                  pl.BlockSpec((1, d), lambda i: (0, 0))],
        out_specs=pl.BlockSpec((tm, d), lambda i: (i, 0)),
        out_shape=jax.ShapeDtypeStruct((m, d), out_dtype),
        compiler_params=_cparams(("parallel",)),
        name="rmsnorm",
    )(x, g.reshape(1, d))


def _inproj_kernel(starts_ref, h_ref, w_ref, o_ref, wb_ref, *, first_tile_scale):
    del starts_ref
    @pl.when(pl.program_id(1) == 0)
    def _():
        wb_ref[...] = w_ref[0].T.astype(BF16)

    acc = _dot(h_ref[...], wb_ref[...])
    if first_tile_scale is not None:
        acc = acc * jnp.where(pl.program_id(0) == 0, first_tile_scale, 1.0)
    o_ref[...] = acc.astype(o_ref.dtype)


def _inproj(h, wt, layer, row_starts, out_dtype, first_tile_scale=None, tm=1024, tn=1024):
    m, k = h.shape
    n_tiles = len(row_starts)
    assert all(r % 8 == 0 for r in row_starts)
    starts = jnp.asarray(np.asarray(row_starts, np.int32) // 8)
    grid_spec = pltpu.PrefetchScalarGridSpec(
        num_scalar_prefetch=1,
        grid=(n_tiles, m // tm),
        in_specs=[pl.BlockSpec((tm, k), lambda j, i, st: (i, 0)),
                  pl.BlockSpec((pl.Element(1), pl.Element(tn), pl.Element(k)),
                               lambda j, i, st: (layer, st[j] * 8, 0))],
        out_specs=pl.BlockSpec((tm, tn), lambda j, i, st: (i, j)),
        scratch_shapes=[pltpu.VMEM((k, tn), BF16)])
    return pl.pallas_call(
        functools.partial(_inproj_kernel, first_tile_scale=first_tile_scale),
        grid_spec=grid_spec,
        out_shape=jax.ShapeDtypeStruct((m, n_tiles * tn), out_dtype),
        compiler_params=_cparams(("parallel", "arbitrary")),
        name="inproj",
    )(starts, h, wt)


def _foxc_kernel(h_ref, w_ref, b_ref, triu_ref, c_ref, carry_ref):
    @pl.when(pl.program_id(0) == 0)
    def _():
        carry_ref[...] = jnp.zeros_like(carry_ref)

    logits = _dot_nt(w_ref[...].astype(BF16), h_ref[...]) + b_ref[...]
    hi, mid, lo = _split3(_log_sigmoid(logits))
    tri = triu_ref[...]
    cum = _dot(hi, tri) + _dot(mid, tri) + _dot(lo, tri) + carry_ref[:, 0:1]
    c_ref[...] = cum
    tm = cum.shape[1]
    carry_ref[...] = jnp.broadcast_to(cum[:, tm - 1:tm], carry_ref.shape)


def _fox_cumlog(h, wt, layer, row0, b_ff, tm=512):
    m, k = h.shape
    n_heads = b_ff.shape[0]
    assert row0 % n_heads == 0
    triu = jnp.asarray(np.triu(np.ones((tm, tm), np.float32)), BF16)
    return pl.pallas_call(
        _foxc_kernel,
        grid=(m // tm,),
        in_specs=[pl.BlockSpec((tm, k), lambda i: (i, 0)),
                  pl.BlockSpec((None, n_heads, k), lambda i: (layer, row0 // n_heads, 0)),
                  pl.BlockSpec((n_heads, 1), lambda i: (0, 0)),
                  pl.BlockSpec((tm, tm), lambda i: (0, 0))],
        out_specs=pl.BlockSpec((n_heads, tm), lambda i: (0, i)),
        out_shape=jax.ShapeDtypeStruct((n_heads, m), F32),
        scratch_shapes=[pltpu.VMEM((n_heads, LANES), F32)],
        compiler_params=_cparams(("arbitrary",)),
        name="fox_cumlog",
    )(h, wt, b_ff.reshape(n_heads, 1), triu)


NEG_BIG = -1e30


LOG2E = 1.0 / math.log(2.0)
ATT_PASS_ROWS = 32
ATT_XPOSE_ROWS = 512
ATT_WIDTHS = (2, 1)
ATT_UNROLL = 4


def _fox_attn_kernel(tab_ref, q_ref, k_ref, v_ref, crow_ref, gate_ref, o_ref,
                     va_ref, kt_ref, sa_ref, sb_ref, pa_ref, pb_ref, ala_ref, alb_ref, acc_ref, m_ref,
                     ct_ref, *, tq, runs):
    @pl.when(pl.program_id(0) == 0)
    def _():
        va_ref[:, HEAD_DIM:2 * HEAD_DIM] = jnp.ones((va_ref.shape[0], HEAD_DIM), BF16)

    va_ref[:, 0:HEAD_DIM] = v_ref[...]
    for r0 in range(0, k_ref.shape[0], ATT_XPOSE_ROWS):
        kt_ref[:, r0:r0 + ATT_XPOSE_ROWS] = (
            k_ref[r0:r0 + ATT_XPOSE_ROWS, :].astype(F32).T.astype(BF16))
    for r0 in range(0, k_ref.shape[0], ATT_XPOSE_ROWS):
        row = crow_ref[0, :, r0:r0 + ATT_XPOSE_ROWS] * LOG2E
        ct_ref[r0:r0 + ATT_XPOSE_ROWS, :] = jnp.broadcast_to(row, (8, ATT_XPOSE_ROWS)).T[:, 0:1]

    def tile(n):
        q0 = pl.multiple_of(tab_ref[0, n] * tq, tq)
        k0 = pl.multiple_of(tab_ref[1, n] * tq, tq)
        return q0, k0

    def logits(n, s_ref, tk):
        q0, k0 = tile(n)
        s_ref[:, 0:tk] = _dot(q_ref[pl.ds(q0, tq), :], kt_ref[:, pl.ds(k0, tk)])

    def softmax(n, s_ref, p_ref, al_ref, masked, tk):
        q0, k0 = tile(n)
        crow = crow_ref[0, :, pl.ds(k0, tk)] * LOG2E
        rows = ATT_PASS_ROWS
        for r in range(tq // rows):
            rs = slice(r * rows, (r + 1) * rows)
            qs = pl.ds(q0 + r * rows, rows)
            s = s_ref[rs, 0:tk] - crow
            if masked:
                row = lax.broadcasted_iota(jnp.int32, (rows, tk), 0) + r * rows
                col = lax.broadcasted_iota(jnp.int32, (rows, tk), 1)
                s = jnp.where(col <= row, s, NEG_BIG)
            ct = ct_ref[qs, :]
            m_new = jnp.max(s, axis=1, keepdims=True) + ct
            if not masked:
                m_old = m_ref[qs, :]
                m_new = jnp.maximum(m_old, m_new)
                al_ref[rs, :] = jnp.exp2(m_old - m_new)
            p_ref[rs, 0:tk] = jnp.exp2(s - (m_new - ct)).astype(BF16)
            m_ref[qs, :] = m_new

    def accumulate(n, p_ref, al_ref, masked, tk):
        q0, k0 = tile(n)
        qs = pl.ds(q0, tq)
        pv = _dot(p_ref[:, 0:tk], va_ref[pl.ds(k0, tk), :])
        acc_ref[qs, :] = pv if masked else al_ref[...] * acc_ref[qs, :] + pv

    def run(first, count, masked, tk):
        if count == 0:
            return
        last = first + count - 1
        nxt = lambda n: jnp.minimum(n, last)
        s_buf = (sa_ref, sb_ref)
        p_buf = ((pa_ref, ala_ref), (pb_ref, alb_ref))
        logits(first, s_buf[0], tk)
        softmax(first, s_buf[0], *p_buf[0], masked, tk)
        logits(nxt(first + 1), s_buf[1], tk)
        n_loop = (count - 1) // ATT_UNROLL

        def body(j, carry):
            n = first + ATT_UNROLL * j
            for u in range(ATT_UNROLL):
                accumulate(n + u, *p_buf[u % 2], masked, tk)
                softmax(n + u + 1, s_buf[(u + 1) % 2], *p_buf[(u + 1) % 2], masked, tk)
                logits(nxt(n + u + 2), s_buf[u % 2], tk)
            return carry

        lax.fori_loop(0, n_loop, body, 0)
        n = first + ATT_UNROLL * n_loop
        rest = count - 1 - ATT_UNROLL * n_loop
        for u in range(rest + 1):
            accumulate(n + u, *p_buf[u % 2], masked, tk)
            if u + 1 <= rest:
                softmax(n + u + 1, s_buf[(u + 1) % 2], *p_buf[(u + 1) % 2], masked, tk)
            if u + 2 <= rest:
                logits(n + u + 2, s_buf[u % 2], tk)

    for first, count, masked, tk in runs:
        run(first, count, masked, tk)

    acc = acc_ref[...]
    out = acc[:, 0:HEAD_DIM] / acc[:, HEAD_DIM:2 * HEAD_DIM]
    o_ref[...] = (out * _silu(gate_ref[...])).astype(o_ref.dtype)


def _fox_attn(zqkv, z, crow, g_off, n_heads, tq=512):
    s_len = zqkv.shape[0]
    nq = s_len // tq
    go = g_off // HEAD_DIM
    tiles = [(i, i) for i in range(nq)]
    runs = [(0, nq, True, tq)]
    for w in ATT_WIDTHS:
        cls = []
        for qi in range(nq):
            k = 0
            for w2 in ATT_WIDTHS:
                n_w2 = (qi - k) // w2
                if w2 == w:
                    cls += [(qi, k + i * w) for i in range(n_w2)]
                k += n_w2 * w2
        cls.sort(key=lambda t: (t[1], t[0]))
        runs.append((len(tiles), len(cls), False, w * tq))
        tiles += cls
    tab = jnp.asarray(np.array(tiles, np.int32).reshape(-1, 2).T)
    wmax = max(ATT_WIDTHS) * tq
    kern = functools.partial(_fox_attn_kernel, tq=tq, runs=tuple(runs))
    once = pl.Buffered(1)
    head_col = lambda base: pl.BlockSpec((s_len, HEAD_DIM), lambda h, t: (0, base + h),
                                         pipeline_mode=once)
    grid_spec = pltpu.PrefetchScalarGridSpec(
        num_scalar_prefetch=1,
        grid=(n_heads,),
        in_specs=[head_col(0), head_col(n_heads), head_col(2 * n_heads),
                  pl.BlockSpec((1, 1, s_len), lambda h, t: (h, 0, 0)),
                  head_col(go)],
        out_specs=pl.BlockSpec((s_len, HEAD_DIM), lambda h, t: (0, h)),
        scratch_shapes=[pltpu.VMEM((s_len, 2 * HEAD_DIM), BF16),
                        pltpu.VMEM((HEAD_DIM, s_len), BF16),
                        pltpu.VMEM((tq, wmax), F32),
                        pltpu.VMEM((tq, wmax), F32),
                        pltpu.VMEM((tq, wmax), BF16),
                        pltpu.VMEM((tq, wmax), BF16),
                        pltpu.VMEM((tq, 1), F32),
                        pltpu.VMEM((tq, 1), F32),
                        pltpu.VMEM((s_len, 2 * HEAD_DIM), F32),
                        pltpu.VMEM((s_len, 1), F32),
                        pltpu.VMEM((s_len, 1), F32)])
    return pl.pallas_call(
        kern,
        grid_spec=grid_spec,
        out_shape=jax.ShapeDtypeStruct((s_len, n_heads * HEAD_DIM), BF16),
        compiler_params=_cparams(("arbitrary",)),
        name="fox_attn",
    )(tab, zqkv, zqkv, zqkv, crow, z)


HG_PROWS = 8 * HG_SUB + 8 * (HG_SUB // 2)


HG_HEADS_PER_STEP = 4


def _hgrn_kernel(q_ref, f_ref, i_ref, gate_ref, lb_ref, ng_ref, tb_ref, ones_ref, mask_ref, o_ref,
                 *scratch, layer, rows):
    @pl.when(pl.program_id(1) == 0)
    def _():
        scratch[0][...] = jnp.zeros_like(scratch[0])

    for hh in range(HG_HEADS_PER_STEP):
        lanes = pl.ds(hh * HEAD_DIM, HEAD_DIM)
        _hgrn_head(*(r.at[:, lanes] for r in (q_ref, f_ref, i_ref, gate_ref, lb_ref, ng_ref)),
                   tb_ref, ones_ref, mask_ref, o_ref.at[:, lanes], *(s.at[hh] for s in scratch),
                   layer=layer, rows=rows)


def _hgrn_head(q_ref, f_ref, i_ref, gate_ref, lb_ref, ng_ref, tb_ref, ones_ref,
               mask_ref, o_ref, st_ref, qt_ref, kt_ref, w_ref, cum_ref, dd_ref, p_ref, sc_ref,
               acc_ref, *, layer, rows):
    z = f_ref[...]
    ls = _log_sigmoid(z)
    if layer == 0:
        g = ls
        logk = ls - z
    else:
        lbp = lb_ref[...]
        e = jnp.exp(lbp - jnp.max(lbp, axis=0, keepdims=True))
        p = e / jnp.sum(e, axis=0, keepdims=True)
        lb = jnp.sum(p[1:layer + 1, :], axis=0, keepdims=True)
        a = jnp.log(lb)
        l1m = jnp.log1p(-lb)
        b = l1m + ls
        g = jnp.maximum(a, b) + jnp.log1p(jnp.exp(-jnp.abs(a - b)))
        logk = l1m + (ls - z)

    cums, tots = [], []
    for r0 in range(0, rows, LANES):
        ct = _dot01(tb_ref[...], g[r0:r0 + LANES, :])
        cums.append(ct[0:LANES, :])
        tots.append(ct[LANES:2 * LANES, :])
    cum = jnp.concatenate(cums, axis=0) * LOG2E
    tot = jnp.concatenate(tots, axis=0) * LOG2E
    w = cum - logk * LOG2E
    qt_ref[...] = (q_ref[...] * jnp.exp2(cum)).astype(BF16)
    kt_ref[...] = jnp.exp2(tot - w).astype(BF16)
    w_ref[...] = w
    cum_ref[...] = cum
    dd_ref[...] = jnp.exp2(tot)

    half = HG_SUB // 2
    n_groups = rows // HG_SUB

    def bcast_row(ref, r):
        return jnp.broadcast_to(ref[r:r + 1, :], (half, HEAD_DIM))

    for g_i in range(n_groups):
        r0 = g_i * HG_SUB
        p0 = g_i * HG_PROWS
        c_lo, c_hi = cum_ref[r0:r0 + half, :], cum_ref[r0 + half:r0 + HG_SUB, :]
        q_lo, q_hi = q_ref[r0:r0 + half, :], q_ref[r0 + half:r0 + HG_SUB, :]
        for s in range(half):
            w_s = bcast_row(w_ref, r0 + s)
            p_lo = q_lo * jnp.exp2(c_lo - w_s + mask_ref[s * half:(s + 1) * half, :])
            p_hi = q_hi * jnp.exp2(c_hi - w_s)
            p_ref[p0 + s * HG_SUB:p0 + (s + 1) * HG_SUB, :] = (
                jnp.concatenate([p_lo, p_hi], axis=0).astype(BF16))
        for s in range(0, half, 2):
            pa = q_hi * jnp.exp2(c_hi - bcast_row(w_ref, r0 + half + s)
                                + mask_ref[s * half:(s + 1) * half, :])
            pb = q_hi * jnp.exp2(c_hi - bcast_row(w_ref, r0 + half + s + 1)
                                + mask_ref[(s + 1) * half:(s + 2) * half, :])
            base = p0 + half * HG_SUB + s * half
            p_ref[base:base + HG_SUB, :] = jnp.concatenate([pa, pb], axis=0).astype(BF16)

    sc_ref[...] = _dot(p_ref[...], ones_ref[...])

    upds = [_dot_tn(i_ref[g_i * HG_SUB:(g_i + 1) * HG_SUB, :].astype(BF16),
                    kt_ref[g_i * HG_SUB:(g_i + 1) * HG_SUB, :]) for g_i in range(n_groups)]
    st = st_ref[...]
    for g_i in range(n_groups):
        r0 = g_i * HG_SUB
        p0 = g_i * HG_PROWS
        o_lo = jnp.zeros((half, HEAD_DIM), F32)
        o_hi = jnp.zeros((half, HEAD_DIM), F32)
        for s in range(half):
            v_s = bcast_row(i_ref, r0 + s)
            o_lo = o_lo + sc_ref[p0 + s * HG_SUB:p0 + s * HG_SUB + half, :] * v_s
            o_hi = o_hi + sc_ref[p0 + s * HG_SUB + half:p0 + (s + 1) * HG_SUB, :] * v_s
        for s in range(half):
            base = p0 + half * HG_SUB + s * half
            o_hi = o_hi + sc_ref[base:base + half, :] * bcast_row(i_ref, r0 + half + s)
        o_inter = _dot(qt_ref[r0:r0 + HG_SUB, :], st.T.astype(BF16))
        acc_ref[r0:r0 + HG_SUB, :] = o_inter + jnp.concatenate([o_lo, o_hi], axis=0)
        st = st * dd_ref[r0:r0 + 1, :] + upds[g_i]
    st_ref[...] = st

    o = acc_ref[...]
    ms = jnp.mean(o * o, axis=-1, keepdims=True)
    o = o * lax.rsqrt(ms + EPS) * ng_ref[...]
    o_ref[...] = (o * _silu(gate_ref[...])).astype(o_ref.dtype)


def _hgrn(z, hg_lb, norm_g, layer, q_off, f_off, i_off, g_off, n_heads, rows=512):
    s_len = z.shape[0]
    depth = hg_lb.shape[0]
    hps = HG_HEADS_PER_STEP
    width = hps * HEAD_DIM
    assert n_heads % hps == 0 and all(o % width == 0 for o in (q_off, f_off, i_off, g_off))
    qo, fo, io, go = (o // width for o in (q_off, f_off, i_off, g_off))
    r = np.arange(LANES)
    same = (r[:, None] // HG_SUB) == (r[None, :] // HG_SUB)
    tb = jnp.asarray(np.concatenate([same & (r[None, :] <= r[:, None]), same]).astype(np.float32), BF16)
    ones = jnp.ones((HEAD_DIM, HEAD_DIM), BF16)
    half = HG_SUB // 2
    t_idx = np.arange(half)
    mask_np = np.where(t_idx[None, :, None] >= t_idx[:, None, None], 0.0, NEG_BIG)
    mask = jnp.asarray(np.broadcast_to(mask_np, (half, half, HEAD_DIM)).reshape(half * half, HEAD_DIM), F32)
    n_prows = (rows // HG_SUB) * HG_PROWS
    kern = functools.partial(_hgrn_kernel, layer=layer, rows=rows)
    blk_spec = lambda off: pl.BlockSpec((rows, width), lambda h, i: (i, off + h))
    const = lambda shape: pl.BlockSpec(shape, lambda h, i: (0, 0))
    per_head = lambda shape, dtype: pltpu.VMEM((hps,) + shape, dtype)
    return pl.pallas_call(
        kern,
        grid=(n_heads // hps, s_len // rows),
        in_specs=[blk_spec(qo), blk_spec(fo), blk_spec(io), blk_spec(go),
                  pl.BlockSpec((depth, width), lambda h, i: (0, h)),
                  pl.BlockSpec((1, width), lambda h, i: (0, h)),
                  const((2 * LANES, LANES)), const((HEAD_DIM, HEAD_DIM)),
                  const((half * half, HEAD_DIM))],
        out_specs=pl.BlockSpec((rows, width), lambda h, i: (i, h)),
        out_shape=jax.ShapeDtypeStruct((s_len, n_heads * HEAD_DIM), BF16),
        scratch_shapes=[per_head((HEAD_DIM, HEAD_DIM), F32),
                        per_head((rows, HEAD_DIM), BF16),
                        per_head((rows, HEAD_DIM), BF16),
                        per_head((rows, HEAD_DIM), F32),
                        per_head((rows, HEAD_DIM), F32),
                        per_head((rows, HEAD_DIM), F32),
                        per_head((n_prows, HEAD_DIM), BF16),
                        per_head((n_prows, HEAD_DIM), F32),
                        per_head((rows, HEAD_DIM), F32)],
        compiler_params=_cparams(("parallel", "arbitrary")),
        name="hgrn2",
    )(z, z, z, z, hg_lb, norm_g.reshape(1, -1), tb, ones, mask)


def _gelu_tanh(x):
    c = math.sqrt(2.0 / math.pi)
    return 0.5 * x * (1.0 + jnp.tanh(c * (x + 0.044715 * (x * x * x))))


def _s5_kernel(u_ref, perm_ref, permt_ref, bm_ref, cm_ref, are_ref, aim_ref, pw_ref,
               d_ref, o_ref, state_ref, x_ref, yp_ref, *, rows, half):
    @pl.when(pl.program_id(0) == 0)
    def _():
        state_ref[...] = jnp.zeros_like(state_ref)

    nblk = bm_ref.shape[0]
    nt = rows // 8
    u = u_ref[...]
    up = _dot(perm_ref[...], u.astype(BF16)).astype(BF16)
    sub = lax.broadcasted_iota(jnp.int32, (8, half), 0)

    re, im = slice(0, half), slice(half, 2 * half)
    for b in range(nblk):
        x_ref[b] = _dot(up[:, b * LANES:(b + 1) * LANES], bm_ref[b])
    for b in range(nblk):
        are = are_ref[b]
        aim = aim_ref[b]
        xr = jnp.zeros((8, half), F32)
        xi = jnp.zeros((8, half), F32)
        for t in range(nt):
            rs = slice(t * 8, (t + 1) * 8)
            xr, xi = (are * xr - aim * xi + x_ref[b, rs, re],
                      are * xi + aim * xr + x_ref[b, rs, im])
            x_ref[b, rs, re] = xr
            x_ref[b, rs, im] = xi

        er, ei = xr, xi
        alre = pw_ref[b, rows - 1:rows, re]
        alim = pw_ref[b, rows - 1:rows, im]
        cr = state_ref[b, 0:1, re]
        ci = state_ref[b, 0:1, im]
        ctr = jnp.zeros((8, half), F32)
        cti = jnp.zeros((8, half), F32)
        for s in range(8):
            ctr = jnp.where(sub == s, cr, ctr)
            cti = jnp.where(sub == s, ci, cti)
            cr, ci = (alre * cr - alim * ci + er[s:s + 1, :],
                      alre * ci + alim * cr + ei[s:s + 1, :])
        state_ref[b, :, re] = jnp.broadcast_to(cr, (8, half))
        state_ref[b, :, im] = jnp.broadcast_to(ci, (8, half))

        for t in range(nt):
            rs = slice(t * 8, (t + 1) * 8)
            pr = pw_ref[b, rs, re]
            pi = pw_ref[b, rs, im]
            x_ref[b, rs, re] = x_ref[b, rs, re] + (pr * ctr - pi * cti)
            x_ref[b, rs, im] = x_ref[b, rs, im] + (pr * cti + pi * ctr)
        yp_ref[:, b * LANES:(b + 1) * LANES] = _dot(x_ref[b].astype(BF16), cm_ref[b])

    y = _dot01(permt_ref[...], yp_ref[...], terms=2) + d_ref[...] * u
    o_ref[...] = _gelu_tanh(y).astype(o_ref.dtype)


def _s5(z, u_off, bm, cm, are, aim, pw, d_skip, rows=256):
    s_len = z.shape[0]
    width = d_skip.shape[0]
    nblk, _, two_half = bm.shape
    half = two_half // 2
    seg = rows // 8
    rho = np.arange(rows)
    t_of = (rho % 8) * seg + rho // 8
    perm_np = np.zeros((rows, rows), np.float32)
    perm_np[rho, t_of] = 1.0
    perm = jnp.asarray(perm_np, BF16)
    permt = jnp.asarray(perm_np.T, BF16)
    uo = u_off // width
    kern = functools.partial(_s5_kernel, rows=rows, half=half)
    c2 = lambda shape: pl.BlockSpec(shape, lambda i: (0, 0))
    c3 = lambda shape: pl.BlockSpec(shape, lambda i: (0, 0, 0))
    return pl.pallas_call(
        kern,
        grid=(s_len // rows,),
        in_specs=[pl.BlockSpec((rows, width), lambda i: (i, uo)),
                  c2((rows, rows)), c2((rows, rows)),
                  c3(bm.shape), c3(cm.shape),
                  c3(are.shape), c3(aim.shape),
                  pl.BlockSpec(pw.shape, lambda i: (0, 0, 0), pipeline_mode=pl.Buffered(1)),
                  c2((1, width))],
        out_specs=pl.BlockSpec((rows, width), lambda i: (i, 0)),
        out_shape=jax.ShapeDtypeStruct((s_len, width), BF16),
        scratch_shapes=[pltpu.VMEM((nblk, 8, two_half), F32),
                        pltpu.VMEM((nblk, rows, two_half), F32),
                        pltpu.VMEM((rows, width), F32)],
        compiler_params=_cparams(("arbitrary",)),
        name="s5",
    )(z, perm, permt, bm, cm, are, aim, pw, d_skip.reshape(1, width))


def _s5_params(a_re, a_im, log_dt, b_re, b_im, c_re, c_im, seg):
    g_n, p_n = a_re.shape
    nblk = g_n // S5_GPB
    dt = jnp.exp(log_dt)[:, None]
    mag = jnp.exp(a_re * dt)
    ang = a_im * dt
    abar_re = mag * jnp.cos(ang)
    abar_im = mag * jnp.sin(ang)
    nr = abar_re - 1.0
    den = a_re * a_re + a_im * a_im
    zr = (nr * a_re + abar_im * a_im) / den
    zi = (abar_im * a_re - nr * a_im) / den
    bbar_re = zr[:, :, None] * b_re - zi[:, :, None] * b_im
    bbar_im = zr[:, :, None] * b_im + zi[:, :, None] * b_re
    same = jnp.asarray(np.eye(S5_GPB, dtype=bool))

    def embed_b(bb):
        bb = bb.reshape(nblk, S5_GPB, 1, p_n, S5_GROUP).transpose(0, 1, 4, 2, 3)
        m = jnp.where(same[None, :, None, :, None], bb, 0.0)
        return m.reshape(nblk, S5_GPB * S5_GROUP, S5_GPB * p_n)

    bm = jnp.concatenate([embed_b(bbar_re), embed_b(bbar_im)], axis=2).astype(BF16)

    def embed_c(cc):
        cc = cc.reshape(nblk, S5_GPB, S5_GROUP, 1, p_n).transpose(0, 1, 4, 3, 2)
        m = jnp.where(same[None, :, None, :, None], cc, 0.0)
        return m.reshape(nblk, S5_GPB * p_n, S5_GPB * S5_GROUP)

    cm = jnp.concatenate([embed_c(c_re), embed_c(-c_im)], axis=1).astype(BF16)

    def tile8(v):
        v = v.reshape(nblk, 1, S5_GPB * p_n)
        return jnp.broadcast_to(v, (nblk, 8, S5_GPB * p_n))

    k = jnp.arange(1, seg + 1, dtype=F32)[:, None, None]
    mag_k = jnp.exp(k * (a_re * dt))
    ang_k = k * ang

    def table(t):
        t = t.reshape(seg, 1, nblk, S5_GPB * p_n)
        return jnp.broadcast_to(t, (seg, 8, nblk, S5_GPB * p_n)).transpose(2, 0, 1, 3).reshape(
            nblk, 8 * seg, S5_GPB * p_n)

    pw = jnp.concatenate([table(mag_k * jnp.cos(ang_k)), table(mag_k * jnp.sin(ang_k))], axis=2)
    return bm, cm, tile8(abar_re), tile8(abar_im), pw


def _glu_kernel(y_ref, w_ref, gate_ref, o_ref, *, width):
    zg = _dot(y_ref[...], w_ref[...])
    o = zg[:, :width] * _sigmoid(zg[:, width:]) * _silu(gate_ref[...])
    o_ref[...] = o.astype(o_ref.dtype)


def _glu(y, w_glu, layer, z, g_off, tm=1024):
    m, width = y.shape
    go = g_off // width
    return pl.pallas_call(
        functools.partial(_glu_kernel, width=width),
        grid=(m // tm,),
        in_specs=[pl.BlockSpec((tm, width), lambda i: (i, 0)),
                  pl.BlockSpec((None, width, 2 * width), lambda i: (layer, 0, 0)),
                  pl.BlockSpec((tm, width), lambda i: (i, go))],
        out_specs=pl.BlockSpec((tm, width), lambda i: (i, 0)),
        out_shape=jax.ShapeDtypeStruct((m, width), BF16),
        compiler_params=_cparams(("parallel",)),
        name="s5_glu",
    )(y, w_glu, z)


def _merge_kernel(oa_ref, ob_ref, oc_ref, ga_ref, gb_ref, gc_ref, bg_ref, wa_ref, wb_ref, wc_ref,
                  wo_ref, x_ref, ng_ref, xo_ref, ho_ref, *, d):
    def gate(g_ref, i):
        return _sigmoid(g_ref[...] + bg_ref[:, i * d:(i + 1) * d])

    merged = gate(ga_ref, 0) * _dot(oa_ref[...], wa_ref[...])
    merged = merged + gate(gb_ref, 1) * _dot(ob_ref[...], wb_ref[...])
    merged = merged + gate(gc_ref, 2) * _dot(oc_ref[...], wc_ref[...])
    xn = x_ref[...] + _dot(merged.astype(BF16), wo_ref[...])
    xo_ref[...] = xn
    ms = jnp.mean(xn * xn, axis=-1, keepdims=True)
    ho_ref[...] = (xn * lax.rsqrt(ms + EPS) * ng_ref[...]).astype(ho_ref.dtype)


def _merge(oa, ob, oc, z, mg_off, b_gate, wa, wb, wc, wo, layer, x, next_g, h_dtype, tm=256):
    m, d = x.shape
    w = oa.shape[1]
    assert mg_off % d == 0
    mo = mg_off // d
    row = lambda width: pl.BlockSpec((tm, width), lambda i: (i, 0))
    gate = lambda k: pl.BlockSpec((tm, d), lambda i: (i, mo + k))
    resident = lambda shape: pl.BlockSpec(shape, lambda i: (0, 0), pipeline_mode=pl.Buffered(1))
    weight = lambda rows: pl.BlockSpec((None, rows, d), lambda i: (layer, 0, 0),
                                       pipeline_mode=pl.Buffered(1))
    return pl.pallas_call(
        functools.partial(_merge_kernel, d=d),
        grid=(m // tm,),
        in_specs=[row(w), row(w), row(w), gate(0), gate(1), gate(2),
                  resident((1, N_BRANCH * d)),
                  weight(w), weight(w), weight(w), weight(d),
                  row(d), resident((1, d))],
        out_specs=[row(d), row(d)],
        out_shape=[jax.ShapeDtypeStruct((m, d), F32), jax.ShapeDtypeStruct((m, d), h_dtype)],
        compiler_params=_cparams(("parallel",)),
        name="merge_out",
    )(oa, ob, oc, z, z, z, b_gate.reshape(1, -1), wa, wb, wc, wo, x, next_g.reshape(1, -1))


def kernel(x, norm_g, w_in, b_gate, fox_bf, hg_lb, hg_norm_g, s5_a_re, s5_a_im, s5_log_dt,
           s5_b_re, s5_b_im, s5_c_re, s5_c_im, s5_d, s5_w_glu, w_br_a, w_br_b, w_br_c, w_out,
           final_g):
    bsz, s_len, d = x.shape
    depth = w_in.shape[0]
    hg_w = hg_lb.shape[1]
    s5_w = s5_d.shape[1]
    n_fox = fox_bf.shape[1]
    fox_w = n_fox * HEAD_DIM
    n_hg = hg_w // HEAD_DIM
    sizes = (hg_w, hg_w, hg_w, hg_w, s5_w, s5_w, fox_w, fox_w, fox_w, n_fox, fox_w, N_BRANCH * d)
    offs = np.concatenate([[0], np.cumsum(sizes)])
    (o_hq, o_hf, o_hi, o_hg, o_su, o_sg, o_fq, o_fk, o_fv, o_ff, o_fg, o_mg, o_end) = (int(v) for v in offs)
    wt = jnp.transpose(w_in, (0, 2, 1))
    tn = fox_w
    tiles = lambda start, stop: list(range(start, stop, tn))
    o_mg2, o_fg2 = 0, o_end - o_mg
    w_glu, w_a, w_b, w_c, w_o = (w.astype(BF16) for w in (s5_w_glu, w_br_a, w_br_b, w_br_c, w_out))
    s5_rows = 256
    outs = []
    for b in range(bsz):
        xb = x[b]
        h = _rmsnorm(xb, norm_g[0], BF16)
        for l in range(depth):
            z = _inproj(h, wt, l, tiles(0, o_fq), F32, tn=tn)
            zqkv = _inproj(h, wt, l, tiles(o_fq, o_ff), BF16,
                           first_tile_scale=HEAD_DIM ** -0.5 * LOG2E, tn=tn)
            zg = _inproj(h, wt, l, tiles(o_mg, o_end) + tiles(o_fg, o_mg), F32, tn=tn)

            o_a = _hgrn(z, hg_lb, hg_norm_g[l], l, o_hq, o_hf, o_hi, o_hg, n_hg)

            s5p = _s5_params(s5_a_re[l], s5_a_im[l], s5_log_dt[l], s5_b_re[l], s5_b_im[l],
                             s5_c_re[l], s5_c_im[l], s5_rows // 8)
            y_b = _s5(z, o_su, *s5p, s5_d[l], rows=s5_rows)
            o_b = _glu(y_b, w_glu, l, z, o_sg)

            ct = _fox_cumlog(h, wt, l, o_ff, fox_bf[l])
            o_c = _fox_attn(zqkv, zg, ct[:, None, :], o_fg2, n_fox)

            last = l == depth - 1
            next_g = final_g if last else norm_g[l + 1]
            xb, h = _merge(o_a, o_b, o_c, zg, o_mg2, b_gate[l], w_a, w_b, w_c, w_o, l,
                           xb, next_g, F32 if last else BF16)
        outs.append(h)
    return outs[0][None] if bsz == 1 else jnp.stack(outs, axis=0)
```
